```python
import math
import jax, jax.numpy as jnp
from jax import lax
import numpy as np

D_MODEL = 2048
BATCH = 4
SEQ = 4096
DEPTH = 1

HEAD_DIM = 128
A_Q_HEADS = 8
A_KV_HEADS = 2
A_WINDOW = 128
A_BLOCK = 128
B_HEADS = 8
GRID_W = 64
B_WIN_R = 8
B_WIN_C = 16
B_QBLK_R = 8
B_QBLK_C = 16
D_FF = 5632
MACARON_W = 0.5
RMS_EPS = 1e-6
NEG_INF = -1e30

A_Q_W = A_Q_HEADS * HEAD_DIM
A_KV_W = A_KV_HEADS * HEAD_DIM
B_W = B_HEADS * HEAD_DIM
IN_PROJ_W = A_Q_W + 2 * A_KV_W + 3 * B_W + 2 * D_MODEL

kernel_name = "hybrid_window_gqa_natten2d_macaron"


def rms_norm(x, g):
    xf = x.astype(jnp.float32)
    y = xf * lax.rsqrt(jnp.mean(xf * xf, axis=-1, keepdims=True) + RMS_EPS)
    return (y * g.astype(jnp.float32)).astype(x.dtype)


def swiglu_ffn(x, w_in, w_out):
    g, u = jnp.split(x @ w_in, 2, axis=-1)
    return (jax.nn.silu(g) * u) @ w_out


def alibi_slopes(n_heads):
    return np.array([2.0 ** (-8.0 * (i + 1) / n_heads) for i in range(n_heads)], dtype=np.float32)


def windowed_gqa(q, k, v, sink):
    b, s, hq, d = q.shape
    hkv = k.shape[2]
    grp = hq // hkv
    nb = s // A_BLOCK
    nshift = 1 + 2 * (A_WINDOW // A_BLOCK)
    span = nshift * A_BLOCK
    pad = ((0, 0), (A_WINDOW, A_WINDOW), (0, 0), (0, 0))
    kp = jnp.pad(k, pad)
    vp = jnp.pad(v, pad)
    kb = jnp.concatenate([kp[:, j * A_BLOCK: j * A_BLOCK + s].reshape(b, nb, A_BLOCK, hkv, d)
                          for j in range(nshift)], axis=2)
    vb = jnp.concatenate([vp[:, j * A_BLOCK: j * A_BLOCK + s].reshape(b, nb, A_BLOCK, hkv, d)
                          for j in range(nshift)], axis=2)
    qb = q.reshape(b, nb, A_BLOCK, hkv, grp, d)
    scores = jnp.einsum('bnqkgd,bnskd->bnkgqs', qb, kb).astype(jnp.float32) * (d ** -0.5)
    a = np.arange(A_BLOCK)[:, None]
    jj = np.arange(span)[None, :]
    absd = np.abs(a + A_WINDOW - jj).astype(np.float32)
    key_pos = np.arange(nb)[:, None, None] * A_BLOCK - A_WINDOW + jj[None]
    valid = (absd[None] <= A_WINDOW) & (key_pos >= 0) & (key_pos < s)
    slopes = alibi_slopes(hq).reshape(hkv, grp, 1, 1)
    bias = jnp.asarray(-slopes * absd[None, None])
    logits = jnp.where(jnp.asarray(valid)[None, :, None, None], scores + bias[None, None], NEG_INF)
    sink_l = jnp.broadcast_to(sink.astype(jnp.float32).reshape(1, 1, hkv, grp, 1, 1),
                              logits.shape[:-1] + (1,))
    p = jax.nn.softmax(jnp.concatenate([logits, sink_l], axis=-1), axis=-1)[..., :-1]
    out = jnp.einsum('bnkgqs,bnskd->bnqkgd', p.astype(v.dtype), vb)
    return out.reshape(b, s, hq * d)


def neighbourhood_attention_2d(q, k, v, rpb):
    b, s, h, d = q.shape
    rows = s // GRID_W
    kr = min(B_WIN_R, rows)
    kc = min(B_WIN_C, GRID_W)
    qr = math.gcd(rows, B_QBLK_R)
    qc = math.gcd(GRID_W, B_QBLK_C)
    lr = min(qr + kr - 1, rows)
    lc = min(qc + kc - 1, GRID_W)
    nrb = rows // qr
    ncb = GRID_W // qc
    rs = np.clip(np.arange(rows) - kr // 2, 0, rows - kr)
    cs = np.clip(np.arange(GRID_W) - kc // 2, 0, GRID_W - kc)
    key_rows = np.minimum(rs[::qr], rows - lr)[:, None] + np.arange(lr)
    key_cols = np.minimum(cs[::qc], GRID_W - lc)[:, None] + np.arange(lc)
    q_rows = np.arange(nrb)[:, None] * qr + np.arange(qr)
    q_cols = np.arange(ncb)[:, None] * qc + np.arange(qc)
    dr = key_rows[:, None, :] - q_rows[:, :, None]
    dc = key_cols[:, None, :] - q_cols[:, :, None]
    rstart = rs[q_rows][:, :, None]
    cstart = cs[q_cols][:, :, None]
    vr = (key_rows[:, None, :] >= rstart) & (key_rows[:, None, :] < rstart + kr)
    vc = (key_cols[:, None, :] >= cstart) & (key_cols[:, None, :] < cstart + kc)
    full = (nrb, ncb, qr, qc, lr, lc)
    valid = (vr[:, None, :, None, :, None] & vc[None, :, None, :, None, :]).reshape(nrb, ncb, qr * qc, lr * lc)
    ridx = np.broadcast_to(np.clip(dr + B_WIN_R - 1, 0, 2 * B_WIN_R - 2)[:, None, :, None, :, None], full)
    cidx = np.broadcast_to(np.clip(dc + B_WIN_C - 1, 0, 2 * B_WIN_C - 2)[None, :, None, :, None, :], full)
    ridx = ridx.reshape(nrb, ncb, qr * qc, lr * lc)
    cidx = cidx.reshape(nrb, ncb, qr * qc, lr * lc)
    bias = rpb.astype(jnp.float32)[:, ridx, cidx].transpose(1, 2, 0, 3, 4)

    def gather_blocks(t):
        t5 = t.reshape(b, rows, GRID_W, h, d)
        tr = jnp.take(t5, key_rows.reshape(-1), axis=1).reshape(b, nrb, lr, GRID_W, h, d)
        tb = jnp.take(tr, key_cols.reshape(-1), axis=3).reshape(b, nrb, lr, ncb, lc, h, d)
        return tb.transpose(0, 1, 3, 2, 4, 5, 6).reshape(b, nrb, ncb, lr * lc, h, d)

    qg = q.reshape(b, nrb, qr, ncb, qc, h, d).transpose(0, 1, 3, 2, 4, 5, 6).reshape(b, nrb, ncb, qr * qc, h, d)
    kg = gather_blocks(k)
    vg = gather_blocks(v)
    scores = jnp.einsum('bijqhd,bijkhd->bijhqk', qg, kg).astype(jnp.float32) * (d ** -0.5)
    logits = jnp.where(jnp.asarray(valid)[None, :, :, None], scores + bias[None], NEG_INF)
    p = jax.nn.softmax(logits, axis=-1)
    out = jnp.einsum('bijhqk,bijkhd->bijqhd', p.astype(v.dtype), vg)
    out = out.reshape(b, nrb, ncb, qr, qc, h * d).transpose(0, 1, 3, 2, 4, 5)
    return out.reshape(b, s, h * d)


def hybrid_mixer(u, w_in, b_gate, sink_a, rpb_b, w_up_a, w_up_b, w_out):
    b, s, _ = u.shape
    cuts = np.cumsum([A_Q_W, A_KV_W, A_KV_W, B_W, B_W, B_W, D_MODEL]).tolist()
    qa, ka, va, qb, kb, vb, ga, gb = jnp.split(u @ w_in, cuts, axis=-1)
    bga, bgb = jnp.split(b_gate, 2)
    ya = windowed_gqa(qa.reshape(b, s, A_Q_HEADS, HEAD_DIM),
                      ka.reshape(b, s, A_KV_HEADS, HEAD_DIM),
                      va.reshape(b, s, A_KV_HEADS, HEAD_DIM), sink_a) @ w_up_a
    yb = neighbourhood_attention_2d(qb.reshape(b, s, B_HEADS, HEAD_DIM),
                                    kb.reshape(b, s, B_HEADS, HEAD_DIM),
                                    vb.reshape(b, s, B_HEADS, HEAD_DIM), rpb_b) @ w_up_b
    m = jax.nn.sigmoid(ga + bga) * ya + jax.nn.sigmoid(gb + bgb) * yb
    return m @ w_out


def setup_inputs(seed: int = 0) -> dict:
    key = jax.random.key(seed)
    ks = jax.random.split(key, 20)
    f32 = jnp.float32

    def w(k, shape, fan_in):
        return jax.random.normal(k, shape, f32) * (fan_in ** -0.5)

    def gain(k):
        return 1.0 + 0.05 * jax.random.normal(k, (DEPTH, D_MODEL), f32)

    return {
        "x": jax.random.normal(ks[0], (BATCH, SEQ, D_MODEL), f32),
        "ffn1_pre_g": gain(ks[1]),
        "ffn1_w_in": w(ks[2], (DEPTH, D_MODEL, 2 * D_FF), D_MODEL),
        "ffn1_w_out": w(ks[3], (DEPTH, D_FF, D_MODEL), D_FF),
        "ffn1_post_g": gain(ks[4]),
        "mix_pre_g": gain(ks[5]),
        "w_in": w(ks[6], (DEPTH, D_MODEL, IN_PROJ_W), D_MODEL),
        "b_gate": 0.01 * jax.random.normal(ks[7], (DEPTH, 2 * D_MODEL), f32),
        "sink_a": jax.random.normal(ks[8], (DEPTH, A_Q_HEADS), f32),
        "rpb_b": 0.02 * jax.random.normal(ks[9], (DEPTH, B_HEADS, 2 * B_WIN_R - 1, 2 * B_WIN_C - 1), f32),
        "w_up_a": w(ks[10], (DEPTH, A_Q_W, D_MODEL), A_Q_W),
        "w_up_b": w(ks[11], (DEPTH, B_W, D_MODEL), B_W),
        "w_out": w(ks[12], (DEPTH, D_MODEL, D_MODEL), D_MODEL),
        "mix_post_g": gain(ks[13]),
        "ffn2_pre_g": gain(ks[14]),
        "ffn2_w_in": w(ks[15], (DEPTH, D_MODEL, 2 * D_FF), D_MODEL),
        "ffn2_w_out": w(ks[16], (DEPTH, D_FF, D_MODEL), D_FF),
        "ffn2_post_g": gain(ks[17]),
    }


def reference(x, ffn1_pre_g, ffn1_w_in, ffn1_w_out, ffn1_post_g, mix_pre_g, w_in, b_gate, sink_a,
              rpb_b, w_up_a, w_up_b, w_out, mix_post_g, ffn2_pre_g, ffn2_w_in, ffn2_w_out, ffn2_post_g):
    h = x
    for l in range(DEPTH):
        f1 = swiglu_ffn(rms_norm(h, ffn1_pre_g[l]), ffn1_w_in[l], ffn1_w_out[l])
        h = h + MACARON_W * rms_norm(f1, ffn1_post_g[l])
        mix = hybrid_mixer(rms_norm(h, mix_pre_g[l]), w_in[l], b_gate[l], sink_a[l], rpb_b[l],
                           w_up_a[l], w_up_b[l], w_out[l])
        h = h + rms_norm(mix, mix_post_g[l])
        f2 = swiglu_ffn(rms_norm(h, ffn2_pre_g[l]), ffn2_w_in[l], ffn2_w_out[l])
        h = h + MACARON_W * rms_norm(f2, ffn2_post_g[l])
    return h
```

```python
import functools
import math

import jax
import jax.numpy as jnp
import numpy as np
from jax import lax
from jax.experimental import pallas as pl
from jax.experimental.pallas import tpu as pltpu

D_MODEL = 2048
SEQ = 4096
HEAD_DIM = 128
A_Q_HEADS = 8
A_KV_HEADS = 2
A_GROUP = A_Q_HEADS // A_KV_HEADS
A_WINDOW = 128
A_BLOCK = 128
B_HEADS = 8
GRID_W = 64
GRID_ROWS = SEQ // GRID_W
B_WIN_R = 8
B_WIN_C = 16
D_FF = 5632
MACARON_W = 0.5
RMS_EPS = 1e-6
NEG_INF = -1e30
SCALE = HEAD_DIM ** -0.5

A_Q_W = A_Q_HEADS * HEAD_DIM
A_KV_W = A_KV_HEADS * HEAD_DIM
B_W = B_HEADS * HEAD_DIM
QKV_W = A_Q_W + 2 * A_KV_W + 3 * B_W

F32 = jnp.float32
BF16 = jnp.bfloat16

VMEM_LIMIT_BYTES = 56 * 1024 * 1024

FFN_TM = 512
FFN_FC = 512
QKV_TM = 1024
QKV_TN = 512
MIX_TM = 512
MIX_TC = 512
NORM_ROWS = 32

A_QB = 4
A_STEP = A_QB * A_BLOCK
B_PAIR_ROWS = 2
B_PAIR_TOK = B_PAIR_ROWS * GRID_W
B_KEY_ROWS = 10
B_KEY_TOK = B_KEY_ROWS * GRID_W
B_PAIRS_PER_STEP = 4
B_STEP = B_PAIRS_PER_STEP * B_PAIR_TOK
B_N_PAIRS = GRID_ROWS // B_PAIR_ROWS
B_VARIANT_PAIRS = (0, 1, 2, B_N_PAIRS - 2, B_N_PAIRS - 1)


def _alibi_slopes(n_heads):
    return [2.0 ** (-8.0 * (i + 1) / n_heads) for i in range(n_heads)]


def _compiler_params(n_axes):
    return pltpu.CompilerParams(dimension_semantics=("arbitrary",) * n_axes,
                                vmem_limit_bytes=VMEM_LIMIT_BYTES)


def _rms_scale(x):
    return x * lax.rsqrt(jnp.mean(x * x, axis=-1, keepdims=True) + RMS_EPS)


def _norm_rows_to(src_ref, gain_ref, dst_ref, rows):
    def body(c, carry):
        r = pl.multiple_of(c * NORM_ROWS, NORM_ROWS)
        x = src_ref[pl.ds(r, NORM_ROWS), :]
        dst_ref[pl.ds(r, NORM_ROWS), :] = (_rms_scale(x) * gain_ref[...]).astype(dst_ref.dtype)
        return carry
    lax.fori_loop(0, rows // NORM_ROWS, body, 0)


def _residual_norm_rows(res_ref, gain_ref, acc_ref, rows, weight):
    def body(c, carry):
        r = pl.multiple_of(c * NORM_ROWS, NORM_ROWS)
        f = acc_ref[pl.ds(r, NORM_ROWS), :]
        acc_ref[pl.ds(r, NORM_ROWS), :] = (res_ref[pl.ds(r, NORM_ROWS), :]
                                          + weight * (_rms_scale(f) * gain_ref[...]))
        return carry
    lax.fori_loop(0, rows // NORM_ROWS, body, 0)


def _ffn_kernel(x_ref, pre_g_ref, wg_ref, wu_ref, wo_ref, post_g_ref, o_ref, n_ref):
    j = pl.program_id(1)

    @pl.when(j == 0)
    def _():
        _norm_rows_to(x_ref, pre_g_ref, n_ref, FFN_TM)
        o_ref[...] = jnp.zeros_like(o_ref)

    n = n_ref[...]
    g = jnp.dot(n, wg_ref[...], preferred_element_type=F32)
    u = jnp.dot(n, wu_ref[...], preferred_element_type=F32)
    a = (g * jax.nn.sigmoid(g) * u).astype(BF16)
    o_ref[...] += jnp.dot(a, wo_ref[...], preferred_element_type=F32)

    @pl.when(j == pl.num_programs(1) - 1)
    def _():
        _residual_norm_rows(x_ref, post_g_ref, o_ref, FFN_TM, MACARON_W)


def _ffn(x, pre_g, w_in, w_out, post_g):
    t = x.shape[0]
    nf = D_FF // FFN_FC
    return pl.pallas_call(
        _ffn_kernel,
        grid=(t // FFN_TM, nf),
        in_specs=[
            pl.BlockSpec((FFN_TM, D_MODEL), lambda i, j: (i, 0)),
            pl.BlockSpec((1, D_MODEL), lambda i, j: (0, 0)),
            pl.BlockSpec((D_MODEL, FFN_FC), lambda i, j: (0, j)),
            pl.BlockSpec((D_MODEL, FFN_FC), lambda i, j: (0, j + nf)),
            pl.BlockSpec((FFN_FC, D_MODEL), lambda i, j: (j, 0)),
            pl.BlockSpec((1, D_MODEL), lambda i, j: (0, 0)),
        ],
        out_specs=pl.BlockSpec((FFN_TM, D_MODEL), lambda i, j: (i, 0)),
        out_shape=jax.ShapeDtypeStruct((t, D_MODEL), F32),
        scratch_shapes=[pltpu.VMEM((FFN_TM, D_MODEL), BF16)],
        compiler_params=_compiler_params(2),
        name="ffn",
    )(x, pre_g, w_in, w_in, w_out, post_g)


def _qkv_kernel(h_ref, g_ref, w_ref, u_ref, o_ref):
    @pl.when(pl.program_id(1) == 0)
    def _():
        _norm_rows_to(h_ref, g_ref, u_ref, QKV_TM)

    o_ref[...] = jnp.dot(u_ref[...], w_ref[...], preferred_element_type=F32).astype(BF16)


def _qkv_proj(h, gain, w_qkv):
    t = h.shape[0]
    return pl.pallas_call(
        _qkv_kernel,
        grid=(t // QKV_TM, QKV_W // QKV_TN),
        in_specs=[
            pl.BlockSpec((QKV_TM, D_MODEL), lambda i, j: (i, 0)),
            pl.BlockSpec((1, D_MODEL), lambda i, j: (0, 0)),
            pl.BlockSpec((D_MODEL, QKV_TN), lambda i, j: (0, j)),
        ],
        out_specs=[
            pl.BlockSpec((QKV_TM, D_MODEL), lambda i, j: (i, 0)),
            pl.BlockSpec((QKV_TM, QKV_TN), lambda i, j: (i, j)),
        ],
        out_shape=[jax.ShapeDtypeStruct((t, D_MODEL), BF16),
                   jax.ShapeDtypeStruct((t, QKV_W), BF16)],
        compiler_params=_compiler_params(2),
        name="qkv_proj",
    )(h, gain, w_qkv)


def _attn_win_kernel(sink_ref, q_ref, kp_ref, kc_ref, kn_ref, vp_ref, vc_ref, vn_ref, o_ref,
                     k_scr, v_scr):
    m = pl.program_id(1)
    k_scr[0:A_BLOCK, :] = kp_ref[...]
    k_scr[A_BLOCK:A_BLOCK + A_STEP, :] = kc_ref[...]
    k_scr[A_BLOCK + A_STEP:, :] = kn_ref[...]
    v_scr[0:A_BLOCK, :] = vp_ref[...]
    v_scr[A_BLOCK:A_BLOCK + A_STEP, :] = vc_ref[...]
    v_scr[A_BLOCK + A_STEP:, :] = vn_ref[...]

    span = 3 * A_BLOCK
    qi = lax.broadcasted_iota(jnp.int32, (A_BLOCK, span), 0)
    kj = lax.broadcasted_iota(jnp.int32, (A_BLOCK, span), 1)
    absd_i = jnp.abs(qi + A_WINDOW - kj)
    absd = absd_i.astype(F32)
    in_window = absd_i <= A_WINDOW
    slopes = _alibi_slopes(A_Q_HEADS)

    def body(t, carry):
        r = pl.multiple_of(t * A_BLOCK, A_BLOCK)
        key_pos = (m * A_QB + t) * A_BLOCK - A_WINDOW + kj
        valid = in_window & (key_pos >= 0) & (key_pos < SEQ)
        for g in range(A_KV_HEADS):
            k = k_scr[pl.ds(r, span), g * HEAD_DIM:(g + 1) * HEAD_DIM]
            v = v_scr[pl.ds(r, span), g * HEAD_DIM:(g + 1) * HEAD_DIM]
            heads = [g * A_GROUP + e for e in range(A_GROUP)]
            qs = jnp.concatenate(
                [q_ref[pl.ds(r, A_BLOCK), h * HEAD_DIM:(h + 1) * HEAD_DIM] for h in heads], axis=0)
            s = lax.dot_general(qs, k, (((1,), (1,)), ((), ())), preferred_element_type=F32)
            probs, dens = [], []
            for e, h in enumerate(heads):
                logits = s[e * A_BLOCK:(e + 1) * A_BLOCK, :] * SCALE + (-slopes[h]) * absd
                logits = jnp.where(valid, logits, NEG_INF)
                sink = sink_ref[h]
                mx = jnp.maximum(jnp.max(logits, axis=-1, keepdims=True), sink)
                p = jnp.exp(logits - mx)
                dens.append(jnp.sum(p, axis=-1, keepdims=True) + jnp.exp(sink - mx))
                probs.append(p.astype(BF16))
            o = jnp.dot(jnp.concatenate(probs, axis=0), v, preferred_element_type=F32)
            for e, h in enumerate(heads):
                o_ref[pl.ds(r, A_BLOCK), h * HEAD_DIM:(h + 1) * HEAD_DIM] = (
                    o[e * A_BLOCK:(e + 1) * A_BLOCK, :] / dens[e]).astype(BF16)
        return carry

    lax.fori_loop(0, A_QB, body, 0)


def _attn_win(qkv, sink, col_q, col_k, col_v):
    b = qkv.shape[0]
    n_steps = SEQ // A_STEP
    n_blocks = SEQ // A_BLOCK
    kcol, vcol = col_k // A_KV_W, col_v // A_KV_W

    def edge(col, shift):
        def index_map(bi, m):
            blk = jnp.clip(m * A_QB + shift, 0, n_blocks - 1)
            return (bi, blk, col)
        return pl.BlockSpec((None, A_BLOCK, A_KV_W), index_map)

    def centre(col):
        return pl.BlockSpec((None, A_STEP, A_KV_W), lambda bi, m: (bi, m, col))

    return pl.pallas_call(
        _attn_win_kernel,
        grid=(b, n_steps),
        in_specs=[
            pl.BlockSpec(memory_space=pltpu.SMEM),
            pl.BlockSpec((None, A_STEP, A_Q_W), lambda bi, m: (bi, m, col_q // A_Q_W)),
            edge(kcol, -1), centre(kcol), edge(kcol, A_QB),
            edge(vcol, -1), centre(vcol), edge(vcol, A_QB),
        ],
        out_specs=pl.BlockSpec((None, A_STEP, A_Q_W), lambda bi, m: (bi, m, 0)),
        out_shape=jax.ShapeDtypeStruct((b, SEQ, A_Q_W), BF16),
        scratch_shapes=[pltpu.VMEM((A_STEP + 2 * A_BLOCK, A_KV_W), BF16),
                        pltpu.VMEM((A_STEP + 2 * A_BLOCK, A_KV_W), BF16)],
        compiler_params=_compiler_params(2),
        name="attn_win",
    )(sink, qkv, qkv, qkv, qkv, qkv, qkv, qkv)


def _nbr_row_start(row):
    return min(max(row - B_WIN_R // 2, 0), GRID_ROWS - B_WIN_R)


def _nbr_key_row0(pair):
    return min(_nbr_row_start(pair * B_PAIR_ROWS), GRID_ROWS - B_KEY_ROWS)


def _build_nbr_bias(rpb_ref, tbl_ref, toep_l, toep_r):
    lanes = 2 * GRID_W
    qc = lax.broadcasted_iota(jnp.int32, (GRID_W, lanes), 0)
    lane = lax.broadcasted_iota(jnp.int32, (GRID_W, lanes), 1)
    col_start = jnp.clip(qc - B_WIN_C // 2, 0, GRID_W - B_WIN_C)
    left = lane < GRID_W
    neg = jnp.full((GRID_W, lanes), NEG_INF, F32)
    n_rel_r, n_rel_c = 2 * B_WIN_R - 1, 2 * B_WIN_C - 1

    def per_head(h, carry):
        for half, dst in ((0, toep_l), (1, toep_r)):
            kc = lane - half * GRID_W
            rel = kc - qc + (B_WIN_C - 1)
            col_ok = (kc >= col_start) & (kc < col_start + B_WIN_C) & (left if half == 0 else ~left)
            for a in range(n_rel_r):
                t = jnp.zeros((GRID_W, lanes), F32)
                for c in range(n_rel_c):
                    t = jnp.where(rel == c, rpb_ref[h, a * n_rel_c + c], t)
                dst[a] = jnp.where(col_ok, t, neg)
        for vi, pair in enumerate(B_VARIANT_PAIRS):
            row0 = _nbr_key_row0(pair)
            for ql in range(B_PAIR_ROWS):
                q_row = pair * B_PAIR_ROWS + ql
                start = _nbr_row_start(q_row)
                for mt in range(B_KEY_ROWS // 2):
                    halves = []
                    for half, src in ((0, toep_l), (1, toep_r)):
                        k_row = row0 + 2 * mt + half
                        if start <= k_row < start + B_WIN_R:
                            halves.append(src[k_row - q_row + B_WIN_R - 1])
                        else:
                            halves.append(neg)
                    tbl_ref[vi, h, ql * GRID_W:(ql + 1) * GRID_W, mt * lanes:(mt + 1) * lanes] = (
                        jnp.where(left, halves[0], halves[1]))
        return carry

    lax.fori_loop(0, B_HEADS, per_head, 0)


def _attn_nbr_kernel(rpb_ref, q_ref, k_ref, v_ref, o_ref, tbl_ref, toep_l, toep_r):
    step = pl.program_id(1)

    @pl.when((pl.program_id(0) == 0) & (step == 0))
    def _():
        _build_nbr_bias(rpb_ref, tbl_ref, toep_l, toep_r)

    def body(pp, carry):
        pair = step * B_PAIRS_PER_STEP + pp
        row0 = jnp.clip(pair * B_PAIR_ROWS - B_WIN_R // 2, 0, GRID_ROWS - B_KEY_ROWS)
        variant = jnp.where(pair < 2, pair, jnp.where(pair >= B_N_PAIRS - 2, pair - (B_N_PAIRS - 5), 2))
        k0 = pl.multiple_of(row0 * GRID_W, GRID_W)
        q0 = pl.multiple_of(pp * B_PAIR_TOK, B_PAIR_TOK)
        for h in range(B_HEADS):
            cols = slice(h * HEAD_DIM, (h + 1) * HEAD_DIM)
            q = q_ref[pl.ds(q0, B_PAIR_TOK), cols]
            k = k_ref[pl.ds(k0, B_KEY_TOK), cols]
            v = v_ref[pl.ds(k0, B_KEY_TOK), cols]
            s = lax.dot_general(q, k, (((1,), (1,)), ((), ())), preferred_element_type=F32)
            logits = s * SCALE + tbl_ref[variant, h]
            mx = jnp.max(logits, axis=-1, keepdims=True)
            p = jnp.exp(logits - mx)
            den = jnp.sum(p, axis=-1, keepdims=True)
            o = jnp.dot(p.astype(BF16), v, preferred_element_type=F32)
            o_ref[pl.ds(q0, B_PAIR_TOK), cols] = (o / den).astype(BF16)
        return carry

    lax.fori_loop(0, B_PAIRS_PER_STEP, body, 0)


def _attn_nbr(qkv, rpb, col_q, col_k, col_v):
    b = qkv.shape[0]
    return pl.pallas_call(
        _attn_nbr_kernel,
        grid=(b, SEQ // B_STEP),
        in_specs=[
            pl.BlockSpec(memory_space=pltpu.SMEM),
            pl.BlockSpec((None, B_STEP, B_W), lambda bi, s: (bi, s, col_q // B_W)),
            pl.BlockSpec((None, SEQ, B_W), lambda bi, s: (bi, 0, col_k // B_W)),
            pl.BlockSpec((None, SEQ, B_W), lambda bi, s: (bi, 0, col_v // B_W)),
        ],
        out_specs=pl.BlockSpec((None, B_STEP, B_W), lambda bi, s: (bi, s, 0)),
        out_shape=jax.ShapeDtypeStruct((b, SEQ, B_W), BF16),
        scratch_shapes=[
            pltpu.VMEM((len(B_VARIANT_PAIRS), B_HEADS, B_PAIR_TOK, B_KEY_TOK), F32),
            pltpu.VMEM((2 * B_WIN_R - 1, GRID_W, 2 * GRID_W), F32),
            pltpu.VMEM((2 * B_WIN_R - 1, GRID_W, 2 * GRID_W), F32),
        ],
        compiler_params=_compiler_params(2),
        name="attn_nbr",
    )(rpb, qkv, qkv, qkv)


def _mix_out_kernel(u_ref, a_ref, b_ref, h_ref, wga_ref, wgb_ref, wua_ref, wub_ref, bga_ref, bgb_ref,
                    wo_ref, post_g_ref, o_ref):
    c = pl.program_id(1)

    @pl.when(c == 0)
    def _():
        o_ref[...] = jnp.zeros_like(o_ref)

    u = u_ref[...]
    ga = jnp.dot(u, wga_ref[...], preferred_element_type=F32) + bga_ref[...]
    gb = jnp.dot(u, wgb_ref[...], preferred_element_type=F32) + bgb_ref[...]
    ya = jnp.dot(a_ref[...], wua_ref[...], preferred_element_type=F32)
    yb = jnp.dot(b_ref[...], wub_ref[...], preferred_element_type=F32)
    mixed = (jax.nn.sigmoid(ga) * ya + jax.nn.sigmoid(gb) * yb).astype(BF16)
    o_ref[...] += jnp.dot(mixed, wo_ref[...], preferred_element_type=F32)

    @pl.when(c == pl.num_programs(1) - 1)
    def _():
        _residual_norm_rows(h_ref, post_g_ref, o_ref, MIX_TM, 1.0)


def _mix_out(u, ya, yb, h, w_gate, w_up_a, w_up_b, b_gate, w_out, post_g):
    t = u.shape[0]
    nc = D_MODEL // MIX_TC
    return pl.pallas_call(
        _mix_out_kernel,
        grid=(t // MIX_TM, nc),
        in_specs=[
            pl.BlockSpec((MIX_TM, D_MODEL), lambda i, c: (i, 0)),
            pl.BlockSpec((MIX_TM, A_Q_W), lambda i, c: (i, 0)),
            pl.BlockSpec((MIX_TM, B_W), lambda i, c: (i, 0)),
            pl.BlockSpec((MIX_TM, D_MODEL), lambda i, c: (i, 0)),
            pl.BlockSpec((D_MODEL, MIX_TC), lambda i, c: (0, c)),
            pl.BlockSpec((D_MODEL, MIX_TC), lambda i, c: (0, c + nc)),
            pl.BlockSpec((A_Q_W, MIX_TC), lambda i, c: (0, c)),
            pl.BlockSpec((B_W, MIX_TC), lambda i, c: (0, c)),
            pl.BlockSpec((1, MIX_TC), lambda i, c: (0, c)),
            pl.BlockSpec((1, MIX_TC), lambda i, c: (0, c + nc)),
            pl.BlockSpec((MIX_TC, D_MODEL), lambda i, c: (c, 0)),
            pl.BlockSpec((1, D_MODEL), lambda i, c: (0, 0)),
        ],
        out_specs=pl.BlockSpec((MIX_TM, D_MODEL), lambda i, c: (i, 0)),
        out_shape=jax.ShapeDtypeStruct((t, D_MODEL), F32),
        compiler_params=_compiler_params(2),
        name="mix_out",
    )(u, ya, yb, h, w_gate, w_gate, w_up_a, w_up_b, b_gate, b_gate, w_out, post_g)


def kernel(x, ffn1_pre_g, ffn1_w_in, ffn1_w_out, ffn1_post_g, mix_pre_g, w_in, b_gate, sink_a, rpb_b,
           w_up_a, w_up_b, w_out, mix_post_g, ffn2_pre_g, ffn2_w_in, ffn2_w_out, ffn2_post_g):
    batch, seq, d = x.shape
    assert (seq, d) == (SEQ, D_MODEL)
    depth = ffn1_w_in.shape[0]
    tokens = batch * seq
    h = x.reshape(tokens, d)

    cuts = np.cumsum([0, A_Q_W, A_KV_W, A_KV_W, B_W, B_W, B_W]).tolist()
    seg = {name: (cuts[i], cuts[i + 1]) for i, name in enumerate(("qa", "ka", "va", "qb", "kb", "vb"))}
    order = ("qa", "qb", "kb", "vb", "ka", "va")
    col, offset = {}, 0
    for name in order:
        col[name] = offset
        offset += seg[name][1] - seg[name][0]

    def row(v):
        return v.reshape(1, -1).astype(F32)

    for l in range(depth):
        w_qkv = jnp.concatenate([w_in[l][:, seg[n][0]:seg[n][1]] for n in order], axis=1).astype(BF16)
        w_gate = w_in[l][:, QKV_W:].astype(BF16)

        h = _ffn(h, row(ffn1_pre_g[l]), ffn1_w_in[l].astype(BF16), ffn1_w_out[l].astype(BF16),
                 row(ffn1_post_g[l]))
        u, qkv = _qkv_proj(h, row(mix_pre_g[l]), w_qkv)
        qkv = qkv.reshape(batch, seq, QKV_W)
        ya = _attn_win(qkv, sink_a[l].astype(F32), col["qa"], col["ka"], col["va"])
        yb = _attn_nbr(qkv, rpb_b[l].astype(F32).reshape(B_HEADS, -1), col["qb"], col["kb"], col["vb"])
        h = _mix_out(u, ya.reshape(tokens, A_Q_W), yb.reshape(tokens, B_W), h, w_gate,
                     w_up_a[l].astype(BF16), w_up_b[l].astype(BF16), row(b_gate[l]),
                     w_out[l].astype(BF16), row(mix_post_g[l]))
        h = _ffn(h, row(ffn2_pre_g[l]), ffn2_w_in[l].astype(BF16), ffn2_w_out[l].astype(BF16),
                 row(ffn2_post_g[l]))
    return h.reshape(batch, seq, d)
```

```python
import functools
import math

import jax
import jax.numpy as jnp
import numpy as np
from jax import lax
from jax.experimental import pallas as pl
from jax.experimental.pallas import tpu as pltpu

D_MODEL = 2048
SEQ = 4096
HEAD_DIM = 128
A_Q_HEADS = 8
A_KV_HEADS = 2
A_GROUP = A_Q_HEADS // A_KV_HEADS
A_WINDOW = 128
A_BLOCK = 128
B_HEADS = 8
GRID_W = 64
GRID_ROWS = SEQ // GRID_W
B_WIN_R = 8
B_WIN_C = 16
D_FF = 5632
MACARON_W = 0.5
RMS_EPS = 1e-6
NEG_INF = -1e30
SCALE = HEAD_DIM ** -0.5
LOG2E = math.log2(math.e)

A_Q_W = A_Q_HEADS * HEAD_DIM
A_KV_W = A_KV_HEADS * HEAD_DIM
B_W = B_HEADS * HEAD_DIM
QKV_W = A_Q_W + 2 * A_KV_W + 3 * B_W

F32 = jnp.float32
BF16 = jnp.bfloat16

VMEM_LIMIT_BYTES = 56 * 1024 * 1024

FFN_TM = 1024
FFN_FC = 512
QKV_TM = 1024
QKV_TN = 1536
MIX_TM = 512
MIX_TC = 512
NORM_ROWS = 32

A_QB = 4
A_STEP = A_QB * A_BLOCK
B_PAIR_ROWS = 2
B_PAIR_TOK = B_PAIR_ROWS * GRID_W
B_KEY_ROWS = 10
B_KEY_TOK = B_KEY_ROWS * GRID_W
B_PAIRS_PER_STEP = 4
B_STEP = B_PAIRS_PER_STEP * B_PAIR_TOK
B_N_PAIRS = GRID_ROWS // B_PAIR_ROWS
B_VARIANT_PAIRS = (0, 1, 2, B_N_PAIRS - 2, B_N_PAIRS - 1)


def _alibi_slopes(n_heads):
    return [2.0 ** (-8.0 * (i + 1) / n_heads) for i in range(n_heads)]


def _compiler_params(n_axes):
    return pltpu.CompilerParams(dimension_semantics=("arbitrary",) * n_axes,
                                vmem_limit_bytes=VMEM_LIMIT_BYTES)


def _rms_scale(x):
    return x * lax.rsqrt(jnp.mean(x * x, axis=-1, keepdims=True) + RMS_EPS)


def _norm_rows_to(src_ref, gain_ref, dst_ref, rows):
    gain = gain_ref[...]
    for c in range(rows // NORM_ROWS):
        rs = slice(c * NORM_ROWS, (c + 1) * NORM_ROWS)
        dst_ref[rs, :] = (_rms_scale(src_ref[rs, :]) * gain).astype(dst_ref.dtype)


def _residual_norm_rows(res_ref, gain_ref, acc_ref, dst_ref, rows, weight):
    gain = gain_ref[...]
    for c in range(rows // NORM_ROWS):
        rs = slice(c * NORM_ROWS, (c + 1) * NORM_ROWS)
        dst_ref[rs, :] = res_ref[rs, :] + weight * (_rms_scale(acc_ref[rs, :]) * gain)


def _ffn_kernel(x_ref, pre_g_ref, wg_ref, wu_ref, wo_ref, post_g_ref, o_ref, n_ref):
    j = pl.program_id(1)

    @pl.when(j == 0)
    def _():
        _norm_rows_to(x_ref, pre_g_ref, n_ref, FFN_TM)
        o_ref[...] = jnp.zeros_like(o_ref)

    n = n_ref[...]
    g = jnp.dot(n, wg_ref[...], preferred_element_type=F32)
    u = jnp.dot(n, wu_ref[...], preferred_element_type=F32)
    a = (g * jax.nn.sigmoid(g) * u).astype(BF16)
    o_ref[...] += jnp.dot(a, wo_ref[...], preferred_element_type=F32)

    @pl.when(j == pl.num_programs(1) - 1)
    def _():
        _residual_norm_rows(x_ref, post_g_ref, o_ref, o_ref, FFN_TM, MACARON_W)


def _ffn(x, pre_g, w_in, w_out, post_g):
    t = x.shape[0]
    nf = D_FF // FFN_FC
    return pl.pallas_call(
        _ffn_kernel,
        grid=(t // FFN_TM, nf),
        in_specs=[
            pl.BlockSpec((FFN_TM, D_MODEL), lambda i, j: (i, 0)),
            pl.BlockSpec((1, D_MODEL), lambda i, j: (0, 0)),
            pl.BlockSpec((D_MODEL, FFN_FC), lambda i, j: (0, j)),
            pl.BlockSpec((D_MODEL, FFN_FC), lambda i, j: (0, j + nf)),
            pl.BlockSpec((FFN_FC, D_MODEL), lambda i, j: (j, 0)),
            pl.BlockSpec((1, D_MODEL), lambda i, j: (0, 0)),
        ],
        out_specs=pl.BlockSpec((FFN_TM, D_MODEL), lambda i, j: (i, 0)),
        out_shape=jax.ShapeDtypeStruct((t, D_MODEL), F32),
        scratch_shapes=[pltpu.VMEM((FFN_TM, D_MODEL), BF16)],
        compiler_params=_compiler_params(2),
        name="ffn",
    )(x, pre_g, w_in, w_in, w_out, post_g)


def _qkv_kernel(h_ref, g_ref, w_ref, u_ref, o_ref):
    @pl.when(pl.program_id(1) == 0)
    def _():
        _norm_rows_to(h_ref, g_ref, u_ref, QKV_TM)

    o_ref[...] = jnp.dot(u_ref[...], w_ref[...], preferred_element_type=F32).astype(BF16)


def _qkv_proj(h, gain, w_qkv):
    t = h.shape[0]
    return pl.pallas_call(
        _qkv_kernel,
        grid=(t // QKV_TM, QKV_W // QKV_TN),
        in_specs=[
            pl.BlockSpec((QKV_TM, D_MODEL), lambda i, j: (i, 0)),
            pl.BlockSpec((1, D_MODEL), lambda i, j: (0, 0)),
            pl.BlockSpec((D_MODEL, QKV_TN), lambda i, j: (0, j)),
        ],
        out_specs=[
            pl.BlockSpec((QKV_TM, D_MODEL), lambda i, j: (i, 0)),
            pl.BlockSpec((QKV_TM, QKV_TN), lambda i, j: (i, j)),
        ],
        out_shape=[jax.ShapeDtypeStruct((t, D_MODEL), BF16),
                   jax.ShapeDtypeStruct((t, QKV_W), BF16)],
        compiler_params=_compiler_params(2),
        name="qkv_proj",
    )(h, gain, w_qkv)


def _attn_win_kernel(sink_ref, q_ref, kp_ref, kc_ref, kn_ref, vp_ref, vc_ref, vn_ref, o_ref,
                     k_scr, v_scr):
    m = pl.program_id(1)
    k_scr[0:A_BLOCK, :] = kp_ref[...]
    k_scr[A_BLOCK:A_BLOCK + A_STEP, :] = kc_ref[...]
    k_scr[A_BLOCK + A_STEP:, :] = kn_ref[...]
    v_scr[0:A_BLOCK, :] = vp_ref[...]
    v_scr[A_BLOCK:A_BLOCK + A_STEP, :] = vc_ref[...]
    v_scr[A_BLOCK + A_STEP:, :] = vn_ref[...]

    span = 3 * A_BLOCK
    qi = lax.broadcasted_iota(jnp.int32, (A_BLOCK, span), 0)
    kj = lax.broadcasted_iota(jnp.int32, (A_BLOCK, span), 1)
    absd_i = jnp.abs(qi + A_WINDOW - kj)
    absd = absd_i.astype(F32)
    in_window = absd_i <= A_WINDOW
    slopes = _alibi_slopes(A_Q_HEADS)

    def body(t, carry):
        r = pl.multiple_of(t * A_BLOCK, A_BLOCK)
        key_pos = (m * A_QB + t) * A_BLOCK - A_WINDOW + kj
        valid = in_window & (key_pos >= 0) & (key_pos < SEQ)
        for g in range(A_KV_HEADS):
            k = k_scr[pl.ds(r, span), g * HEAD_DIM:(g + 1) * HEAD_DIM]
            v = v_scr[pl.ds(r, span), g * HEAD_DIM:(g + 1) * HEAD_DIM]
            heads = [g * A_GROUP + e for e in range(A_GROUP)]
            qs = jnp.concatenate(
                [q_ref[pl.ds(r, A_BLOCK), h * HEAD_DIM:(h + 1) * HEAD_DIM] for h in heads], axis=0)
            s = lax.dot_general(qs, k, (((1,), (1,)), ((), ())), preferred_element_type=F32)
            probs, dens = [], []
            for e, h in enumerate(heads):
                logits = (s[e * A_BLOCK:(e + 1) * A_BLOCK, :] * (SCALE * LOG2E)
                          + (-slopes[h] * LOG2E) * absd)
                logits = jnp.where(valid, logits, NEG_INF)
                sink = sink_ref[h] * LOG2E
                mx = jnp.maximum(jnp.max(logits, axis=-1, keepdims=True), sink)
                p = jnp.exp2(logits - mx)
                dens.append(jnp.sum(p, axis=-1, keepdims=True) + jnp.exp2(sink - mx))
                probs.append(p.astype(BF16))
            o = jnp.dot(jnp.concatenate(probs, axis=0), v, preferred_element_type=F32)
            for e, h in enumerate(heads):
                o_ref[pl.ds(r, A_BLOCK), h * HEAD_DIM:(h + 1) * HEAD_DIM] = (
                    o[e * A_BLOCK:(e + 1) * A_BLOCK, :] / dens[e]).astype(BF16)
        return carry

    lax.fori_loop(0, A_QB, body, 0)


def _attn_win(qkv, sink, col_q, col_k, col_v):
    b = qkv.shape[0]
    n_steps = SEQ // A_STEP
    n_blocks = SEQ // A_BLOCK
    kcol, vcol = col_k // A_KV_W, col_v // A_KV_W

    def edge(col, shift):
        def index_map(bi, m):
            blk = jnp.clip(m * A_QB + shift, 0, n_blocks - 1)
            return (bi, blk, col)
        return pl.BlockSpec((None, A_BLOCK, A_KV_W), index_map)

    def centre(col):
        return pl.BlockSpec((None, A_STEP, A_KV_W), lambda bi, m: (bi, m, col))

    return pl.pallas_call(
        _attn_win_kernel,
        grid=(b, n_steps),
        in_specs=[
            pl.BlockSpec(memory_space=pltpu.SMEM),
            pl.BlockSpec((None, A_STEP, A_Q_W), lambda bi, m: (bi, m, col_q // A_Q_W)),
            edge(kcol, -1), centre(kcol), edge(kcol, A_QB),
            edge(vcol, -1), centre(vcol), edge(vcol, A_QB),
        ],
        out_specs=pl.BlockSpec((None, A_STEP, A_Q_W), lambda bi, m: (bi, m, 0)),
        out_shape=jax.ShapeDtypeStruct((b, SEQ, A_Q_W), BF16),
        scratch_shapes=[pltpu.VMEM((A_STEP + 2 * A_BLOCK, A_KV_W), BF16),
                        pltpu.VMEM((A_STEP + 2 * A_BLOCK, A_KV_W), BF16)],
        compiler_params=_compiler_params(2),
        name="attn_win",
    )(sink, qkv, qkv, qkv, qkv, qkv, qkv, qkv)


def _nbr_row_start(row):
    return min(max(row - B_WIN_R // 2, 0), GRID_ROWS - B_WIN_R)


def _nbr_key_row0(pair):
    return min(_nbr_row_start(pair * B_PAIR_ROWS), GRID_ROWS - B_KEY_ROWS)


def _build_nbr_bias(rpb_ref, tbl_ref, toep_l, toep_r):
    lanes = 2 * GRID_W
    qc = lax.broadcasted_iota(jnp.int32, (GRID_W, lanes), 0)
    lane = lax.broadcasted_iota(jnp.int32, (GRID_W, lanes), 1)
    col_start = jnp.clip(qc - B_WIN_C // 2, 0, GRID_W - B_WIN_C)
    left = lane < GRID_W
    neg = jnp.full((GRID_W, lanes), NEG_INF, F32)
    n_rel_r, n_rel_c = 2 * B_WIN_R - 1, 2 * B_WIN_C - 1

    def per_head(h, carry):
        for half, dst in ((0, toep_l), (1, toep_r)):
            kc = lane - half * GRID_W
            rel = kc - qc + (B_WIN_C - 1)
            col_ok = (kc >= col_start) & (kc < col_start + B_WIN_C) & (left if half == 0 else ~left)
            for a in range(n_rel_r):
                t = jnp.zeros((GRID_W, lanes), F32)
                for c in range(n_rel_c):
                    t = jnp.where(rel == c, rpb_ref[h, a * n_rel_c + c] * LOG2E, t)
                dst[a] = jnp.where(col_ok, t, neg)
        for vi, pair in enumerate(B_VARIANT_PAIRS):
            row0 = _nbr_key_row0(pair)
            for ql in range(B_PAIR_ROWS):
                q_row = pair * B_PAIR_ROWS + ql
                start = _nbr_row_start(q_row)
                for mt in range(B_KEY_ROWS // 2):
                    halves = []
                    for half, src in ((0, toep_l), (1, toep_r)):
                        k_row = row0 + 2 * mt + half
                        if start <= k_row < start + B_WIN_R:
                            halves.append(src[k_row - q_row + B_WIN_R - 1])
                        else:
                            halves.append(neg)
                    tbl_ref[vi, h, ql * GRID_W:(ql + 1) * GRID_W, mt * lanes:(mt + 1) * lanes] = (
                        jnp.where(left, halves[0], halves[1]))
        return carry

    lax.fori_loop(0, B_HEADS, per_head, 0)


def _attn_nbr_kernel(rpb_ref, q_ref, k_ref, v_ref, o_ref, tbl_ref, toep_l, toep_r):
    step = pl.program_id(1)

    @pl.when((pl.program_id(0) == 0) & (step == 0))
    def _():
        _build_nbr_bias(rpb_ref, tbl_ref, toep_l, toep_r)

    def body(pp, carry):
        pair = step * B_PAIRS_PER_STEP + pp
        row0 = jnp.clip(pair * B_PAIR_ROWS - B_WIN_R // 2, 0, GRID_ROWS - B_KEY_ROWS)
        variant = jnp.where(pair < 2, pair, jnp.where(pair >= B_N_PAIRS - 2, pair - (B_N_PAIRS - 5), 2))
        k0 = pl.multiple_of(row0 * GRID_W, GRID_W)
        q0 = pl.multiple_of(pp * B_PAIR_TOK, B_PAIR_TOK)
        head_cols = [slice(h * HEAD_DIM, (h + 1) * HEAD_DIM) for h in range(B_HEADS)]
        scores = [
            lax.dot_general(q_ref[pl.ds(q0, B_PAIR_TOK), cols], k_ref[pl.ds(k0, B_KEY_TOK), cols],
                            (((1,), (1,)), ((), ())), preferred_element_type=F32)
            for cols in head_cols]
        s = jnp.concatenate(scores, axis=0)
        logits = s * (SCALE * LOG2E) + tbl_ref[variant].reshape(B_HEADS * B_PAIR_TOK, B_KEY_TOK)
        mx = jnp.max(logits, axis=-1, keepdims=True)
        p = jnp.exp2(logits - mx)
        inv_den = 1.0 / jnp.sum(p, axis=-1, keepdims=True)
        p = p.astype(BF16)
        for h, cols in enumerate(head_cols):
            rows = slice(h * B_PAIR_TOK, (h + 1) * B_PAIR_TOK)
            o = jnp.dot(p[rows, :], v_ref[pl.ds(k0, B_KEY_TOK), cols], preferred_element_type=F32)
            o_ref[pl.ds(q0, B_PAIR_TOK), cols] = (o * inv_den[rows, :]).astype(BF16)
        return carry

    lax.fori_loop(0, B_PAIRS_PER_STEP, body, 0)


def _attn_nbr(qkv, rpb, col_q, col_k, col_v):
    b = qkv.shape[0]
    return pl.pallas_call(
        _attn_nbr_kernel,
        grid=(b, SEQ // B_STEP),
        in_specs=[
            pl.BlockSpec(memory_space=pltpu.SMEM),
            pl.BlockSpec((None, B_STEP, B_W), lambda bi, s: (bi, s, col_q // B_W)),
            pl.BlockSpec((None, SEQ, B_W), lambda bi, s: (bi, 0, col_k // B_W)),
            pl.BlockSpec((None, SEQ, B_W), lambda bi, s: (bi, 0, col_v // B_W)),
        ],
        out_specs=pl.BlockSpec((None, B_STEP, B_W), lambda bi, s: (bi, s, 0)),
        out_shape=jax.ShapeDtypeStruct((b, SEQ, B_W), BF16),
        scratch_shapes=[
            pltpu.VMEM((len(B_VARIANT_PAIRS), B_HEADS, B_PAIR_TOK, B_KEY_TOK), F32),
            pltpu.VMEM((2 * B_WIN_R - 1, GRID_W, 2 * GRID_W), F32),
            pltpu.VMEM((2 * B_WIN_R - 1, GRID_W, 2 * GRID_W), F32),
        ],
        compiler_params=_compiler_params(2),
        name="attn_nbr",
    )(rpb, qkv, qkv, qkv)


def _mix_out_kernel(u_ref, a_ref, b_ref, h_ref, wga_ref, wgb_ref, wua_ref, wub_ref, bga_ref, bgb_ref,
                    wo_ref, post_g_ref, o_ref, acc_ref):
    c = pl.program_id(1)

    @pl.when(c == 0)
    def _():
        acc_ref[...] = jnp.zeros_like(acc_ref)

    u = u_ref[...]
    ga = jnp.dot(u, wga_ref[...], preferred_element_type=F32) + bga_ref[...]
    gb = jnp.dot(u, wgb_ref[...], preferred_element_type=F32) + bgb_ref[...]
    ya = jnp.dot(a_ref[...], wua_ref[...], preferred_element_type=F32)
    yb = jnp.dot(b_ref[...], wub_ref[...], preferred_element_type=F32)
    mixed = (jax.nn.sigmoid(ga) * ya + jax.nn.sigmoid(gb) * yb).astype(BF16)
    acc_ref[...] += jnp.dot(mixed, wo_ref[...], preferred_element_type=F32)

    @pl.when(c == pl.num_programs(1) - 1)
    def _():
        _residual_norm_rows(h_ref, post_g_ref, acc_ref, o_ref, MIX_TM, 1.0)


def _mix_out(u, ya, yb, h, w_gate, w_up_a, w_up_b, b_gate, w_out, post_g):
    t = u.shape[0]
    nc = D_MODEL // MIX_TC
    return pl.pallas_call(
        _mix_out_kernel,
        grid=(t // MIX_TM, nc),
        in_specs=[
            pl.BlockSpec((MIX_TM, D_MODEL), lambda i, c: (i, 0)),
            pl.BlockSpec((MIX_TM, A_Q_W), lambda i, c: (i, 0)),
            pl.BlockSpec((MIX_TM, B_W), lambda i, c: (i, 0)),
            pl.BlockSpec((MIX_TM, D_MODEL), lambda i, c: (i, 0)),
            pl.BlockSpec((D_MODEL, MIX_TC), lambda i, c: (0, c)),
            pl.BlockSpec((D_MODEL, MIX_TC), lambda i, c: (0, c + nc)),
            pl.BlockSpec((A_Q_W, MIX_TC), lambda i, c: (0, c)),
            pl.BlockSpec((B_W, MIX_TC), lambda i, c: (0, c)),
            pl.BlockSpec((1, MIX_TC), lambda i, c: (0, c)),
            pl.BlockSpec((1, MIX_TC), lambda i, c: (0, c + nc)),
            pl.BlockSpec((MIX_TC, D_MODEL), lambda i, c: (c, 0)),
            pl.BlockSpec((1, D_MODEL), lambda i, c: (0, 0)),
        ],
        out_specs=pl.BlockSpec((MIX_TM, D_MODEL), lambda i, c: (i, 0)),
        out_shape=jax.ShapeDtypeStruct((t, D_MODEL), F32),
        scratch_shapes=[pltpu.VMEM((MIX_TM, D_MODEL), F32)],
        compiler_params=_compiler_params(2),
        name="mix_out",
    )(u, ya, yb, h, w_gate, w_gate, w_up_a, w_up_b, b_gate, b_gate, w_out, post_g)


def kernel(x, ffn1_pre_g, ffn1_w_in, ffn1_w_out, ffn1_post_g, mix_pre_g, w_in, b_gate, sink_a, rpb_b,
           w_up_a, w_up_b, w_out, mix_post_g, ffn2_pre_g, ffn2_w_in, ffn2_w_out, ffn2_post_g):
    batch, seq, d = x.shape
    assert (seq, d) == (SEQ, D_MODEL)
    depth = ffn1_w_in.shape[0]
    tokens = batch * seq
    h = x.reshape(tokens, d)

    cuts = np.cumsum([0, A_Q_W, A_KV_W, A_KV_W, B_W, B_W, B_W]).tolist()
    seg = {name: (cuts[i], cuts[i + 1]) for i, name in enumerate(("qa", "ka", "va", "qb", "kb", "vb"))}
    order = ("qa", "qb", "kb", "vb", "ka", "va")
    col, offset = {}, 0
    for name in order:
        col[name] = offset
        offset += seg[name][1] - seg[name][0]

    def row(v):
        return v.reshape(1, -1).astype(F32)

    for l in range(depth):
        w_qkv = jnp.concatenate([w_in[l][:, seg[n][0]:seg[n][1]] for n in order], axis=1).astype(BF16)
        w_gate = w_in[l][:, QKV_W:].astype(BF16)

        h = _ffn(h, row(ffn1_pre_g[l]), ffn1_w_in[l].astype(BF16), ffn1_w_out[l].astype(BF16),
                 row(ffn1_post_g[l]))
        u, qkv = _qkv_proj(h, row(mix_pre_g[l]), w_qkv)
        qkv = qkv.reshape(batch, seq, QKV_W)
        ya = _attn_win(qkv, sink_a[l].astype(F32), col["qa"], col["ka"], col["va"])
        yb = _attn_nbr(qkv, rpb_b[l].astype(F32).reshape(B_HEADS, -1), col["qb"], col["kb"], col["vb"])
        h = _mix_out(u, ya.reshape(tokens, A_Q_W), yb.reshape(tokens, B_W), h, w_gate,
                     w_up_a[l].astype(BF16), w_up_b[l].astype(BF16), row(b_gate[l]),
                     w_out[l].astype(BF16), row(mix_post_g[l]))
        h = _ffn(h, row(ffn2_pre_g[l]), ffn2_w_in[l].astype(BF16), ffn2_w_out[l].astype(BF16),
                 row(ffn2_post_g[l]))
    return h.reshape(batch, seq, d)
```

```python
import functools
import math
from typing import Callable, NamedTuple

import jax
import jax.numpy as jnp
import numpy as np
from jax import lax
from jax.experimental import pallas as pl
from jax.experimental.pallas import tpu as pltpu

D_MODEL = 2048
SEQ = 4096
HEAD_DIM = 128
A_Q_HEADS = 8
A_KV_HEADS = 2
A_GROUP = A_Q_HEADS // A_KV_HEADS
A_WINDOW = 128
A_BLOCK = 128
B_HEADS = 8
GRID_W = 64
GRID_ROWS = SEQ // GRID_W
B_WIN_R = 8
B_WIN_C = 16
D_FF = 5632
MACARON_W = 0.5
RMS_EPS = 1e-6
NEG_INF = -1e30
SCALE = HEAD_DIM ** -0.5
LOG2E = math.log2(math.e)
Q_PRESCALE = SCALE * LOG2E

A_Q_W = A_Q_HEADS * HEAD_DIM
A_KV_W = A_KV_HEADS * HEAD_DIM
B_W = B_HEADS * HEAD_DIM
QKV_W = A_Q_W + 2 * A_KV_W + 3 * B_W

_cuts = np.cumsum([0, A_Q_W, A_KV_W, A_KV_W, B_W, B_W, B_W]).tolist()
QKV_SRC = {name: (_cuts[i], _cuts[i + 1]) for i, name in enumerate(("qa", "ka", "va", "qb", "kb", "vb"))}
QKV_ORDER = ("qa", "qb", "kb", "vb", "ka", "va")
QKV_COL = {}
for _name in QKV_ORDER:
    QKV_COL[_name] = sum(QKV_SRC[n][1] - QKV_SRC[n][0] for n in QKV_ORDER[:QKV_ORDER.index(_name)])

F32 = jnp.float32
BF16 = jnp.bfloat16

VMEM_LIMIT_BYTES = 61 * 1024 * 1024

FFN_TM = 1024
FFN_FC = 512
QKV_TM = 1024
QKV_TN = 1536
MIX_TM = 512
MIX_TC = 512
NORM_ROWS = 32
CAST_ROWS = 16

A_QB = 4
A_STEP = A_QB * A_BLOCK
B_PAIR_ROWS = 2
B_PAIR_TOK = B_PAIR_ROWS * GRID_W
B_KEY_ROWS = 10
B_KEY_TOK = B_KEY_ROWS * GRID_W
B_PAIRS_PER_STEP = 4
B_STEP = B_PAIRS_PER_STEP * B_PAIR_TOK
B_N_PAIRS = GRID_ROWS // B_PAIR_ROWS
B_VARIANT_PAIRS = (0, 1, 2, B_N_PAIRS - 2, B_N_PAIRS - 1)


def _alibi_slopes(n_heads):
    return [2.0 ** (-8.0 * (i + 1) / n_heads) for i in range(n_heads)]


def _compiler_params(n_axes):
    return pltpu.CompilerParams(dimension_semantics=("arbitrary",) * n_axes,
                                vmem_limit_bytes=VMEM_LIMIT_BYTES)


def _rms_scale(x):
    return x * lax.rsqrt(jnp.mean(x * x, axis=-1, keepdims=True) + RMS_EPS)


def _norm_rows_to(src_ref, gain_ref, dst_ref, rows):
    gain = gain_ref[...]
    for c in range(rows // NORM_ROWS):
        rs = slice(c * NORM_ROWS, (c + 1) * NORM_ROWS)
        dst_ref[rs, :] = (_rms_scale(src_ref[rs, :]) * gain).astype(dst_ref.dtype)


def _residual_norm_rows(res_ref, gain_ref, acc_ref, dst_ref, rows, weight):
    gain = gain_ref[...]
    for c in range(rows // NORM_ROWS):
        rs = slice(c * NORM_ROWS, (c + 1) * NORM_ROWS)
        dst_ref[rs, :] = res_ref[rs, :] + weight * (_rms_scale(acc_ref[rs, :]) * gain)


class CastJob(NamedTuple):
    src: jax.Array
    in_spec: pl.BlockSpec
    out_specs: tuple
    out_shapes: tuple
    body: Callable


def _cast_block(src_ref, dst_ref):
    dst_ref[...] = src_ref[...].astype(BF16)


def _tile_rows_job(src, n_tiles, n_steps, body=_cast_block, out_cols=None):
    rows, cols = src.shape
    per_tile = rows // (n_tiles * CAST_ROWS)
    assert per_tile * n_tiles * CAST_ROWS == rows and per_tile <= n_steps
    index_map = lambda i, j: (i * per_tile + jnp.minimum(j, per_tile - 1), 0)
    out_cols = (cols,) if out_cols is None else out_cols
    return CastJob(src, pl.BlockSpec((CAST_ROWS, cols), index_map),
                   tuple(pl.BlockSpec((CAST_ROWS, c), index_map) for c in out_cols),
                   tuple(jax.ShapeDtypeStruct((rows, c), BF16) for c in out_cols), body)


def _ffn_kernel(*refs, jobs):
    x_ref, pre_g_ref, wg_ref, wu_ref, wo_ref, post_g_ref = refs[:6]
    job_srcs = refs[6:6 + len(jobs)]
    o_ref = refs[6 + len(jobs)]
    job_dsts = refs[7 + len(jobs):-1]
    n_ref = refs[-1]
    j = pl.program_id(1)

    @pl.when(j == 0)
    def _():
        _norm_rows_to(x_ref, pre_g_ref, n_ref, FFN_TM)
        o_ref[...] = jnp.zeros_like(o_ref)

    n = n_ref[...]
    g = jnp.dot(n, wg_ref[...], preferred_element_type=F32)
    u = jnp.dot(n, wu_ref[...], preferred_element_type=F32)
    a = (g * jax.nn.sigmoid(g) * u).astype(BF16)
    o_ref[...] += jnp.dot(a, wo_ref[...], preferred_element_type=F32)

    first = 0
    for job, src_ref in zip(jobs, job_srcs):
        job.body(src_ref, *job_dsts[first:first + len(job.out_specs)])
        first += len(job.out_specs)

    @pl.when(j == pl.num_programs(1) - 1)
    def _():
        _residual_norm_rows(x_ref, post_g_ref, o_ref, o_ref, FFN_TM, MACARON_W)


def _ffn(x, pre_g, w_in, w_out, post_g, jobs=()):
    t = x.shape[0]
    nf = D_FF // FFN_FC
    out_specs = [pl.BlockSpec((FFN_TM, D_MODEL), lambda i, j: (i, 0))]
    out_shapes = [jax.ShapeDtypeStruct((t, D_MODEL), F32)]
    for job in jobs:
        out_specs.extend(job.out_specs)
        out_shapes.extend(job.out_shapes)
    return pl.pallas_call(
        functools.partial(_ffn_kernel, jobs=jobs),
        grid=(t // FFN_TM, nf),
        in_specs=[
            pl.BlockSpec((FFN_TM, D_MODEL), lambda i, j: (i, 0)),
            pl.BlockSpec((1, D_MODEL), lambda i, j: (0, 0)),
            pl.BlockSpec((D_MODEL, FFN_FC), lambda i, j: (0, j)),
            pl.BlockSpec((D_MODEL, FFN_FC), lambda i, j: (0, j + nf)),
            pl.BlockSpec((FFN_FC, D_MODEL), lambda i, j: (j, 0)),
            pl.BlockSpec((1, D_MODEL), lambda i, j: (0, 0)),
        ] + [job.in_spec for job in jobs],
        out_specs=out_specs,
        out_shape=out_shapes,
        scratch_shapes=[pltpu.VMEM((FFN_TM, D_MODEL), BF16)],
        compiler_params=_compiler_params(2),
        name="ffn",
    )(x, pre_g, w_in, w_in, w_out, post_g, *[job.src for job in jobs])


def _qkv_kernel(h_ref, g_ref, w_ref, u_ref, o_ref):
    @pl.when(pl.program_id(1) == 0)
    def _():
        _norm_rows_to(h_ref, g_ref, u_ref, QKV_TM)

    o_ref[...] = jnp.dot(u_ref[...], w_ref[...], preferred_element_type=F32).astype(BF16)


def _qkv_proj(h, gain, w_qkv):
    t = h.shape[0]
    return pl.pallas_call(
        _qkv_kernel,
        grid=(t // QKV_TM, QKV_W // QKV_TN),
        in_specs=[
            pl.BlockSpec((QKV_TM, D_MODEL), lambda i, j: (i, 0)),
            pl.BlockSpec((1, D_MODEL), lambda i, j: (0, 0)),
            pl.BlockSpec((D_MODEL, QKV_TN), lambda i, j: (0, j)),
        ],
        out_specs=[
            pl.BlockSpec((QKV_TM, D_MODEL), lambda i, j: (i, 0)),
            pl.BlockSpec((QKV_TM, QKV_TN), lambda i, j: (i, j)),
        ],
        out_shape=[jax.ShapeDtypeStruct((t, D_MODEL), BF16),
                   jax.ShapeDtypeStruct((t, QKV_W), BF16)],
        compiler_params=_compiler_params(2),
        name="qkv_proj",
    )(h, gain, w_qkv)


def _attn_win_kernel(sink_ref, q_ref, kp_ref, kc_ref, kn_ref, vp_ref, vc_ref, vn_ref, o_ref,
                     k_scr, v_scr):
    m = pl.program_id(1)
    k_scr[0:A_BLOCK, :] = kp_ref[...]
    k_scr[A_BLOCK:A_BLOCK + A_STEP, :] = kc_ref[...]
    k_scr[A_BLOCK + A_STEP:, :] = kn_ref[...]
    v_scr[0:A_BLOCK, :] = vp_ref[...]
    v_scr[A_BLOCK:A_BLOCK + A_STEP, :] = vc_ref[...]
    v_scr[A_BLOCK + A_STEP:, :] = vn_ref[...]

    span = 3 * A_BLOCK
    qi = lax.broadcasted_iota(jnp.int32, (A_BLOCK, span), 0)
    kj = lax.broadcasted_iota(jnp.int32, (A_BLOCK, span), 1)
    absd_i = jnp.abs(qi + A_WINDOW - kj)
    absd = absd_i.astype(F32)
    in_window = absd_i <= A_WINDOW
    slopes = _alibi_slopes(A_Q_HEADS)

    def body(t, carry):
        r = pl.multiple_of(t * A_BLOCK, A_BLOCK)
        key_pos = (m * A_QB + t) * A_BLOCK - A_WINDOW + kj
        valid = in_window & (key_pos >= 0) & (key_pos < SEQ)
        for g in range(A_KV_HEADS):
            k = k_scr[pl.ds(r, span), g * HEAD_DIM:(g + 1) * HEAD_DIM]
            v = v_scr[pl.ds(r, span), g * HEAD_DIM:(g + 1) * HEAD_DIM]
            heads = [g * A_GROUP + e for e in range(A_GROUP)]
            qs = jnp.concatenate(
                [q_ref[pl.ds(r, A_BLOCK), h * HEAD_DIM:(h + 1) * HEAD_DIM] for h in heads], axis=0)
            s = lax.dot_general(qs, k, (((1,), (1,)), ((), ())), preferred_element_type=F32)
            probs, dens = [], []
            for e, h in enumerate(heads):
                logits = s[e * A_BLOCK:(e + 1) * A_BLOCK, :] + (-slopes[h] * LOG2E) * absd
                logits = jnp.where(valid, logits, NEG_INF)
                sink = sink_ref[h] * LOG2E
                mx = jnp.maximum(jnp.max(logits, axis=-1, keepdims=True), sink)
                p = jnp.exp2(logits - mx)
                dens.append(jnp.sum(p, axis=-1, keepdims=True) + jnp.exp2(sink - mx))
                probs.append(p.astype(BF16))
            o = jnp.dot(jnp.concatenate(probs, axis=0), v, preferred_element_type=F32)
            for e, h in enumerate(heads):
                o_ref[pl.ds(r, A_BLOCK), h * HEAD_DIM:(h + 1) * HEAD_DIM] = (
                    o[e * A_BLOCK:(e + 1) * A_BLOCK, :] / dens[e]).astype(BF16)
        return carry

    lax.fori_loop(0, A_QB, body, 0)


def _attn_win(qkv, sink, col_q, col_k, col_v):
    b = qkv.shape[0]
    n_steps = SEQ // A_STEP
    n_blocks = SEQ // A_BLOCK
    kcol, vcol = col_k // A_KV_W, col_v // A_KV_W

    def edge(col, shift):
        def index_map(bi, m):
            blk = jnp.clip(m * A_QB + shift, 0, n_blocks - 1)
            return (bi, blk, col)
        return pl.BlockSpec((None, A_BLOCK, A_KV_W), index_map)

    def centre(col):
        return pl.BlockSpec((None, A_STEP, A_KV_W), lambda bi, m: (bi, m, col))

    return pl.pallas_call(
        _attn_win_kernel,
        grid=(b, n_steps),
        in_specs=[
            pl.BlockSpec(memory_space=pltpu.SMEM),
            pl.BlockSpec((None, A_STEP, A_Q_W), lambda bi, m: (bi, m, col_q // A_Q_W)),
            edge(kcol, -1), centre(kcol), edge(kcol, A_QB),
            edge(vcol, -1), centre(vcol), edge(vcol, A_QB),
        ],
        out_specs=pl.BlockSpec((None, A_STEP, A_Q_W), lambda bi, m: (bi, m, 0)),
        out_shape=jax.ShapeDtypeStruct((b, SEQ, A_Q_W), BF16),
        scratch_shapes=[pltpu.VMEM((A_STEP + 2 * A_BLOCK, A_KV_W), BF16),
                        pltpu.VMEM((A_STEP + 2 * A_BLOCK, A_KV_W), BF16)],
        compiler_params=_compiler_params(2),
        name="attn_win",
    )(sink, qkv, qkv, qkv, qkv, qkv, qkv, qkv)


def _nbr_row_start(row):
    return min(max(row - B_WIN_R // 2, 0), GRID_ROWS - B_WIN_R)


def _nbr_key_row0(pair):
    return min(_nbr_row_start(pair * B_PAIR_ROWS), GRID_ROWS - B_KEY_ROWS)


def _build_nbr_bias(rpb_ref, tbl_ref, toep_l, toep_r):
    lanes = 2 * GRID_W
    qc = lax.broadcasted_iota(jnp.int32, (GRID_W, lanes), 0)
    lane = lax.broadcasted_iota(jnp.int32, (GRID_W, lanes), 1)
    col_start = jnp.clip(qc - B_WIN_C // 2, 0, GRID_W - B_WIN_C)
    left = lane < GRID_W
    neg = jnp.full((GRID_W, lanes), NEG_INF, F32)
    n_rel_r, n_rel_c = 2 * B_WIN_R - 1, 2 * B_WIN_C - 1

    def per_head(h, carry):
        for half, dst in ((0, toep_l), (1, toep_r)):
            kc = lane - half * GRID_W
            rel = kc - qc + (B_WIN_C - 1)
            col_ok = (kc >= col_start) & (kc < col_start + B_WIN_C) & (left if half == 0 else ~left)
            for a in range(n_rel_r):
                t = jnp.zeros((GRID_W, lanes), F32)
                for c in range(n_rel_c):
                    t = jnp.where(rel == c, rpb_ref[h, a * n_rel_c + c] * LOG2E, t)
                dst[a] = jnp.where(col_ok, t, neg)
        for vi, pair in enumerate(B_VARIANT_PAIRS):
            row0 = _nbr_key_row0(pair)
            for ql in range(B_PAIR_ROWS):
                q_row = pair * B_PAIR_ROWS + ql
                start = _nbr_row_start(q_row)
                for mt in range(B_KEY_ROWS // 2):
                    halves = []
                    for half, src in ((0, toep_l), (1, toep_r)):
                        k_row = row0 + 2 * mt + half
                        if start <= k_row < start + B_WIN_R:
                            halves.append(src[k_row - q_row + B_WIN_R - 1])
                        else:
                            halves.append(neg)
                    tbl_ref[vi, h, ql * GRID_W:(ql + 1) * GRID_W, mt * lanes:(mt + 1) * lanes] = (
                        jnp.where(left, halves[0], halves[1]))
        return carry

    lax.fori_loop(0, B_HEADS, per_head, 0)


def _attn_nbr_kernel(rpb_ref, q_ref, k_ref, v_ref, o_ref, tbl_ref, toep_l, toep_r):
    step = pl.program_id(1)

    @pl.when((pl.program_id(0) == 0) & (step == 0))
    def _():
        _build_nbr_bias(rpb_ref, tbl_ref, toep_l, toep_r)

    def body(pp, carry):
        pair = step * B_PAIRS_PER_STEP + pp
        row0 = jnp.clip(pair * B_PAIR_ROWS - B_WIN_R // 2, 0, GRID_ROWS - B_KEY_ROWS)
        variant = jnp.where(pair < 2, pair, jnp.where(pair >= B_N_PAIRS - 2, pair - (B_N_PAIRS - 5), 2))
        k0 = pl.multiple_of(row0 * GRID_W, GRID_W)
        q0 = pl.multiple_of(pp * B_PAIR_TOK, B_PAIR_TOK)
        head_cols = [slice(h * HEAD_DIM, (h + 1) * HEAD_DIM) for h in range(B_HEADS)]
        scores = [
            lax.dot_general(q_ref[pl.ds(q0, B_PAIR_TOK), cols], k_ref[pl.ds(k0, B_KEY_TOK), cols],
                            (((1,), (1,)), ((), ())), preferred_element_type=F32)
            for cols in head_cols]
        s = jnp.concatenate(scores, axis=0)
        logits = s + tbl_ref[variant].reshape(B_HEADS * B_PAIR_TOK, B_KEY_TOK)
        mx = jnp.max(logits, axis=-1, keepdims=True)
        p = jnp.exp2(logits - mx)
        inv_den = 1.0 / jnp.sum(p, axis=-1, keepdims=True)
        p = p.astype(BF16)
        for h, cols in enumerate(head_cols):
            rows = slice(h * B_PAIR_TOK, (h + 1) * B_PAIR_TOK)
            o = jnp.dot(p[rows, :], v_ref[pl.ds(k0, B_KEY_TOK), cols], preferred_element_type=F32)
            o_ref[pl.ds(q0, B_PAIR_TOK), cols] = (o * inv_den[rows, :]).astype(BF16)
        return carry

    lax.fori_loop(0, B_PAIRS_PER_STEP, body, 0)


def _attn_nbr(qkv, rpb, col_q, col_k, col_v):
    b = qkv.shape[0]
    return pl.pallas_call(
        _attn_nbr_kernel,
        grid=(b, SEQ // B_STEP),
        in_specs=[
            pl.BlockSpec(memory_space=pltpu.SMEM),
            pl.BlockSpec((None, B_STEP, B_W), lambda bi, s: (bi, s, col_q // B_W)),
            pl.BlockSpec((None, SEQ, B_W), lambda bi, s: (bi, 0, col_k // B_W)),
            pl.BlockSpec((None, SEQ, B_W), lambda bi, s: (bi, 0, col_v // B_W)),
        ],
        out_specs=pl.BlockSpec((None, B_STEP, B_W), lambda bi, s: (bi, s, 0)),
        out_shape=jax.ShapeDtypeStruct((b, SEQ, B_W), BF16),
        scratch_shapes=[
            pltpu.VMEM((len(B_VARIANT_PAIRS), B_HEADS, B_PAIR_TOK, B_KEY_TOK), F32),
            pltpu.VMEM((2 * B_WIN_R - 1, GRID_W, 2 * GRID_W), F32),
            pltpu.VMEM((2 * B_WIN_R - 1, GRID_W, 2 * GRID_W), F32),
        ],
        compiler_params=_compiler_params(2),
        name="attn_nbr",
    )(rpb, qkv, qkv, qkv)


def _mix_out_kernel(u_ref, a_ref, b_ref, h_ref, wga_ref, wgb_ref, wua_ref, wub_ref, bga_ref, bgb_ref,
                    wo_ref, post_g_ref, o_ref, acc_ref):
    c = pl.program_id(1)

    @pl.when(c == 0)
    def _():
        acc_ref[...] = jnp.zeros_like(acc_ref)

    u = u_ref[...]
    ga = jnp.dot(u, wga_ref[...], preferred_element_type=F32) + bga_ref[...]
    gb = jnp.dot(u, wgb_ref[...], preferred_element_type=F32) + bgb_ref[...]
    ya = jnp.dot(a_ref[...], wua_ref[...], preferred_element_type=F32)
    yb = jnp.dot(b_ref[...], wub_ref[...], preferred_element_type=F32)
    mixed = (jax.nn.sigmoid(ga) * ya + jax.nn.sigmoid(gb) * yb).astype(BF16)
    acc_ref[...] += jnp.dot(mixed, wo_ref[...], preferred_element_type=F32)

    @pl.when(c == pl.num_programs(1) - 1)
    def _():
        _residual_norm_rows(h_ref, post_g_ref, acc_ref, o_ref, MIX_TM, 1.0)


def _mix_out(u, ya, yb, h, w_gate, w_up_a, w_up_b, b_gate, w_out, post_g):
    t = u.shape[0]
    nc = D_MODEL // MIX_TC
    return pl.pallas_call(
        _mix_out_kernel,
        grid=(t // MIX_TM, nc),
        in_specs=[
            pl.BlockSpec((MIX_TM, D_MODEL), lambda i, c: (i, 0)),
            pl.BlockSpec((MIX_TM, A_Q_W), lambda i, c: (i, 0)),
            pl.BlockSpec((MIX_TM, B_W), lambda i, c: (i, 0)),
            pl.BlockSpec((MIX_TM, D_MODEL), lambda i, c: (i, 0)),
            pl.BlockSpec((D_MODEL, MIX_TC), lambda i, c: (0, c)),
            pl.BlockSpec((D_MODEL, MIX_TC), lambda i, c: (0, c + nc)),
            pl.BlockSpec((A_Q_W, MIX_TC), lambda i, c: (0, c)),
            pl.BlockSpec((B_W, MIX_TC), lambda i, c: (0, c)),
            pl.BlockSpec((1, MIX_TC), lambda i, c: (0, c)),
            pl.BlockSpec((1, MIX_TC), lambda i, c: (0, c + nc)),
            pl.BlockSpec((MIX_TC, D_MODEL), lambda i, c: (c, 0)),
            pl.BlockSpec((1, D_MODEL), lambda i, c: (0, 0)),
        ],
        out_specs=pl.BlockSpec((MIX_TM, D_MODEL), lambda i, c: (i, 0)),
        out_shape=jax.ShapeDtypeStruct((t, D_MODEL), F32),
        scratch_shapes=[pltpu.VMEM((MIX_TM, D_MODEL), F32)],
        compiler_params=_compiler_params(2),
        name="mix_out",
    )(u, ya, yb, h, w_gate, w_gate, w_up_a, w_up_b, b_gate, b_gate, w_out, post_g)


def _cast_mixer_w_in(src_ref, qkv_ref, gate_ref):
    for name in QKV_ORDER:
        lo, hi = QKV_SRC[name]
        blk = src_ref[:, lo:hi]
        if name in ("qa", "qb"):
            blk = blk * Q_PRESCALE
        qkv_ref[:, QKV_COL[name]:QKV_COL[name] + hi - lo] = blk.astype(BF16)
    gate_ref[...] = src_ref[:, QKV_W:].astype(BF16)


def kernel(x, ffn1_pre_g, ffn1_w_in, ffn1_w_out, ffn1_post_g, mix_pre_g, w_in, b_gate, sink_a, rpb_b,
           w_up_a, w_up_b, w_out, mix_post_g, ffn2_pre_g, ffn2_w_in, ffn2_w_out, ffn2_post_g):
    batch, seq, d = x.shape
    assert (seq, d) == (SEQ, D_MODEL)
    depth = ffn1_w_in.shape[0]
    tokens = batch * seq
    n_tiles, nf = tokens // FFN_TM, D_FF // FFN_FC
    h = x.reshape(tokens, d)

    def row(v):
        return v.reshape(1, -1).astype(F32)

    def grid_job(src, block, index_map):
        spec = pl.BlockSpec(block, index_map)
        return CastJob(src, spec, (spec,), (jax.ShapeDtypeStruct(src.shape, BF16),), _cast_block)

    for l in range(depth):
        jobs = (
            grid_job(ffn2_w_in[l], (D_MODEL // n_tiles, 2 * D_FF // nf), lambda i, j: (i, j)),
            grid_job(ffn2_w_out[l], (D_FF // nf, D_MODEL // n_tiles), lambda i, j: (j, i)),
            _tile_rows_job(w_in[l], n_tiles, nf, _cast_mixer_w_in, (QKV_W, 2 * D_MODEL)),
            _tile_rows_job(w_out[l], n_tiles, nf),
            _tile_rows_job(w_up_a[l], n_tiles, nf),
            _tile_rows_job(w_up_b[l], n_tiles, nf),
        )
        h, w2_in, w2_out, w_qkv, w_gate, w_o, w_ua, w_ub = _ffn(
            h, row(ffn1_pre_g[l]), ffn1_w_in[l].astype(BF16), ffn1_w_out[l].astype(BF16),
            row(ffn1_post_g[l]), jobs)
        u, qkv = _qkv_proj(h, row(mix_pre_g[l]), w_qkv)
        qkv = qkv.reshape(batch, seq, QKV_W)
        ya = _attn_win(qkv, sink_a[l].astype(F32), QKV_COL["qa"], QKV_COL["ka"], QKV_COL["va"])
        yb = _attn_nbr(qkv, rpb_b[l].astype(F32).reshape(B_HEADS, -1),
                       QKV_COL["qb"], QKV_COL["kb"], QKV_COL["vb"])
        h = _mix_out(u, ya.reshape(tokens, A_Q_W), yb.reshape(tokens, B_W), h, w_gate, w_ua, w_ub,
                     row(b_gate[l]), w_o, row(mix_post_g[l]))
        h, = _ffn(h, row(ffn2_pre_g[l]), w2_in, w2_out, row(ffn2_post_g[l]))
    return h.reshape(batch, seq, d)
```

```python
import functools
import math
from typing import Callable, NamedTuple

import jax
import jax.numpy as jnp
import numpy as np
from jax import lax
from jax.experimental import pallas as pl
from jax.experimental.pallas import tpu as pltpu

D_MODEL = 2048
SEQ = 4096
HEAD_DIM = 128
A_Q_HEADS = 8
A_KV_HEADS = 2
A_GROUP = A_Q_HEADS // A_KV_HEADS
A_WINDOW = 128
A_BLOCK = 128
B_HEADS = 8
GRID_W = 64
GRID_ROWS = SEQ // GRID_W
B_WIN_R = 8
B_WIN_C = 16
D_FF = 5632
MACARON_W = 0.5
RMS_EPS = 1e-6
NEG_INF = -1e30
SCALE = HEAD_DIM ** -0.5
LOG2E = math.log2(math.e)
Q_PRESCALE = SCALE * LOG2E

A_Q_W = A_Q_HEADS * HEAD_DIM
A_KV_W = A_KV_HEADS * HEAD_DIM
B_W = B_HEADS * HEAD_DIM
QKV_W = A_Q_W + 2 * A_KV_W + 3 * B_W

_cuts = np.cumsum([0, A_Q_W, A_KV_W, A_KV_W, B_W, B_W, B_W]).tolist()
QKV_SRC = {name: (_cuts[i], _cuts[i + 1]) for i, name in enumerate(("qa", "ka", "va", "qb", "kb", "vb"))}
QKV_ORDER = ("qa", "qb", "kb", "vb", "ka", "va")
QKV_COL = {}
for _name in QKV_ORDER:
    QKV_COL[_name] = sum(QKV_SRC[n][1] - QKV_SRC[n][0] for n in QKV_ORDER[:QKV_ORDER.index(_name)])

F32 = jnp.float32
BF16 = jnp.bfloat16

VMEM_LIMIT_BYTES = 61 * 1024 * 1024

FFN_TM = 1024
FFN_FC = 512
QKV_TM = 1024
QKV_TN = 1536
MIX_TM = 512
MIX_TC = 512
NORM_ROWS = 32
CAST_ROWS = 16

A_QB = 4
A_STEP = A_QB * A_BLOCK
B_PAIR_ROWS = 2
B_PAIR_TOK = B_PAIR_ROWS * GRID_W
B_KEY_ROWS = 10
B_KEY_TOK = B_KEY_ROWS * GRID_W
B_PAIRS_PER_STEP = 4
B_STEP = B_PAIRS_PER_STEP * B_PAIR_TOK
B_N_PAIRS = GRID_ROWS // B_PAIR_ROWS
B_VARIANT_PAIRS = (0, 1, 2, B_N_PAIRS - 2, B_N_PAIRS - 1)


def _alibi_slopes(n_heads):
    return [2.0 ** (-8.0 * (i + 1) / n_heads) for i in range(n_heads)]


def _compiler_params(n_axes):
    return pltpu.CompilerParams(dimension_semantics=("arbitrary",) * n_axes,
                                vmem_limit_bytes=VMEM_LIMIT_BYTES)


def _rms_scale(x):
    return x * lax.rsqrt(jnp.mean(x * x, axis=-1, keepdims=True) + RMS_EPS)


def _norm_rows_to(src_ref, gain_ref, dst_ref, rows):
    gain = gain_ref[...]
    for c in range(rows // NORM_ROWS):
        rs = slice(c * NORM_ROWS, (c + 1) * NORM_ROWS)
        dst_ref[rs, :] = (_rms_scale(src_ref[rs, :]) * gain).astype(dst_ref.dtype)


def _residual_norm_rows(res_ref, gain_ref, acc_ref, dst_ref, rows, weight):
    gain = weight * gain_ref[...]
    for c in range(rows // NORM_ROWS):
        rs = slice(c * NORM_ROWS, (c + 1) * NORM_ROWS)
        dst_ref[rs, :] = res_ref[rs, :] + _rms_scale(acc_ref[rs, :]) * gain


class CastJob(NamedTuple):
    src: jax.Array
    in_spec: pl.BlockSpec
    out_specs: tuple
    out_shapes: tuple
    body: Callable


def _cast_block(src_ref, dst_ref):
    dst_ref[...] = src_ref[...].astype(BF16)


def _tile_rows_job(src, n_tiles, n_steps, body=_cast_block, out_cols=None):
    rows, cols = src.shape
    per_tile = rows // (n_tiles * CAST_ROWS)
    assert per_tile * n_tiles * CAST_ROWS == rows and per_tile <= n_steps
    index_map = lambda i, j: (i * per_tile + jnp.minimum(j, per_tile - 1), 0)
    out_cols = (cols,) if out_cols is None else out_cols
    return CastJob(src, pl.BlockSpec((CAST_ROWS, cols), index_map),
                   tuple(pl.BlockSpec((CAST_ROWS, c), index_map) for c in out_cols),
                   tuple(jax.ShapeDtypeStruct((rows, c), BF16) for c in out_cols), body)


def _ffn_kernel(*refs, jobs):
    x_ref, pre_g_ref, wg_ref, wu_ref, wo_ref, post_g_ref = refs[:6]
    job_srcs = refs[6:6 + len(jobs)]
    o_ref = refs[6 + len(jobs)]
    job_dsts = refs[7 + len(jobs):-1]
    n_ref = refs[-1]
    j = pl.program_id(1)

    @pl.when(j == 0)
    def _():
        _norm_rows_to(x_ref, pre_g_ref, n_ref, FFN_TM)
        o_ref[...] = jnp.zeros_like(o_ref)

    n = n_ref[...]
    g = jnp.dot(n, wg_ref[...], preferred_element_type=F32)
    u = jnp.dot(n, wu_ref[...], preferred_element_type=F32)
    a = (g * jax.nn.sigmoid(g) * u).astype(BF16)
    o_ref[...] += jnp.dot(a, wo_ref[...], preferred_element_type=F32)

    first = 0
    for job, src_ref in zip(jobs, job_srcs):
        job.body(src_ref, *job_dsts[first:first + len(job.out_specs)])
        first += len(job.out_specs)

    @pl.when(j == pl.num_programs(1) - 1)
    def _():
        _residual_norm_rows(x_ref, post_g_ref, o_ref, o_ref, FFN_TM, MACARON_W)


def _ffn(x, pre_g, w_in, w_out, post_g, jobs=()):
    t = x.shape[0]
    nf = D_FF // FFN_FC
    out_specs = [pl.BlockSpec((FFN_TM, D_MODEL), lambda i, j: (i, 0))]
    out_shapes = [jax.ShapeDtypeStruct((t, D_MODEL), F32)]
    for job in jobs:
        out_specs.extend(job.out_specs)
        out_shapes.extend(job.out_shapes)
    return pl.pallas_call(
        functools.partial(_ffn_kernel, jobs=jobs),
        grid=(t // FFN_TM, nf),
        in_specs=[
            pl.BlockSpec((FFN_TM, D_MODEL), lambda i, j: (i, 0)),
            pl.BlockSpec((1, D_MODEL), lambda i, j: (0, 0)),
            pl.BlockSpec((D_MODEL, FFN_FC), lambda i, j: (0, j)),
            pl.BlockSpec((D_MODEL, FFN_FC), lambda i, j: (0, j + nf)),
            pl.BlockSpec((FFN_FC, D_MODEL), lambda i, j: (j, 0)),
            pl.BlockSpec((1, D_MODEL), lambda i, j: (0, 0)),
        ] + [job.in_spec for job in jobs],
        out_specs=out_specs,
        out_shape=out_shapes,
        scratch_shapes=[pltpu.VMEM((FFN_TM, D_MODEL), BF16)],
        compiler_params=_compiler_params(2),
        name="ffn",
    )(x, pre_g, w_in, w_in, w_out, post_g, *[job.src for job in jobs])


def _qkv_kernel(h_ref, g_ref, w_ref, u_ref, o_ref):
    @pl.when(pl.program_id(1) == 0)
    def _():
        _norm_rows_to(h_ref, g_ref, u_ref, QKV_TM)

    o_ref[...] = jnp.dot(u_ref[...], w_ref[...], preferred_element_type=F32).astype(BF16)


def _qkv_proj(h, gain, w_qkv):
    t = h.shape[0]
    return pl.pallas_call(
        _qkv_kernel,
        grid=(t // QKV_TM, QKV_W // QKV_TN),
        in_specs=[
            pl.BlockSpec((QKV_TM, D_MODEL), lambda i, j: (i, 0)),
            pl.BlockSpec((1, D_MODEL), lambda i, j: (0, 0)),
            pl.BlockSpec((D_MODEL, QKV_TN), lambda i, j: (0, j)),
        ],
        out_specs=[
            pl.BlockSpec((QKV_TM, D_MODEL), lambda i, j: (i, 0)),
            pl.BlockSpec((QKV_TM, QKV_TN), lambda i, j: (i, j)),
        ],
        out_shape=[jax.ShapeDtypeStruct((t, D_MODEL), BF16),
                   jax.ShapeDtypeStruct((t, QKV_W), BF16)],
        compiler_params=_compiler_params(2),
        name="qkv_proj",
    )(h, gain, w_qkv)


def _build_win_bias(tbl_ref):
    span = 3 * A_BLOCK
    qi = lax.broadcasted_iota(jnp.int32, (A_BLOCK, span), 0)
    kj = lax.broadcasted_iota(jnp.int32, (A_BLOCK, span), 1)
    absd_i = jnp.abs(qi + A_WINDOW - kj)
    absd = absd_i.astype(F32)
    in_window = absd_i <= A_WINDOW
    valid = (in_window & (kj >= A_BLOCK), in_window, in_window & (kj < 2 * A_BLOCK))
    for h, slope in enumerate(_alibi_slopes(A_Q_HEADS)):
        bias = (-slope * LOG2E) * absd
        for variant in range(3):
            tbl_ref[variant, h] = jnp.where(valid[variant], bias, NEG_INF)


def _attn_win_kernel(sink_ref, q_ref, kp_ref, kc_ref, kn_ref, vp_ref, vc_ref, vn_ref, o_ref,
                     k_scr, v_scr, tbl_ref):
    m = pl.program_id(1)

    @pl.when((pl.program_id(0) == 0) & (m == 0))
    def _():
        _build_win_bias(tbl_ref)

    k_scr[0:A_BLOCK, :] = kp_ref[...]
    k_scr[A_BLOCK:A_BLOCK + A_STEP, :] = kc_ref[...]
    k_scr[A_BLOCK + A_STEP:, :] = kn_ref[...]
    v_scr[0:A_BLOCK, :] = vp_ref[...]
    v_scr[A_BLOCK:A_BLOCK + A_STEP, :] = vc_ref[...]
    v_scr[A_BLOCK + A_STEP:, :] = vn_ref[...]
    span = 3 * A_BLOCK
    last_block = SEQ // A_BLOCK - 1

    def body(t, carry):
        r = pl.multiple_of(t * A_BLOCK, A_BLOCK)
        block = m * A_QB + t
        variant = jnp.where(block == 0, 0, jnp.where(block == last_block, 2, 1))
        for g in range(A_KV_HEADS):
            k = k_scr[pl.ds(r, span), g * HEAD_DIM:(g + 1) * HEAD_DIM]
            v = v_scr[pl.ds(r, span), g * HEAD_DIM:(g + 1) * HEAD_DIM]
            heads = [g * A_GROUP + e for e in range(A_GROUP)]
            qs = jnp.concatenate(
                [q_ref[pl.ds(r, A_BLOCK), h * HEAD_DIM:(h + 1) * HEAD_DIM] for h in heads], axis=0)
            s = lax.dot_general(qs, k, (((1,), (1,)), ((), ())), preferred_element_type=F32)
            probs, dens = [], []
            for e, h in enumerate(heads):
                logits = s[e * A_BLOCK:(e + 1) * A_BLOCK, :] + tbl_ref[variant, h]
                sink = sink_ref[h] * LOG2E
                mx = jnp.maximum(jnp.max(logits, axis=-1, keepdims=True), sink)
                p = jnp.exp2(logits - mx)
                dens.append(jnp.sum(p, axis=-1, keepdims=True) + jnp.exp2(sink - mx))
                probs.append(p.astype(BF16))
            o = jnp.dot(jnp.concatenate(probs, axis=0), v, preferred_element_type=F32)
            for e, h in enumerate(heads):
                o_ref[pl.ds(r, A_BLOCK), h * HEAD_DIM:(h + 1) * HEAD_DIM] = (
                    o[e * A_BLOCK:(e + 1) * A_BLOCK, :] / dens[e]).astype(BF16)
        return carry

    lax.fori_loop(0, A_QB, body, 0, unroll=True)


def _attn_win(qkv, sink, col_q, col_k, col_v):
    b = qkv.shape[0]
    n_steps = SEQ // A_STEP
    n_blocks = SEQ // A_BLOCK
    kcol, vcol = col_k // A_KV_W, col_v // A_KV_W

    def edge(col, shift):
        def index_map(bi, m):
            blk = jnp.clip(m * A_QB + shift, 0, n_blocks - 1)
            return (bi, blk, col)
        return pl.BlockSpec((None, A_BLOCK, A_KV_W), index_map)

    def centre(col):
        return pl.BlockSpec((None, A_STEP, A_KV_W), lambda bi, m: (bi, m, col))

    return pl.pallas_call(
        _attn_win_kernel,
        grid=(b, n_steps),
        in_specs=[
            pl.BlockSpec(memory_space=pltpu.SMEM),
            pl.BlockSpec((None, A_STEP, A_Q_W), lambda bi, m: (bi, m, col_q // A_Q_W)),
            edge(kcol, -1), centre(kcol), edge(kcol, A_QB),
            edge(vcol, -1), centre(vcol), edge(vcol, A_QB),
        ],
        out_specs=pl.BlockSpec((None, A_STEP, A_Q_W), lambda bi, m: (bi, m, 0)),
        out_shape=jax.ShapeDtypeStruct((b, SEQ, A_Q_W), BF16),
        scratch_shapes=[pltpu.VMEM((A_STEP + 2 * A_BLOCK, A_KV_W), BF16),
                        pltpu.VMEM((A_STEP + 2 * A_BLOCK, A_KV_W), BF16),
                        pltpu.VMEM((3, A_Q_HEADS, A_BLOCK, 3 * A_BLOCK), F32)],
        compiler_params=_compiler_params(2),
        name="attn_win",
    )(sink, qkv, qkv, qkv, qkv, qkv, qkv, qkv)


def _nbr_row_start(row):
    return min(max(row - B_WIN_R // 2, 0), GRID_ROWS - B_WIN_R)


def _nbr_key_row0(pair):
    return min(_nbr_row_start(pair * B_PAIR_ROWS), GRID_ROWS - B_KEY_ROWS)


def _build_nbr_bias(rpb_ref, tbl_ref, toep_l, toep_r):
    lanes = 2 * GRID_W
    qc = lax.broadcasted_iota(jnp.int32, (GRID_W, lanes), 0)
    lane = lax.broadcasted_iota(jnp.int32, (GRID_W, lanes), 1)
    col_start = jnp.clip(qc - B_WIN_C // 2, 0, GRID_W - B_WIN_C)
    left = lane < GRID_W
    neg = jnp.full((GRID_W, lanes), NEG_INF, F32)
    n_rel_r, n_rel_c = 2 * B_WIN_R - 1, 2 * B_WIN_C - 1

    def per_head(h, carry):
        for half, dst in ((0, toep_l), (1, toep_r)):
            kc = lane - half * GRID_W
            rel = kc - qc + (B_WIN_C - 1)
            col_ok = (kc >= col_start) & (kc < col_start + B_WIN_C) & (left if half == 0 else ~left)
            for a in range(n_rel_r):
                t = jnp.zeros((GRID_W, lanes), F32)
                for c in range(n_rel_c):
                    t = jnp.where(rel == c, rpb_ref[h, a * n_rel_c + c] * LOG2E, t)
                dst[a] = jnp.where(col_ok, t, neg)
        for vi, pair in enumerate(B_VARIANT_PAIRS):
            row0 = _nbr_key_row0(pair)
            for ql in range(B_PAIR_ROWS):
                q_row = pair * B_PAIR_ROWS + ql
                start = _nbr_row_start(q_row)
                for mt in range(B_KEY_ROWS // 2):
                    halves = []
                    for half, src in ((0, toep_l), (1, toep_r)):
                        k_row = row0 + 2 * mt + half
                        if start <= k_row < start + B_WIN_R:
                            halves.append(src[k_row - q_row + B_WIN_R - 1])
                        else:
                            halves.append(neg)
                    tbl_ref[vi, h, ql * GRID_W:(ql + 1) * GRID_W, mt * lanes:(mt + 1) * lanes] = (
                        jnp.where(left, halves[0], halves[1]))
        return carry

    lax.fori_loop(0, B_HEADS, per_head, 0)


def _attn_nbr_kernel(rpb_ref, q_ref, k_ref, v_ref, o_ref, tbl_ref, toep_l, toep_r):
    step = pl.program_id(1)

    @pl.when((pl.program_id(0) == 0) & (step == 0))
    def _():
        _build_nbr_bias(rpb_ref, tbl_ref, toep_l, toep_r)

    def body(pp, carry):
        pair = step * B_PAIRS_PER_STEP + pp
        row0 = jnp.clip(pair * B_PAIR_ROWS - B_WIN_R // 2, 0, GRID_ROWS - B_KEY_ROWS)
        variant = jnp.where(pair < 2, pair, jnp.where(pair >= B_N_PAIRS - 2, pair - (B_N_PAIRS - 5), 2))
        k0 = pl.multiple_of(row0 * GRID_W, GRID_W)
        q0 = pl.multiple_of(pp * B_PAIR_TOK, B_PAIR_TOK)
        head_cols = [slice(h * HEAD_DIM, (h + 1) * HEAD_DIM) for h in range(B_HEADS)]
        scores = [
            lax.dot_general(q_ref[pl.ds(q0, B_PAIR_TOK), cols], k_ref[pl.ds(k0, B_KEY_TOK), cols],
                            (((1,), (1,)), ((), ())), preferred_element_type=F32)
            for cols in head_cols]
        s = jnp.concatenate(scores, axis=0)
        logits = s + tbl_ref[variant].reshape(B_HEADS * B_PAIR_TOK, B_KEY_TOK)
        mx = jnp.max(logits, axis=-1, keepdims=True)
        p = jnp.exp2(logits - mx)
        inv_den = 1.0 / jnp.sum(p, axis=-1, keepdims=True)
        p = p.astype(BF16)
        for h, cols in enumerate(head_cols):
            rows = slice(h * B_PAIR_TOK, (h + 1) * B_PAIR_TOK)
            o = jnp.dot(p[rows, :], v_ref[pl.ds(k0, B_KEY_TOK), cols], preferred_element_type=F32)
            o_ref[pl.ds(q0, B_PAIR_TOK), cols] = (o * inv_den[rows, :]).astype(BF16)
        return carry

    lax.fori_loop(0, B_PAIRS_PER_STEP, body, 0, unroll=2)


def _attn_nbr(qkv, rpb, col_q, col_k, col_v):
    b = qkv.shape[0]
    return pl.pallas_call(
        _attn_nbr_kernel,
        grid=(b, SEQ // B_STEP),
        in_specs=[
            pl.BlockSpec(memory_space=pltpu.SMEM),
            pl.BlockSpec((None, B_STEP, B_W), lambda bi, s: (bi, s, col_q // B_W)),
            pl.BlockSpec((None, SEQ, B_W), lambda bi, s: (bi, 0, col_k // B_W)),
            pl.BlockSpec((None, SEQ, B_W), lambda bi, s: (bi, 0, col_v // B_W)),
        ],
        out_specs=pl.BlockSpec((None, B_STEP, B_W), lambda bi, s: (bi, s, 0)),
        out_shape=jax.ShapeDtypeStruct((b, SEQ, B_W), BF16),
        scratch_shapes=[
            pltpu.VMEM((len(B_VARIANT_PAIRS), B_HEADS, B_PAIR_TOK, B_KEY_TOK), F32),
            pltpu.VMEM((2 * B_WIN_R - 1, GRID_W, 2 * GRID_W), F32),
            pltpu.VMEM((2 * B_WIN_R - 1, GRID_W, 2 * GRID_W), F32),
        ],
        compiler_params=_compiler_params(2),
        name="attn_nbr",
    )(rpb, qkv, qkv, qkv)


def _mix_out_kernel(u_ref, a_ref, b_ref, h_ref, wga_ref, wgb_ref, wua_ref, wub_ref, bga_ref, bgb_ref,
                    wo_ref, post_g_ref, o_ref, acc_ref):
    c = pl.program_id(1)

    @pl.when(c == 0)
    def _():
        acc_ref[...] = jnp.zeros_like(acc_ref)

    u = u_ref[...]
    ga = jnp.dot(u, wga_ref[...], preferred_element_type=F32) + bga_ref[...]
    gb = jnp.dot(u, wgb_ref[...], preferred_element_type=F32) + bgb_ref[...]
    ya = jnp.dot(a_ref[...], wua_ref[...], preferred_element_type=F32)
    yb = jnp.dot(b_ref[...], wub_ref[...], preferred_element_type=F32)
    mixed = (jax.nn.sigmoid(ga) * ya + jax.nn.sigmoid(gb) * yb).astype(BF16)
    acc_ref[...] += jnp.dot(mixed, wo_ref[...], preferred_element_type=F32)

    @pl.when(c == pl.num_programs(1) - 1)
    def _():
        _residual_norm_rows(h_ref, post_g_ref, acc_ref, o_ref, MIX_TM, 1.0)


def _mix_out(u, ya, yb, h, w_gate, w_up_a, w_up_b, b_gate, w_out, post_g):
    t = u.shape[0]
    nc = D_MODEL // MIX_TC
    return pl.pallas_call(
        _mix_out_kernel,
        grid=(t // MIX_TM, nc),
        in_specs=[
            pl.BlockSpec((MIX_TM, D_MODEL), lambda i, c: (i, 0)),
            pl.BlockSpec((MIX_TM, A_Q_W), lambda i, c: (i, 0)),
            pl.BlockSpec((MIX_TM, B_W), lambda i, c: (i, 0)),
            pl.BlockSpec((MIX_TM, D_MODEL), lambda i, c: (i, 0)),
            pl.BlockSpec((D_MODEL, MIX_TC), lambda i, c: (0, c)),
            pl.BlockSpec((D_MODEL, MIX_TC), lambda i, c: (0, c + nc)),
            pl.BlockSpec((A_Q_W, MIX_TC), lambda i, c: (0, c)),
            pl.BlockSpec((B_W, MIX_TC), lambda i, c: (0, c)),
            pl.BlockSpec((1, MIX_TC), lambda i, c: (0, c)),
            pl.BlockSpec((1, MIX_TC), lambda i, c: (0, c + nc)),
            pl.BlockSpec((MIX_TC, D_MODEL), lambda i, c: (c, 0)),
            pl.BlockSpec((1, D_MODEL), lambda i, c: (0, 0)),
        ],
        out_specs=pl.BlockSpec((MIX_TM, D_MODEL), lambda i, c: (i, 0)),
        out_shape=jax.ShapeDtypeStruct((t, D_MODEL), F32),
        scratch_shapes=[pltpu.VMEM((MIX_TM, D_MODEL), F32)],
        compiler_params=_compiler_params(2),
        name="mix_out",
    )(u, ya, yb, h, w_gate, w_gate, w_up_a, w_up_b, b_gate, b_gate, w_out, post_g)


def _cast_mixer_w_in(src_ref, qkv_ref, gate_ref):
    for name in QKV_ORDER:
        lo, hi = QKV_SRC[name]
        blk = src_ref[:, lo:hi]
        if name in ("qa", "qb"):
            blk = blk * Q_PRESCALE
        qkv_ref[:, QKV_COL[name]:QKV_COL[name] + hi - lo] = blk.astype(BF16)
    gate_ref[...] = src_ref[:, QKV_W:].astype(BF16)


def kernel(x, ffn1_pre_g, ffn1_w_in, ffn1_w_out, ffn1_post_g, mix_pre_g, w_in, b_gate, sink_a, rpb_b,
           w_up_a, w_up_b, w_out, mix_post_g, ffn2_pre_g, ffn2_w_in, ffn2_w_out, ffn2_post_g):
    batch, seq, d = x.shape
    assert (seq, d) == (SEQ, D_MODEL)
    depth = ffn1_w_in.shape[0]
    tokens = batch * seq
    n_tiles, nf = tokens // FFN_TM, D_FF // FFN_FC
    h = x.reshape(tokens, d)

    def row(v):
        return v.reshape(1, -1).astype(F32)

    def grid_job(src, block, index_map):
        spec = pl.BlockSpec(block, index_map)
        return CastJob(src, spec, (spec,), (jax.ShapeDtypeStruct(src.shape, BF16),), _cast_block)

    for l in range(depth):
        jobs = (
            grid_job(ffn2_w_in[l], (D_MODEL // n_tiles, 2 * D_FF // nf), lambda i, j: (i, j)),
            grid_job(ffn2_w_out[l], (D_FF // nf, D_MODEL // n_tiles), lambda i, j: (j, i)),
            _tile_rows_job(w_in[l], n_tiles, nf, _cast_mixer_w_in, (QKV_W, 2 * D_MODEL)),
            _tile_rows_job(w_out[l], n_tiles, nf),
            _tile_rows_job(w_up_a[l], n_tiles, nf),
            _tile_rows_job(w_up_b[l], n_tiles, nf),
        )
        h, w2_in, w2_out, w_qkv, w_gate, w_o, w_ua, w_ub = _ffn(
            h, row(ffn1_pre_g[l]), ffn1_w_in[l].astype(BF16), ffn1_w_out[l].astype(BF16),
            row(ffn1_post_g[l]), jobs)
        u, qkv = _qkv_proj(h, row(mix_pre_g[l]), w_qkv)
        qkv = qkv.reshape(batch, seq, QKV_W)
        ya = _attn_win(qkv, sink_a[l].astype(F32), QKV_COL["qa"], QKV_COL["ka"], QKV_COL["va"])
        yb = _attn_nbr(qkv, rpb_b[l].astype(F32).reshape(B_HEADS, -1),
                       QKV_COL["qb"], QKV_COL["kb"], QKV_COL["vb"])
        h = _mix_out(u, ya.reshape(tokens, A_Q_W), yb.reshape(tokens, B_W), h, w_gate, w_ua, w_ub,
                     row(b_gate[l]), w_o, row(mix_post_g[l]))
        h, = _ffn(h, row(ffn2_pre_g[l]), w2_in, w2_out, row(ffn2_post_g[l]))
    return h.reshape(batch, seq, d)
```

```python
import functools
import math
from typing import Callable, NamedTuple

import jax
import jax.numpy as jnp
import numpy as np
from jax import lax
from jax.experimental import pallas as pl
from jax.experimental.pallas import tpu as pltpu

D_MODEL = 2048
SEQ = 4096
HEAD_DIM = 128
A_Q_HEADS = 8
A_KV_HEADS = 2
A_GROUP = A_Q_HEADS // A_KV_HEADS
A_WINDOW = 128
A_BLOCK = 128
B_HEADS = 8
GRID_W = 64
GRID_ROWS = SEQ // GRID_W
B_WIN_R = 8
B_WIN_C = 16
D_FF = 5632
MACARON_W = 0.5
RMS_EPS = 1e-6
NEG_INF = -1e30
SCALE = HEAD_DIM ** -0.5
LOG2E = math.log2(math.e)
Q_PRESCALE = SCALE * LOG2E

A_Q_W = A_Q_HEADS * HEAD_DIM
A_KV_W = A_KV_HEADS * HEAD_DIM
B_W = B_HEADS * HEAD_DIM
QKV_W = A_Q_W + 2 * A_KV_W + 3 * B_W

_cuts = np.cumsum([0, A_Q_W, A_KV_W, A_KV_W, B_W, B_W, B_W]).tolist()
QKV_SRC = {name: (_cuts[i], _cuts[i + 1]) for i, name in enumerate(("qa", "ka", "va", "qb", "kb", "vb"))}
QKV_ORDER = ("qa", "qb", "kb", "vb", "ka", "va")
QKV_COL = {}
for _name in QKV_ORDER:
    QKV_COL[_name] = sum(QKV_SRC[n][1] - QKV_SRC[n][0] for n in QKV_ORDER[:QKV_ORDER.index(_name)])

F32 = jnp.float32
BF16 = jnp.bfloat16

VMEM_LIMIT_BYTES = 61 * 1024 * 1024

FFN_TM = 1024
FFN_FC = 512
QKV_TM = 1024
QKV_TN = 1536
OUT_TM = 1024
OUT_ROWS = 256
NORM_ROWS = 32
CAST_ROWS = 16

GATE_TC = 512
A_QB = 4
A_STEP = A_QB * A_BLOCK
B_PAIR_ROWS = 2
B_PAIR_TOK = B_PAIR_ROWS * GRID_W
B_KEY_ROWS = 10
B_KEY_TOK = B_KEY_ROWS * GRID_W
B_PAIRS_PER_STEP = 2
B_STEP = B_PAIRS_PER_STEP * B_PAIR_TOK
B_N_PAIRS = GRID_ROWS // B_PAIR_ROWS
B_WINDOW_ALIGN = 256
B_WINDOW_TOK = B_STEP + 2 * B_WINDOW_ALIGN
B_VARIANT_PAIRS = (0, 1, 2, B_N_PAIRS - 2, B_N_PAIRS - 1)


def _alibi_slopes(n_heads):
    return [2.0 ** (-8.0 * (i + 1) / n_heads) for i in range(n_heads)]


def _compiler_params(n_axes):
    return pltpu.CompilerParams(dimension_semantics=("arbitrary",) * n_axes,
                                vmem_limit_bytes=VMEM_LIMIT_BYTES)


def _rms_scale(x):
    return x * lax.rsqrt(jnp.mean(x * x, axis=-1, keepdims=True) + RMS_EPS)


def _norm_rows_to(src_ref, gain_ref, dst_ref, rows):
    gain = gain_ref[...]
    for c in range(rows // NORM_ROWS):
        rs = slice(c * NORM_ROWS, (c + 1) * NORM_ROWS)
        dst_ref[rs, :] = (_rms_scale(src_ref[rs, :]) * gain).astype(dst_ref.dtype)


def _residual_norm_rows(res_ref, gain_ref, acc_ref, dst_ref, rows, weight):
    gain = weight * gain_ref[...]
    for c in range(rows // NORM_ROWS):
        rs = slice(c * NORM_ROWS, (c + 1) * NORM_ROWS)
        dst_ref[rs, :] = res_ref[rs, :] + _rms_scale(acc_ref[rs, :]) * gain


class CastJob(NamedTuple):
    src: jax.Array
    in_spec: pl.BlockSpec
    out_specs: tuple
    out_shapes: tuple
    body: Callable


def _cast_block(src_ref, dst_ref):
    dst_ref[...] = src_ref[...].astype(BF16)


def _tile_rows_job(src, n_tiles, n_steps, body=_cast_block, out_cols=None):
    rows, cols = src.shape
    per_tile = rows // (n_tiles * CAST_ROWS)
    assert per_tile * n_tiles * CAST_ROWS == rows and per_tile <= n_steps
    index_map = lambda i, j: (i * per_tile + jnp.minimum(j, per_tile - 1), 0)
    out_cols = (cols,) if out_cols is None else out_cols
    return CastJob(src, pl.BlockSpec((CAST_ROWS, cols), index_map),
                   tuple(pl.BlockSpec((CAST_ROWS, c), index_map) for c in out_cols),
                   tuple(jax.ShapeDtypeStruct((rows, c), BF16) for c in out_cols), body)


def _ffn_kernel(*refs, jobs):
    x_ref, pre_g_ref, wg_ref, wu_ref, wo_ref, post_g_ref = refs[:6]
    job_srcs = refs[6:6 + len(jobs)]
    o_ref = refs[6 + len(jobs)]
    job_dsts = refs[7 + len(jobs):-1]
    n_ref = refs[-1]
    j = pl.program_id(1)

    @pl.when(j == 0)
    def _():
        _norm_rows_to(x_ref, pre_g_ref, n_ref, FFN_TM)
        o_ref[...] = jnp.zeros_like(o_ref)

    n = n_ref[...]
    g = jnp.dot(n, wg_ref[...], preferred_element_type=F32)
    u = jnp.dot(n, wu_ref[...], preferred_element_type=F32)
    a = (g * jax.nn.sigmoid(g) * u).astype(BF16)
    o_ref[...] += jnp.dot(a, wo_ref[...], preferred_element_type=F32)

    first = 0
    for job, src_ref in zip(jobs, job_srcs):
        job.body(src_ref, *job_dsts[first:first + len(job.out_specs)])
        first += len(job.out_specs)

    @pl.when(j == pl.num_programs(1) - 1)
    def _():
        _residual_norm_rows(x_ref, post_g_ref, o_ref, o_ref, FFN_TM, MACARON_W)


def _ffn(x, pre_g, w_in, w_out, post_g, jobs=()):
    t = x.shape[0]
    nf = D_FF // FFN_FC
    out_specs = [pl.BlockSpec((FFN_TM, D_MODEL), lambda i, j: (i, 0))]
    out_shapes = [jax.ShapeDtypeStruct((t, D_MODEL), F32)]
    for job in jobs:
        out_specs.extend(job.out_specs)
        out_shapes.extend(job.out_shapes)
    return pl.pallas_call(
        functools.partial(_ffn_kernel, jobs=jobs),
        grid=(t // FFN_TM, nf),
        in_specs=[
            pl.BlockSpec((FFN_TM, D_MODEL), lambda i, j: (i, 0)),
            pl.BlockSpec((1, D_MODEL), lambda i, j: (0, 0)),
            pl.BlockSpec((D_MODEL, FFN_FC), lambda i, j: (0, j)),
            pl.BlockSpec((D_MODEL, FFN_FC), lambda i, j: (0, j + nf)),
            pl.BlockSpec((FFN_FC, D_MODEL), lambda i, j: (j, 0)),
            pl.BlockSpec((1, D_MODEL), lambda i, j: (0, 0)),
        ] + [job.in_spec for job in jobs],
        out_specs=out_specs,
        out_shape=out_shapes,
        scratch_shapes=[pltpu.VMEM((FFN_TM, D_MODEL), BF16)],
        compiler_params=_compiler_params(2),
        name="ffn",
    )(x, pre_g, w_in, w_in, w_out, post_g, *[job.src for job in jobs])


def _qkv_kernel(h_ref, g_ref, w_ref, u_ref, o_ref):
    @pl.when(pl.program_id(1) == 0)
    def _():
        _norm_rows_to(h_ref, g_ref, u_ref, QKV_TM)

    o_ref[...] = jnp.dot(u_ref[...], w_ref[...], preferred_element_type=F32).astype(BF16)


def _qkv_proj(h, gain, w_qkv):
    t = h.shape[0]
    return pl.pallas_call(
        _qkv_kernel,
        grid=(t // QKV_TM, QKV_W // QKV_TN),
        in_specs=[
            pl.BlockSpec((QKV_TM, D_MODEL), lambda i, j: (i, 0)),
            pl.BlockSpec((1, D_MODEL), lambda i, j: (0, 0)),
            pl.BlockSpec((D_MODEL, QKV_TN), lambda i, j: (0, j)),
        ],
        out_specs=[
            pl.BlockSpec((QKV_TM, D_MODEL), lambda i, j: (i, 0)),
            pl.BlockSpec((QKV_TM, QKV_TN), lambda i, j: (i, j)),
        ],
        out_shape=[jax.ShapeDtypeStruct((t, D_MODEL), BF16),
                   jax.ShapeDtypeStruct((t, QKV_W), BF16)],
        compiler_params=_compiler_params(2),
        name="qkv_proj",
    )(h, gain, w_qkv)


def _build_win_bias(tbl_ref):
    span = 3 * A_BLOCK
    qi = lax.broadcasted_iota(jnp.int32, (A_BLOCK, span), 0)
    kj = lax.broadcasted_iota(jnp.int32, (A_BLOCK, span), 1)
    absd_i = jnp.abs(qi + A_WINDOW - kj)
    absd = absd_i.astype(F32)
    in_window = absd_i <= A_WINDOW
    valid = (in_window & (kj >= A_BLOCK), in_window, in_window & (kj < 2 * A_BLOCK))
    for h, slope in enumerate(_alibi_slopes(A_Q_HEADS)):
        bias = (-slope * LOG2E) * absd
        for variant in range(3):
            tbl_ref[variant, h] = jnp.where(valid[variant], bias, NEG_INF)


def _attn_win_kernel(sink_ref, q_ref, kp_ref, kc_ref, kn_ref, vp_ref, vc_ref, vn_ref, u_ref, wg_ref, wup_ref,
                     bg_ref, o_ref, k_scr, v_scr, tbl_ref, a_scr):
    m = pl.program_id(1)

    @pl.when((pl.program_id(0) == 0) & (m == 0))
    def _():
        _build_win_bias(tbl_ref)

    k_scr[0:A_BLOCK, :] = kp_ref[...]
    k_scr[A_BLOCK:A_BLOCK + A_STEP, :] = kc_ref[...]
    k_scr[A_BLOCK + A_STEP:, :] = kn_ref[...]
    v_scr[0:A_BLOCK, :] = vp_ref[...]
    v_scr[A_BLOCK:A_BLOCK + A_STEP, :] = vc_ref[...]
    v_scr[A_BLOCK + A_STEP:, :] = vn_ref[...]
    span = 3 * A_BLOCK
    last_block = SEQ // A_BLOCK - 1
    piece = D_MODEL // (A_QB * A_KV_HEADS)

    def gate_piece(i):
        cols = slice(i * piece, (i + 1) * piece)
        gate = jnp.dot(u_ref[...], wg_ref[:, cols], preferred_element_type=F32) + bg_ref[:, cols]
        o_ref[:, cols] = jax.nn.sigmoid(gate)

    def sub_block(t):
        r = t * A_BLOCK
        block = m * A_QB + t
        variant = jnp.where(block == 0, 0, jnp.where(block == last_block, 2, 1))
        for g in range(A_KV_HEADS):
            k = k_scr[pl.ds(r, span), g * HEAD_DIM:(g + 1) * HEAD_DIM]
            v = v_scr[pl.ds(r, span), g * HEAD_DIM:(g + 1) * HEAD_DIM]
            heads = [g * A_GROUP + e for e in range(A_GROUP)]
            qs = jnp.concatenate(
                [q_ref[pl.ds(r, A_BLOCK), h * HEAD_DIM:(h + 1) * HEAD_DIM] for h in heads], axis=0)
            s = lax.dot_general(qs, k, (((1,), (1,)), ((), ())), preferred_element_type=F32)
            gate_piece(t * A_KV_HEADS + g)
            probs, dens = [], []
            for e, h in enumerate(heads):
                logits = s[e * A_BLOCK:(e + 1) * A_BLOCK, :] + tbl_ref[variant, h]
                sink = sink_ref[h] * LOG2E
                mx = jnp.maximum(jnp.max(logits, axis=-1, keepdims=True), sink)
                p = jnp.exp2(logits - mx)
                dens.append(jnp.sum(p, axis=-1, keepdims=True) + jnp.exp2(sink - mx))
                probs.append(p.astype(BF16))
            o = jnp.dot(jnp.concatenate(probs, axis=0), v, preferred_element_type=F32)
            for e, h in enumerate(heads):
                a_scr[pl.ds(r, A_BLOCK), h * HEAD_DIM:(h + 1) * HEAD_DIM] = (
                    o[e * A_BLOCK:(e + 1) * A_BLOCK, :] / dens[e]).astype(BF16)

    for t in range(A_QB):
        sub_block(t)
    attn = a_scr[...]
    for c in range(D_MODEL // GATE_TC):
        cols = slice(c * GATE_TC, (c + 1) * GATE_TC)
        o_ref[:, cols] = o_ref[:, cols] * jnp.dot(attn, wup_ref[:, cols], preferred_element_type=F32)


def _attn_win(qkv, sink, col_q, col_k, col_v, u, w_gate, w_up, b_gate):
    b = qkv.shape[0]
    n_steps = SEQ // A_STEP
    n_blocks = SEQ // A_BLOCK
    kcol, vcol = col_k // A_KV_W, col_v // A_KV_W

    def edge(col, shift):
        def index_map(bi, m):
            blk = jnp.clip(m * A_QB + shift, 0, n_blocks - 1)
            return (bi, blk, col)
        return pl.BlockSpec((None, A_BLOCK, A_KV_W), index_map)

    def centre(col):
        return pl.BlockSpec((None, A_STEP, A_KV_W), lambda bi, m: (bi, m, col))

    return pl.pallas_call(
        _attn_win_kernel,
        grid=(b, n_steps),
        in_specs=[
            pl.BlockSpec(memory_space=pltpu.SMEM),
            pl.BlockSpec((None, A_STEP, A_Q_W), lambda bi, m: (bi, m, col_q // A_Q_W)),
            edge(kcol, -1), centre(kcol), edge(kcol, A_QB),
            edge(vcol, -1), centre(vcol), edge(vcol, A_QB),
            pl.BlockSpec((None, A_STEP, D_MODEL), lambda bi, m: (bi, m, 0)),
            pl.BlockSpec((D_MODEL, D_MODEL), lambda bi, m: (0, 0)),
            pl.BlockSpec((A_Q_W, D_MODEL), lambda bi, m: (0, 0)),
            pl.BlockSpec((1, D_MODEL), lambda bi, m: (0, 0)),
        ],
        out_specs=pl.BlockSpec((None, A_STEP, D_MODEL), lambda bi, m: (bi, m, 0)),
        out_shape=jax.ShapeDtypeStruct((b, SEQ, D_MODEL), F32),
        scratch_shapes=[pltpu.VMEM((A_STEP + 2 * A_BLOCK, A_KV_W), BF16),
                        pltpu.VMEM((A_STEP + 2 * A_BLOCK, A_KV_W), BF16),
                        pltpu.VMEM((3, A_Q_HEADS, A_BLOCK, 3 * A_BLOCK), F32),
                        pltpu.VMEM((A_STEP, A_Q_W), BF16)],
        compiler_params=_compiler_params(2),
        name="attn_win",
    )(sink, qkv, qkv, qkv, qkv, qkv, qkv, qkv, u, w_gate, w_up, b_gate)


def _nbr_row_start(row):
    return min(max(row - B_WIN_R // 2, 0), GRID_ROWS - B_WIN_R)


def _nbr_key_row0(pair):
    return min(_nbr_row_start(pair * B_PAIR_ROWS), GRID_ROWS - B_KEY_ROWS)


def _build_nbr_bias(rpb_ref, tbl_ref, toep_l, toep_r):
    lanes = 2 * GRID_W
    qc = lax.broadcasted_iota(jnp.int32, (GRID_W, lanes), 0)
    lane = lax.broadcasted_iota(jnp.int32, (GRID_W, lanes), 1)
    col_start = jnp.clip(qc - B_WIN_C // 2, 0, GRID_W - B_WIN_C)
    left = lane < GRID_W
    neg = jnp.full((GRID_W, lanes), NEG_INF, F32)
    n_rel_r, n_rel_c = 2 * B_WIN_R - 1, 2 * B_WIN_C - 1

    def per_head(h, carry):
        for half, dst in ((0, toep_l), (1, toep_r)):
            kc = lane - half * GRID_W
            rel = kc - qc + (B_WIN_C - 1)
            col_ok = (kc >= col_start) & (kc < col_start + B_WIN_C) & (left if half == 0 else ~left)
            for a in range(n_rel_r):
                t = jnp.zeros((GRID_W, lanes), F32)
                for c in range(n_rel_c):
                    t = jnp.where(rel == c, rpb_ref[h, a * n_rel_c + c] * LOG2E, t)
                dst[a] = jnp.where(col_ok, t, neg)
        for vi, pair in enumerate(B_VARIANT_PAIRS):
            row0 = _nbr_key_row0(pair)
            for ql in range(B_PAIR_ROWS):
                q_row = pair * B_PAIR_ROWS + ql
                start = _nbr_row_start(q_row)
                for mt in range(B_KEY_ROWS // 2):
                    halves = []
                    for half, src in ((0, toep_l), (1, toep_r)):
                        k_row = row0 + 2 * mt + half
                        if start <= k_row < start + B_WIN_R:
                            halves.append(src[k_row - q_row + B_WIN_R - 1])
                        else:
                            halves.append(neg)
                    tbl_ref[vi, h, ql * GRID_W:(ql + 1) * GRID_W, mt * lanes:(mt + 1) * lanes] = (
                        jnp.where(left, halves[0], halves[1]))
        return carry

    lax.fori_loop(0, B_HEADS, per_head, 0)


def _nbr_window_block(step):
    return jnp.clip(step * (B_STEP // B_WINDOW_ALIGN) - 1, 0, (SEQ - B_WINDOW_TOK) // B_WINDOW_ALIGN)


def _attn_nbr_kernel(rpb_ref, q_ref, k_ref, v_ref, u_ref, wg_ref, wup_ref, bg_ref, sa_ref, o_ref,
                     tbl_ref, toep_l, toep_r, attn_scr, sig_scr):
    step = pl.program_id(1)

    @pl.when((pl.program_id(0) == 0) & (step == 0))
    def _():
        _build_nbr_bias(rpb_ref, tbl_ref, toep_l, toep_r)

    window_row0 = _nbr_window_block(step) * (B_WINDOW_ALIGN // GRID_W)
    piece = D_MODEL // B_PAIRS_PER_STEP
    head_cols = [slice(h * HEAD_DIM, (h + 1) * HEAD_DIM) for h in range(B_HEADS)]

    def gate_piece(i):
        cols = slice(i * piece, (i + 1) * piece)
        gate = jnp.dot(u_ref[...], wg_ref[:, cols], preferred_element_type=F32) + bg_ref[:, cols]
        sig_scr[:, cols] = jax.nn.sigmoid(gate)

    for pp in range(B_PAIRS_PER_STEP):
        pair = step * B_PAIRS_PER_STEP + pp
        row0 = jnp.clip(pair * B_PAIR_ROWS - B_WIN_R // 2, 0, GRID_ROWS - B_KEY_ROWS)
        variant = jnp.where(pair < 2, pair, jnp.where(pair >= B_N_PAIRS - 2, pair - (B_N_PAIRS - 5), 2))
        k0 = pl.multiple_of((row0 - window_row0) * GRID_W, GRID_W)
        q_rows = slice(pp * B_PAIR_TOK, (pp + 1) * B_PAIR_TOK)
        scores = [
            lax.dot_general(q_ref[q_rows, cols], k_ref[0, pl.ds(k0, B_KEY_TOK), cols],
                            (((1,), (1,)), ((), ())), preferred_element_type=F32)
            for cols in head_cols]
        gate_piece(pp)
        s = jnp.concatenate(scores, axis=0)
        logits = s + tbl_ref[variant].reshape(B_HEADS * B_PAIR_TOK, B_KEY_TOK)
        mx = jnp.max(logits, axis=-1, keepdims=True)
        p = jnp.exp2(logits - mx)
        inv_den = 1.0 / jnp.sum(p, axis=-1, keepdims=True)
        p = p.astype(BF16)
        for h, cols in enumerate(head_cols):
            rows = slice(h * B_PAIR_TOK, (h + 1) * B_PAIR_TOK)
            o = jnp.dot(p[rows, :], v_ref[0, pl.ds(k0, B_KEY_TOK), cols], preferred_element_type=F32)
            attn_scr[q_rows, cols] = (o * inv_den[rows, :]).astype(BF16)

    attn = attn_scr[...]
    for c in range(D_MODEL // GATE_TC):
        cols = slice(c * GATE_TC, (c + 1) * GATE_TC)
        up = jnp.dot(attn, wup_ref[:, cols], preferred_element_type=F32)
        o_ref[:, cols] = (sa_ref[:, cols] + sig_scr[:, cols] * up).astype(BF16)


def _attn_nbr(qkv, rpb, col_q, col_k, col_v, u, w_gate, w_up, b_gate, sa):
    b = qkv.shape[0]
    assert B_WINDOW_ALIGN == (B_WIN_R // 2) * GRID_W and B_STEP % B_WINDOW_ALIGN == 0

    def window(col):
        return pl.BlockSpec((pl.Element(1), pl.Element(B_WINDOW_TOK), pl.Element(B_W)),
                            lambda bi, s: (bi, _nbr_window_block(s) * B_WINDOW_ALIGN, col))

    resident = dict(pipeline_mode=pl.Buffered(1))
    return pl.pallas_call(
        _attn_nbr_kernel,
        grid=(b, SEQ // B_STEP),
        in_specs=[
            pl.BlockSpec(memory_space=pltpu.SMEM),
            pl.BlockSpec((None, B_STEP, B_W), lambda bi, s: (bi, s, col_q // B_W)),
            window(col_k), window(col_v),
            pl.BlockSpec((None, B_STEP, D_MODEL), lambda bi, s: (bi, s, 0)),
            pl.BlockSpec((D_MODEL, D_MODEL), lambda bi, s: (0, 1), **resident),
            pl.BlockSpec((B_W, D_MODEL), lambda bi, s: (0, 0), **resident),
            pl.BlockSpec((1, D_MODEL), lambda bi, s: (0, 1)),
            pl.BlockSpec((None, B_STEP, D_MODEL), lambda bi, s: (bi, s, 0)),
        ],
        out_specs=pl.BlockSpec((None, B_STEP, D_MODEL), lambda bi, s: (bi, s, 0)),
        out_shape=jax.ShapeDtypeStruct((b, SEQ, D_MODEL), BF16),
        scratch_shapes=[
            pltpu.VMEM((len(B_VARIANT_PAIRS), B_HEADS, B_PAIR_TOK, B_KEY_TOK), F32),
            pltpu.VMEM((2 * B_WIN_R - 1, GRID_W, 2 * GRID_W), F32),
            pltpu.VMEM((2 * B_WIN_R - 1, GRID_W, 2 * GRID_W), F32),
            pltpu.VMEM((B_STEP, B_W), BF16),
            pltpu.VMEM((B_STEP, D_MODEL), F32),
        ],
        compiler_params=_compiler_params(2),
        name="attn_nbr",
    )(rpb, qkv, qkv, qkv, u, w_gate, w_up, b_gate, sa)


def _out_proj_kernel(m_ref, h_ref, w_ref, post_g_ref, o_ref):
    gain = post_g_ref[...]
    for r in range(OUT_TM // OUT_ROWS):
        y = jnp.dot(m_ref[r * OUT_ROWS:(r + 1) * OUT_ROWS, :], w_ref[...], preferred_element_type=F32)
        for c in range(OUT_ROWS // NORM_ROWS):
            rs = slice(r * OUT_ROWS + c * NORM_ROWS, r * OUT_ROWS + (c + 1) * NORM_ROWS)
            o_ref[rs, :] = h_ref[rs, :] + _rms_scale(y[c * NORM_ROWS:(c + 1) * NORM_ROWS, :]) * gain


def _out_proj(m, h, w_out, post_g):
    t = m.shape[0]
    return pl.pallas_call(
        _out_proj_kernel,
        grid=(t // OUT_TM,),
        in_specs=[
            pl.BlockSpec((OUT_TM, D_MODEL), lambda i: (i, 0)),
            pl.BlockSpec((OUT_TM, D_MODEL), lambda i: (i, 0)),
            pl.BlockSpec((D_MODEL, D_MODEL), lambda i: (0, 0), pipeline_mode=pl.Buffered(1)),
            pl.BlockSpec((1, D_MODEL), lambda i: (0, 0)),
        ],
        out_specs=pl.BlockSpec((OUT_TM, D_MODEL), lambda i: (i, 0)),
        out_shape=jax.ShapeDtypeStruct((t, D_MODEL), F32),
        compiler_params=_compiler_params(1),
        name="out_proj",
    )(m, h, w_out, post_g)


def _cast_mixer_w_in(src_ref, qkv_ref, gate_ref):
    for name in QKV_ORDER:
        lo, hi = QKV_SRC[name]
        blk = src_ref[:, lo:hi]
        if name in ("qa", "qb"):
            blk = blk * Q_PRESCALE
        qkv_ref[:, QKV_COL[name]:QKV_COL[name] + hi - lo] = blk.astype(BF16)
    gate_ref[...] = src_ref[:, QKV_W:].astype(BF16)


def kernel(x, ffn1_pre_g, ffn1_w_in, ffn1_w_out, ffn1_post_g, mix_pre_g, w_in, b_gate, sink_a, rpb_b,
           w_up_a, w_up_b, w_out, mix_post_g, ffn2_pre_g, ffn2_w_in, ffn2_w_out, ffn2_post_g):
    batch, seq, d = x.shape
    assert (seq, d) == (SEQ, D_MODEL)
    depth = ffn1_w_in.shape[0]
    tokens = batch * seq
    n_tiles, nf = tokens // FFN_TM, D_FF // FFN_FC
    h = x.reshape(tokens, d)

    def row(v):
        return v.reshape(1, -1).astype(F32)

    def grid_job(src, block, index_map):
        spec = pl.BlockSpec(block, index_map)
        return CastJob(src, spec, (spec,), (jax.ShapeDtypeStruct(src.shape, BF16),), _cast_block)

    for l in range(depth):
        jobs = (
            grid_job(ffn2_w_in[l], (D_MODEL // n_tiles, 2 * D_FF // nf), lambda i, j: (i, j)),
            grid_job(ffn2_w_out[l], (D_FF // nf, D_MODEL // n_tiles), lambda i, j: (j, i)),
            _tile_rows_job(w_in[l], n_tiles, nf, _cast_mixer_w_in, (QKV_W, 2 * D_MODEL)),
            _tile_rows_job(w_out[l], n_tiles, nf),
            _tile_rows_job(w_up_a[l], n_tiles, nf),
            _tile_rows_job(w_up_b[l], n_tiles, nf),
        )
        h, w2_in, w2_out, w_qkv, w_gate, w_o, w_ua, w_ub = _ffn(
            h, row(ffn1_pre_g[l]), ffn1_w_in[l].astype(BF16), ffn1_w_out[l].astype(BF16),
            row(ffn1_post_g[l]), jobs)
        u, qkv = _qkv_proj(h, row(mix_pre_g[l]), w_qkv)
        qkv = qkv.reshape(batch, seq, QKV_W)
        u = u.reshape(batch, seq, d)
        sa = _attn_win(qkv, sink_a[l].astype(F32), QKV_COL["qa"], QKV_COL["ka"], QKV_COL["va"],
                       u, w_gate, w_ua, row(b_gate[l]))
        mixed = _attn_nbr(qkv, rpb_b[l].astype(F32).reshape(B_HEADS, -1),
                          QKV_COL["qb"], QKV_COL["kb"], QKV_COL["vb"],
                          u, w_gate, w_ub, row(b_gate[l]), sa)
        h = _out_proj(mixed.reshape(tokens, d), h, w_o, row(mix_post_g[l]))
        h, = _ffn(h, row(ffn2_pre_g[l]), w2_in, w2_out, row(ffn2_post_g[l]))
    return h.reshape(batch, seq, d)
```

```python
import functools
import math
from typing import Callable, NamedTuple

import jax
import jax.numpy as jnp
import numpy as np
from jax import lax
from jax.experimental import pallas as pl
from jax.experimental.pallas import tpu as pltpu

D_MODEL = 2048
SEQ = 4096
HEAD_DIM = 128
A_Q_HEADS = 8
A_KV_HEADS = 2
A_GROUP = A_Q_HEADS // A_KV_HEADS
A_WINDOW = 128
A_BLOCK = 128
B_HEADS = 8
GRID_W = 64
GRID_ROWS = SEQ // GRID_W
B_WIN_R = 8
B_WIN_C = 16
D_FF = 5632
MACARON_W = 0.5
RMS_EPS = 1e-6
NEG_INF = -1e30
SCALE = HEAD_DIM ** -0.5
LOG2E = math.log2(math.e)
Q_PRESCALE = SCALE * LOG2E

A_Q_W = A_Q_HEADS * HEAD_DIM
A_KV_W = A_KV_HEADS * HEAD_DIM
B_W = B_HEADS * HEAD_DIM
QKV_W = A_Q_W + 2 * A_KV_W + 3 * B_W

_cuts = np.cumsum([0, A_Q_W, A_KV_W, A_KV_W, B_W, B_W, B_W]).tolist()
QKV_SRC = {name: (_cuts[i], _cuts[i + 1]) for i, name in enumerate(("qa", "ka", "va", "qb", "kb", "vb"))}
QKV_ORDER = ("qa", "qb", "kb", "vb", "ka", "va")
QKV_COL = {}
for _name in QKV_ORDER:
    QKV_COL[_name] = sum(QKV_SRC[n][1] - QKV_SRC[n][0] for n in QKV_ORDER[:QKV_ORDER.index(_name)])

F32 = jnp.float32
BF16 = jnp.bfloat16

VMEM_LIMIT_BYTES = 61 * 1024 * 1024

FFN_TM = 1024
FFN_FC = 512
FFN_ROWS = 256
QKV_TM = 1024
QKV_TN = 1536
OUT_TM = 1024
OUT_ROWS = 256
NORM_ROWS = 32
CAST_ROWS = 16

GATE_TC = 512
A_QB = 4
A_STEP = A_QB * A_BLOCK
B_PAIR_ROWS = 2
B_PAIR_TOK = B_PAIR_ROWS * GRID_W
B_KEY_ROWS = 10
B_KEY_TOK = B_KEY_ROWS * GRID_W
B_PAIRS_PER_STEP = 2
B_STEP = B_PAIRS_PER_STEP * B_PAIR_TOK
B_N_PAIRS = GRID_ROWS // B_PAIR_ROWS
B_WINDOW_ALIGN = 256
B_WINDOW_TOK = B_STEP + 2 * B_WINDOW_ALIGN
B_VARIANT_PAIRS = (0, 1, 2, B_N_PAIRS - 2, B_N_PAIRS - 1)


def _alibi_slopes(n_heads):
    return [2.0 ** (-8.0 * (i + 1) / n_heads) for i in range(n_heads)]


def _compiler_params(n_axes):
    return pltpu.CompilerParams(dimension_semantics=("arbitrary",) * n_axes,
                                vmem_limit_bytes=VMEM_LIMIT_BYTES)


def _rms_scale(x):
    return x * lax.rsqrt(jnp.mean(x * x, axis=-1, keepdims=True) + RMS_EPS)


def _norm_rows_to(src_ref, gain_ref, dst_ref, rows):
    gain = gain_ref[...]
    for c in range(rows // NORM_ROWS):
        rs = slice(c * NORM_ROWS, (c + 1) * NORM_ROWS)
        dst_ref[rs, :] = (_rms_scale(src_ref[rs, :]) * gain).astype(dst_ref.dtype)


def _residual_norm_rows(res_ref, gain_ref, acc_ref, dst_ref, rows, weight):
    gain = weight * gain_ref[...]
    for c in range(rows // NORM_ROWS):
        rs = slice(c * NORM_ROWS, (c + 1) * NORM_ROWS)
        dst_ref[rs, :] = res_ref[rs, :] + _rms_scale(acc_ref[rs, :]) * gain


class CastJob(NamedTuple):
    src: jax.Array
    in_spec: pl.BlockSpec
    out_specs: tuple
    out_shapes: tuple
    body: Callable


def _cast_block(src_ref, dst_ref):
    dst_ref[...] = src_ref[...].astype(BF16)


def _tile_rows_job(src, n_tiles, n_steps, body=_cast_block, out_cols=None):
    rows, cols = src.shape
    per_tile = rows // (n_tiles * CAST_ROWS)
    assert per_tile * n_tiles * CAST_ROWS == rows and per_tile <= n_steps
    index_map = lambda i, j: (i * per_tile + jnp.minimum(j, per_tile - 1), 0)
    out_cols = (cols,) if out_cols is None else out_cols
    return CastJob(src, pl.BlockSpec((CAST_ROWS, cols), index_map),
                   tuple(pl.BlockSpec((CAST_ROWS, c), index_map) for c in out_cols),
                   tuple(jax.ShapeDtypeStruct((rows, c), BF16) for c in out_cols), body)


def _ffn_kernel(*refs, jobs):
    x_ref, pre_g_ref, wg_ref, wu_ref, wo_ref, post_g_ref = refs[:6]
    job_srcs = refs[6:6 + len(jobs)]
    o_ref = refs[6 + len(jobs)]
    job_dsts = refs[7 + len(jobs):-1]
    n_ref = refs[-1]
    j = pl.program_id(1)
    last = pl.num_programs(1) - 1
    row_chunks = [slice(r * FFN_ROWS, (r + 1) * FFN_ROWS) for r in range(FFN_TM // FFN_ROWS)]

    def swiglu(n):
        g = jnp.dot(n, wg_ref[...], preferred_element_type=F32)
        u = jnp.dot(n, wu_ref[...], preferred_element_type=F32)
        a = (g * jax.nn.sigmoid(g) * u).astype(BF16)
        return jnp.dot(a, wo_ref[...], preferred_element_type=F32)

    def run_jobs():
        first = 0
        for job, src_ref in zip(jobs, job_srcs):
            job.body(src_ref, *job_dsts[first:first + len(job.out_specs)])
            first += len(job.out_specs)

    @pl.when(j == 0)
    def _():
        gain = pre_g_ref[...]
        for rows in row_chunks:
            for c in range(rows.start, rows.stop, NORM_ROWS):
                rs = slice(c, c + NORM_ROWS)
                n_ref[rs, :] = (_rms_scale(x_ref[rs, :]) * gain).astype(BF16)
            o_ref[rows, :] = swiglu(n_ref[rows, :])
        run_jobs()

    @pl.when((j > 0) & (j < last))
    def _():
        o_ref[...] += swiglu(n_ref[...])
        run_jobs()

    @pl.when(j == last)
    def _():
        gain = MACARON_W * post_g_ref[...]
        for rows in row_chunks:
            f = o_ref[rows, :] + swiglu(n_ref[rows, :])
            for c in range(0, FFN_ROWS, NORM_ROWS):
                rs = slice(rows.start + c, rows.start + c + NORM_ROWS)
                o_ref[rs, :] = x_ref[rs, :] + _rms_scale(f[c:c + NORM_ROWS, :]) * gain
        run_jobs()


def _ffn(x, pre_g, w_in, w_out, post_g, jobs=()):
    t = x.shape[0]
    nf = D_FF // FFN_FC
    assert nf >= 2
    out_specs = [pl.BlockSpec((FFN_TM, D_MODEL), lambda i, j: (i, 0))]
    out_shapes = [jax.ShapeDtypeStruct((t, D_MODEL), F32)]
    for job in jobs:
        out_specs.extend(job.out_specs)
        out_shapes.extend(job.out_shapes)
    return pl.pallas_call(
        functools.partial(_ffn_kernel, jobs=jobs),
        grid=(t // FFN_TM, nf),
        in_specs=[
            pl.BlockSpec((FFN_TM, D_MODEL), lambda i, j: (i, 0)),
            pl.BlockSpec((1, D_MODEL), lambda i, j: (0, 0)),
            pl.BlockSpec((D_MODEL, FFN_FC), lambda i, j: (0, j)),
            pl.BlockSpec((D_MODEL, FFN_FC), lambda i, j: (0, j + nf)),
            pl.BlockSpec((FFN_FC, D_MODEL), lambda i, j: (j, 0)),
            pl.BlockSpec((1, D_MODEL), lambda i, j: (0, 0)),
        ] + [job.in_spec for job in jobs],
        out_specs=out_specs,
        out_shape=out_shapes,
        scratch_shapes=[pltpu.VMEM((FFN_TM, D_MODEL), BF16)],
        compiler_params=_compiler_params(2),
        name="ffn",
    )(x, pre_g, w_in, w_in, w_out, post_g, *[job.src for job in jobs])


def _qkv_kernel(h_ref, g_ref, w_ref, u_ref, o_ref):
    @pl.when(pl.program_id(1) == 0)
    def _():
        _norm_rows_to(h_ref, g_ref, u_ref, QKV_TM)

    o_ref[...] = jnp.dot(u_ref[...], w_ref[...], preferred_element_type=F32).astype(BF16)


def _qkv_proj(h, gain, w_qkv):
    t = h.shape[0]
    return pl.pallas_call(
        _qkv_kernel,
        grid=(t // QKV_TM, QKV_W // QKV_TN),
        in_specs=[
            pl.BlockSpec((QKV_TM, D_MODEL), lambda i, j: (i, 0)),
            pl.BlockSpec((1, D_MODEL), lambda i, j: (0, 0)),
            pl.BlockSpec((D_MODEL, QKV_TN), lambda i, j: (0, j)),
        ],
        out_specs=[
            pl.BlockSpec((QKV_TM, D_MODEL), lambda i, j: (i, 0)),
            pl.BlockSpec((QKV_TM, QKV_TN), lambda i, j: (i, j)),
        ],
        out_shape=[jax.ShapeDtypeStruct((t, D_MODEL), BF16),
                   jax.ShapeDtypeStruct((t, QKV_W), BF16)],
        compiler_params=_compiler_params(2),
        name="qkv_proj",
    )(h, gain, w_qkv)


def _build_win_bias(tbl_ref):
    span = 3 * A_BLOCK
    qi = lax.broadcasted_iota(jnp.int32, (A_BLOCK, span), 0)
    kj = lax.broadcasted_iota(jnp.int32, (A_BLOCK, span), 1)
    absd_i = jnp.abs(qi + A_WINDOW - kj)
    absd = absd_i.astype(F32)
    in_window = absd_i <= A_WINDOW
    valid = (in_window & (kj >= A_BLOCK), in_window, in_window & (kj < 2 * A_BLOCK))
    for h, slope in enumerate(_alibi_slopes(A_Q_HEADS)):
        bias = (-slope * LOG2E) * absd
        for variant in range(3):
            tbl_ref[variant, h] = jnp.where(valid[variant], bias, NEG_INF)


def _attn_win_kernel(sink_ref, q_ref, kp_ref, kc_ref, kn_ref, vp_ref, vc_ref, vn_ref, u_ref, wg_ref, wup_ref,
                     bg_ref, o_ref, k_scr, v_scr, tbl_ref, a_scr):
    m = pl.program_id(1)

    @pl.when((pl.program_id(0) == 0) & (m == 0))
    def _():
        _build_win_bias(tbl_ref)

    k_scr[0:A_BLOCK, :] = kp_ref[...]
    k_scr[A_BLOCK:A_BLOCK + A_STEP, :] = kc_ref[...]
    k_scr[A_BLOCK + A_STEP:, :] = kn_ref[...]
    v_scr[0:A_BLOCK, :] = vp_ref[...]
    v_scr[A_BLOCK:A_BLOCK + A_STEP, :] = vc_ref[...]
    v_scr[A_BLOCK + A_STEP:, :] = vn_ref[...]
    span = 3 * A_BLOCK
    last_block = SEQ // A_BLOCK - 1
    piece = D_MODEL // (A_QB * A_KV_HEADS)

    def gate_piece(i):
        cols = slice(i * piece, (i + 1) * piece)
        gate = jnp.dot(u_ref[...], wg_ref[:, cols], preferred_element_type=F32) + bg_ref[:, cols]
        o_ref[:, cols] = jax.nn.sigmoid(gate)

    def sub_block(t):
        r = t * A_BLOCK
        block = m * A_QB + t
        variant = jnp.where(block == 0, 0, jnp.where(block == last_block, 2, 1))
        for g in range(A_KV_HEADS):
            k = k_scr[pl.ds(r, span), g * HEAD_DIM:(g + 1) * HEAD_DIM]
            v = v_scr[pl.ds(r, span), g * HEAD_DIM:(g + 1) * HEAD_DIM]
            heads = [g * A_GROUP + e for e in range(A_GROUP)]
            qs = jnp.concatenate(
                [q_ref[pl.ds(r, A_BLOCK), h * HEAD_DIM:(h + 1) * HEAD_DIM] for h in heads], axis=0)
            s = lax.dot_general(qs, k, (((1,), (1,)), ((), ())), preferred_element_type=F32)
            gate_piece(t * A_KV_HEADS + g)
            probs, dens = [], []
            for e, h in enumerate(heads):
                logits = s[e * A_BLOCK:(e + 1) * A_BLOCK, :] + tbl_ref[variant, h]
                sink = sink_ref[h] * LOG2E
                mx = jnp.maximum(jnp.max(logits, axis=-1, keepdims=True), sink)
                p = jnp.exp2(logits - mx)
                dens.append(jnp.sum(p, axis=-1, keepdims=True) + jnp.exp2(sink - mx))
                probs.append(p.astype(BF16))
            o = jnp.dot(jnp.concatenate(probs, axis=0), v, preferred_element_type=F32)
            for e, h in enumerate(heads):
                a_scr[pl.ds(r, A_BLOCK), h * HEAD_DIM:(h + 1) * HEAD_DIM] = (
                    o[e * A_BLOCK:(e + 1) * A_BLOCK, :] / dens[e]).astype(BF16)

    for t in range(A_QB):
        sub_block(t)
    attn = a_scr[...]
    for c in range(D_MODEL // GATE_TC):
        cols = slice(c * GATE_TC, (c + 1) * GATE_TC)
        o_ref[:, cols] = o_ref[:, cols] * jnp.dot(attn, wup_ref[:, cols], preferred_element_type=F32)


def _attn_win(qkv, sink, col_q, col_k, col_v, u, w_gate, w_up, b_gate):
    b = qkv.shape[0]
    n_steps = SEQ // A_STEP
    n_blocks = SEQ // A_BLOCK
    kcol, vcol = col_k // A_KV_W, col_v // A_KV_W

    def edge(col, shift):
        def index_map(bi, m):
            blk = jnp.clip(m * A_QB + shift, 0, n_blocks - 1)
            return (bi, blk, col)
        return pl.BlockSpec((None, A_BLOCK, A_KV_W), index_map)

    def centre(col):
        return pl.BlockSpec((None, A_STEP, A_KV_W), lambda bi, m: (bi, m, col))

    return pl.pallas_call(
        _attn_win_kernel,
        grid=(b, n_steps),
        in_specs=[
            pl.BlockSpec(memory_space=pltpu.SMEM),
            pl.BlockSpec((None, A_STEP, A_Q_W), lambda bi, m: (bi, m, col_q // A_Q_W)),
            edge(kcol, -1), centre(kcol), edge(kcol, A_QB),
            edge(vcol, -1), centre(vcol), edge(vcol, A_QB),
            pl.BlockSpec((None, A_STEP, D_MODEL), lambda bi, m: (bi, m, 0)),
            pl.BlockSpec((D_MODEL, D_MODEL), lambda bi, m: (0, 0)),
            pl.BlockSpec((A_Q_W, D_MODEL), lambda bi, m: (0, 0)),
            pl.BlockSpec((1, D_MODEL), lambda bi, m: (0, 0)),
        ],
        out_specs=pl.BlockSpec((None, A_STEP, D_MODEL), lambda bi, m: (bi, m, 0)),
        out_shape=jax.ShapeDtypeStruct((b, SEQ, D_MODEL), F32),
        scratch_shapes=[pltpu.VMEM((A_STEP + 2 * A_BLOCK, A_KV_W), BF16),
                        pltpu.VMEM((A_STEP + 2 * A_BLOCK, A_KV_W), BF16),
                        pltpu.VMEM((3, A_Q_HEADS, A_BLOCK, 3 * A_BLOCK), F32),
                        pltpu.VMEM((A_STEP, A_Q_W), BF16)],
        compiler_params=_compiler_params(2),
        name="attn_win",
    )(sink, qkv, qkv, qkv, qkv, qkv, qkv, qkv, u, w_gate, w_up, b_gate)


def _nbr_row_start(row):
    return min(max(row - B_WIN_R // 2, 0), GRID_ROWS - B_WIN_R)


def _nbr_key_row0(pair):
    return min(_nbr_row_start(pair * B_PAIR_ROWS), GRID_ROWS - B_KEY_ROWS)


def _build_nbr_bias(rpb_ref, tbl_ref, toep_l, toep_r):
    lanes = 2 * GRID_W
    qc = lax.broadcasted_iota(jnp.int32, (GRID_W, lanes), 0)
    lane = lax.broadcasted_iota(jnp.int32, (GRID_W, lanes), 1)
    col_start = jnp.clip(qc - B_WIN_C // 2, 0, GRID_W - B_WIN_C)
    left = lane < GRID_W
    neg = jnp.full((GRID_W, lanes), NEG_INF, F32)
    n_rel_r, n_rel_c = 2 * B_WIN_R - 1, 2 * B_WIN_C - 1

    def per_head(h, carry):
        for half, dst in ((0, toep_l), (1, toep_r)):
            kc = lane - half * GRID_W
            rel = kc - qc + (B_WIN_C - 1)
            col_ok = (kc >= col_start) & (kc < col_start + B_WIN_C) & (left if half == 0 else ~left)
            for a in range(n_rel_r):
                t = jnp.zeros((GRID_W, lanes), F32)
                for c in range(n_rel_c):
                    t = jnp.where(rel == c, rpb_ref[h, a * n_rel_c + c] * LOG2E, t)
                dst[a] = jnp.where(col_ok, t, neg)
        for vi, pair in enumerate(B_VARIANT_PAIRS):
            row0 = _nbr_key_row0(pair)
            for ql in range(B_PAIR_ROWS):
                q_row = pair * B_PAIR_ROWS + ql
                start = _nbr_row_start(q_row)
                for mt in range(B_KEY_ROWS // 2):
                    halves = []
                    for half, src in ((0, toep_l), (1, toep_r)):
                        k_row = row0 + 2 * mt + half
                        if start <= k_row < start + B_WIN_R:
                            halves.append(src[k_row - q_row + B_WIN_R - 1])
                        else:
                            halves.append(neg)
                    tbl_ref[vi, h, ql * GRID_W:(ql + 1) * GRID_W, mt * lanes:(mt + 1) * lanes] = (
                        jnp.where(left, halves[0], halves[1]))
        return carry

    lax.fori_loop(0, B_HEADS, per_head, 0)


def _nbr_window_block(step):
    return jnp.clip(step * (B_STEP // B_WINDOW_ALIGN) - 1, 0, (SEQ - B_WINDOW_TOK) // B_WINDOW_ALIGN)


def _attn_nbr_kernel(rpb_ref, q_ref, k_ref, v_ref, u_ref, wg_ref, wup_ref, bg_ref, sa_ref, o_ref,
                     tbl_ref, toep_l, toep_r, attn_scr, sig_scr):
    step = pl.program_id(1)

    @pl.when((pl.program_id(0) == 0) & (step == 0))
    def _():
        _build_nbr_bias(rpb_ref, tbl_ref, toep_l, toep_r)

    window_row0 = _nbr_window_block(step) * (B_WINDOW_ALIGN // GRID_W)
    piece = D_MODEL // B_PAIRS_PER_STEP
    head_cols = [slice(h * HEAD_DIM, (h + 1) * HEAD_DIM) for h in range(B_HEADS)]

    def gate_piece(i):
        cols = slice(i * piece, (i + 1) * piece)
        gate = jnp.dot(u_ref[...], wg_ref[:, cols], preferred_element_type=F32) + bg_ref[:, cols]
        sig_scr[:, cols] = jax.nn.sigmoid(gate)

    for pp in range(B_PAIRS_PER_STEP):
        pair = step * B_PAIRS_PER_STEP + pp
        row0 = jnp.clip(pair * B_PAIR_ROWS - B_WIN_R // 2, 0, GRID_ROWS - B_KEY_ROWS)
        variant = jnp.where(pair < 2, pair, jnp.where(pair >= B_N_PAIRS - 2, pair - (B_N_PAIRS - 5), 2))
        k0 = pl.multiple_of((row0 - window_row0) * GRID_W, GRID_W)
        q_rows = slice(pp * B_PAIR_TOK, (pp + 1) * B_PAIR_TOK)
        scores = [
            lax.dot_general(q_ref[q_rows, cols], k_ref[0, pl.ds(k0, B_KEY_TOK), cols],
                            (((1,), (1,)), ((), ())), preferred_element_type=F32)
            for cols in head_cols]
        gate_piece(pp)
        s = jnp.concatenate(scores, axis=0)
        logits = s + tbl_ref[variant].reshape(B_HEADS * B_PAIR_TOK, B_KEY_TOK)
        mx = jnp.max(logits, axis=-1, keepdims=True)
        p = jnp.exp2(logits - mx)
        inv_den = 1.0 / jnp.sum(p, axis=-1, keepdims=True)
        p = p.astype(BF16)
        for h, cols in enumerate(head_cols):
            rows = slice(h * B_PAIR_TOK, (h + 1) * B_PAIR_TOK)
            o = jnp.dot(p[rows, :], v_ref[0, pl.ds(k0, B_KEY_TOK), cols], preferred_element_type=F32)
            attn_scr[q_rows, cols] = (o * inv_den[rows, :]).astype(BF16)

    attn = attn_scr[...]
    for c in range(D_MODEL // GATE_TC):
        cols = slice(c * GATE_TC, (c + 1) * GATE_TC)
        up = jnp.dot(attn, wup_ref[:, cols], preferred_element_type=F32)
        o_ref[:, cols] = (sa_ref[:, cols] + sig_scr[:, cols] * up).astype(BF16)


def _attn_nbr(qkv, rpb, col_q, col_k, col_v, u, w_gate, w_up, b_gate, sa):
    b = qkv.shape[0]
    assert B_WINDOW_ALIGN == (B_WIN_R // 2) * GRID_W and B_STEP % B_WINDOW_ALIGN == 0

    def window(col):
        return pl.BlockSpec((pl.Element(1), pl.Element(B_WINDOW_TOK), pl.Element(B_W)),
                            lambda bi, s: (bi, _nbr_window_block(s) * B_WINDOW_ALIGN, col))

    resident = dict(pipeline_mode=pl.Buffered(1))
    return pl.pallas_call(
        _attn_nbr_kernel,
        grid=(b, SEQ // B_STEP),
        in_specs=[
            pl.BlockSpec(memory_space=pltpu.SMEM),
            pl.BlockSpec((None, B_STEP, B_W), lambda bi, s: (bi, s, col_q // B_W)),
            window(col_k), window(col_v),
            pl.BlockSpec((None, B_STEP, D_MODEL), lambda bi, s: (bi, s, 0)),
            pl.BlockSpec((D_MODEL, D_MODEL), lambda bi, s: (0, 1), **resident),
            pl.BlockSpec((B_W, D_MODEL), lambda bi, s: (0, 0), **resident),
            pl.BlockSpec((1, D_MODEL), lambda bi, s: (0, 1)),
            pl.BlockSpec((None, B_STEP, D_MODEL), lambda bi, s: (bi, s, 0)),
        ],
        out_specs=pl.BlockSpec((None, B_STEP, D_MODEL), lambda bi, s: (bi, s, 0)),
        out_shape=jax.ShapeDtypeStruct((b, SEQ, D_MODEL), BF16),
        scratch_shapes=[
            pltpu.VMEM((len(B_VARIANT_PAIRS), B_HEADS, B_PAIR_TOK, B_KEY_TOK), F32),
            pltpu.VMEM((2 * B_WIN_R - 1, GRID_W, 2 * GRID_W), F32),
            pltpu.VMEM((2 * B_WIN_R - 1, GRID_W, 2 * GRID_W), F32),
            pltpu.VMEM((B_STEP, B_W), BF16),
            pltpu.VMEM((B_STEP, D_MODEL), F32),
        ],
        compiler_params=_compiler_params(2),
        name="attn_nbr",
    )(rpb, qkv, qkv, qkv, u, w_gate, w_up, b_gate, sa)


def _out_proj_kernel(m_ref, h_ref, w_ref, post_g_ref, o_ref):
    gain = post_g_ref[...]
    for r in range(OUT_TM // OUT_ROWS):
        y = jnp.dot(m_ref[r * OUT_ROWS:(r + 1) * OUT_ROWS, :], w_ref[...], preferred_element_type=F32)
        for c in range(OUT_ROWS // NORM_ROWS):
            rs = slice(r * OUT_ROWS + c * NORM_ROWS, r * OUT_ROWS + (c + 1) * NORM_ROWS)
            o_ref[rs, :] = h_ref[rs, :] + _rms_scale(y[c * NORM_ROWS:(c + 1) * NORM_ROWS, :]) * gain


def _out_proj(m, h, w_out, post_g):
    t = m.shape[0]
    return pl.pallas_call(
        _out_proj_kernel,
        grid=(t // OUT_TM,),
        in_specs=[
            pl.BlockSpec((OUT_TM, D_MODEL), lambda i: (i, 0)),
            pl.BlockSpec((OUT_TM, D_MODEL), lambda i: (i, 0)),
            pl.BlockSpec((D_MODEL, D_MODEL), lambda i: (0, 0), pipeline_mode=pl.Buffered(1)),
            pl.BlockSpec((1, D_MODEL), lambda i: (0, 0)),
        ],
        out_specs=pl.BlockSpec((OUT_TM, D_MODEL), lambda i: (i, 0)),
        out_shape=jax.ShapeDtypeStruct((t, D_MODEL), F32),
        compiler_params=_compiler_params(1),
        name="out_proj",
    )(m, h, w_out, post_g)


def _cast_mixer_w_in(src_ref, qkv_ref, gate_ref):
    for name in QKV_ORDER:
        lo, hi = QKV_SRC[name]
        blk = src_ref[:, lo:hi]
        if name in ("qa", "qb"):
            blk = blk * Q_PRESCALE
        qkv_ref[:, QKV_COL[name]:QKV_COL[name] + hi - lo] = blk.astype(BF16)
    gate_ref[...] = src_ref[:, QKV_W:].astype(BF16)


def kernel(x, ffn1_pre_g, ffn1_w_in, ffn1_w_out, ffn1_post_g, mix_pre_g, w_in, b_gate, sink_a, rpb_b,
           w_up_a, w_up_b, w_out, mix_post_g, ffn2_pre_g, ffn2_w_in, ffn2_w_out, ffn2_post_g):
    batch, seq, d = x.shape
    assert (seq, d) == (SEQ, D_MODEL)
    depth = ffn1_w_in.shape[0]
    tokens = batch * seq
    n_tiles, nf = tokens // FFN_TM, D_FF // FFN_FC
    h = x.reshape(tokens, d)

    def row(v):
        return v.reshape(1, -1).astype(F32)

    def grid_job(src, block, index_map):
        spec = pl.BlockSpec(block, index_map)
        return CastJob(src, spec, (spec,), (jax.ShapeDtypeStruct(src.shape, BF16),), _cast_block)

    for l in range(depth):
        jobs = (
            grid_job(ffn2_w_in[l], (D_MODEL // n_tiles, 2 * D_FF // nf), lambda i, j: (i, j)),
            grid_job(ffn2_w_out[l], (D_FF // nf, D_MODEL // n_tiles), lambda i, j: (j, i)),
            _tile_rows_job(w_in[l], n_tiles, nf, _cast_mixer_w_in, (QKV_W, 2 * D_MODEL)),
            _tile_rows_job(w_out[l], n_tiles, nf),
            _tile_rows_job(w_up_a[l], n_tiles, nf),
            _tile_rows_job(w_up_b[l], n_tiles, nf),
        )
        h, w2_in, w2_out, w_qkv, w_gate, w_o, w_ua, w_ub = _ffn(
            h, row(ffn1_pre_g[l]), ffn1_w_in[l].astype(BF16), ffn1_w_out[l].astype(BF16),
            row(ffn1_post_g[l]), jobs)
        u, qkv = _qkv_proj(h, row(mix_pre_g[l]), w_qkv)
        qkv = qkv.reshape(batch, seq, QKV_W)
        u = u.reshape(batch, seq, d)
        sa = _attn_win(qkv, sink_a[l].astype(F32), QKV_COL["qa"], QKV_COL["ka"], QKV_COL["va"],
                       u, w_gate, w_ua, row(b_gate[l]))
        mixed = _attn_nbr(qkv, rpb_b[l].astype(F32).reshape(B_HEADS, -1),
                          QKV_COL["qb"], QKV_COL["kb"], QKV_COL["vb"],
                          u, w_gate, w_ub, row(b_gate[l]), sa)
        h = _out_proj(mixed.reshape(tokens, d), h, w_o, row(mix_post_g[l]))
        h, = _ffn(h, row(ffn2_pre_g[l]), w2_in, w2_out, row(ffn2_post_g[l]))
    return h.reshape(batch, seq, d)
```

```python
import functools
import math
from typing import Callable, NamedTuple

import jax
import jax.numpy as jnp
import numpy as np
from jax import lax
from jax.experimental import pallas as pl
from jax.experimental.pallas import tpu as pltpu

D_MODEL = 2048
SEQ = 4096
HEAD_DIM = 128
A_Q_HEADS = 8
A_KV_HEADS = 2
A_GROUP = A_Q_HEADS // A_KV_HEADS
A_WINDOW = 128
A_BLOCK = 128
B_HEADS = 8
GRID_W = 64
GRID_ROWS = SEQ // GRID_W
B_WIN_R = 8
B_WIN_C = 16
D_FF = 5632
MACARON_W = 0.5
RMS_EPS = 1e-6
NEG_INF = -1e30
SCALE = HEAD_DIM ** -0.5
LOG2E = math.log2(math.e)
Q_PRESCALE = SCALE * LOG2E

A_Q_W = A_Q_HEADS * HEAD_DIM
A_KV_W = A_KV_HEADS * HEAD_DIM
B_W = B_HEADS * HEAD_DIM
QKV_W = A_Q_W + 2 * A_KV_W + 3 * B_W

_cuts = np.cumsum([0, A_Q_W, A_KV_W, A_KV_W, B_W, B_W, B_W]).tolist()
QKV_SRC = {name: (_cuts[i], _cuts[i + 1]) for i, name in enumerate(("qa", "ka", "va", "qb", "kb", "vb"))}
QKV_ORDER = ("qa", "qb", "kb", "vb", "ka", "va")
QKV_COL = {}
for _name in QKV_ORDER:
    QKV_COL[_name] = sum(QKV_SRC[n][1] - QKV_SRC[n][0] for n in QKV_ORDER[:QKV_ORDER.index(_name)])

F32 = jnp.float32
BF16 = jnp.bfloat16

VMEM_LIMIT_BYTES = 61 * 1024 * 1024

FFN_TM = 1024
FFN_FC = 512
FFN_ROWS = 256
QKV_TM = 512
QKV_ROWS = 256
OUT_TM = 1024
OUT_ROWS = 256
NORM_ROWS = 32
CAST_ROWS = 16

GATE_TC = 512
A_QB = 4
A_STEP = A_QB * A_BLOCK
B_PAIR_ROWS = 2
B_PAIR_TOK = B_PAIR_ROWS * GRID_W
B_KEY_ROWS = 10
B_KEY_TOK = B_KEY_ROWS * GRID_W
B_PAIRS_PER_STEP = 2
B_STEP = B_PAIRS_PER_STEP * B_PAIR_TOK
B_N_PAIRS = GRID_ROWS // B_PAIR_ROWS
B_WINDOW_ALIGN = 256
B_WINDOW_TOK = B_STEP + 2 * B_WINDOW_ALIGN
B_VARIANT_PAIRS = (0, 1, 2, B_N_PAIRS - 2, B_N_PAIRS - 1)


def _alibi_slopes(n_heads):
    return [2.0 ** (-8.0 * (i + 1) / n_heads) for i in range(n_heads)]


def _compiler_params(n_axes):
    return pltpu.CompilerParams(dimension_semantics=("arbitrary",) * n_axes,
                                vmem_limit_bytes=VMEM_LIMIT_BYTES)


def _rms_scale(x):
    return x * lax.rsqrt(jnp.mean(x * x, axis=-1, keepdims=True) + RMS_EPS)


class CastJob(NamedTuple):
    src: jax.Array
    in_spec: pl.BlockSpec
    out_specs: tuple
    out_shapes: tuple
    body: Callable


def _cast_block(src_ref, dst_ref):
    dst_ref[...] = src_ref[...].astype(BF16)


def _tile_rows_job(src, n_tiles, n_steps, body=_cast_block, out_cols=None):
    rows, cols = src.shape
    per_tile = rows // (n_tiles * CAST_ROWS)
    assert per_tile * n_tiles * CAST_ROWS == rows and per_tile <= n_steps
    index_map = lambda i, j: (i * per_tile + jnp.minimum(j, per_tile - 1), 0)
    out_cols = (cols,) if out_cols is None else out_cols
    return CastJob(src, pl.BlockSpec((CAST_ROWS, cols), index_map),
                   tuple(pl.BlockSpec((CAST_ROWS, c), index_map) for c in out_cols),
                   tuple(jax.ShapeDtypeStruct((rows, c), BF16) for c in out_cols), body)


def _ffn_kernel(*refs, jobs):
    x_ref, pre_g_ref, wg_ref, wu_ref, wo_ref, post_g_ref = refs[:6]
    job_srcs = refs[6:6 + len(jobs)]
    o_ref = refs[6 + len(jobs)]
    job_dsts = refs[7 + len(jobs):-1]
    n_ref = refs[-1]
    j = pl.program_id(1)
    last = pl.num_programs(1) - 1
    row_chunks = [slice(r * FFN_ROWS, (r + 1) * FFN_ROWS) for r in range(FFN_TM // FFN_ROWS)]

    def swiglu(n):
        g = jnp.dot(n, wg_ref[...], preferred_element_type=F32)
        u = jnp.dot(n, wu_ref[...], preferred_element_type=F32)
        a = (g * jax.nn.sigmoid(g) * u).astype(BF16)
        return jnp.dot(a, wo_ref[...], preferred_element_type=F32)

    def run_jobs():
        first = 0
        for job, src_ref in zip(jobs, job_srcs):
            job.body(src_ref, *job_dsts[first:first + len(job.out_specs)])
            first += len(job.out_specs)

    @pl.when(j == 0)
    def _():
        gain = pre_g_ref[...]
        for rows in row_chunks:
            for c in range(rows.start, rows.stop, NORM_ROWS):
                rs = slice(c, c + NORM_ROWS)
                n_ref[rs, :] = (_rms_scale(x_ref[rs, :]) * gain).astype(BF16)
            o_ref[rows, :] = swiglu(n_ref[rows, :])
        run_jobs()

    @pl.when((j > 0) & (j < last))
    def _():
        o_ref[...] += swiglu(n_ref[...])
        run_jobs()

    @pl.when(j == last)
    def _():
        gain = MACARON_W * post_g_ref[...]
        for rows in row_chunks:
            f = o_ref[rows, :] + swiglu(n_ref[rows, :])
            for c in range(0, FFN_ROWS, NORM_ROWS):
                rs = slice(rows.start + c, rows.start + c + NORM_ROWS)
                o_ref[rs, :] = x_ref[rs, :] + _rms_scale(f[c:c + NORM_ROWS, :]) * gain
        run_jobs()


def _ffn(x, pre_g, w_in, w_out, post_g, jobs=()):
    t = x.shape[0]
    nf = D_FF // FFN_FC
    assert nf >= 2
    out_specs = [pl.BlockSpec((FFN_TM, D_MODEL), lambda i, j: (i, 0))]
    out_shapes = [jax.ShapeDtypeStruct((t, D_MODEL), F32)]
    for job in jobs:
        out_specs.extend(job.out_specs)
        out_shapes.extend(job.out_shapes)
    return pl.pallas_call(
        functools.partial(_ffn_kernel, jobs=jobs),
        grid=(t // FFN_TM, nf),
        in_specs=[
            pl.BlockSpec((FFN_TM, D_MODEL), lambda i, j: (i, 0)),
            pl.BlockSpec((1, D_MODEL), lambda i, j: (0, 0)),
            pl.BlockSpec((D_MODEL, FFN_FC), lambda i, j: (0, j)),
            pl.BlockSpec((D_MODEL, FFN_FC), lambda i, j: (0, j + nf)),
            pl.BlockSpec((FFN_FC, D_MODEL), lambda i, j: (j, 0)),
            pl.BlockSpec((1, D_MODEL), lambda i, j: (0, 0)),
        ] + [job.in_spec for job in jobs],
        out_specs=out_specs,
        out_shape=out_shapes,
        scratch_shapes=[pltpu.VMEM((FFN_TM, D_MODEL), BF16)],
        compiler_params=_compiler_params(2),
        name="ffn",
    )(x, pre_g, w_in, w_in, w_out, post_g, *[job.src for job in jobs])


def _qkv_kernel(h_ref, g_ref, w_ref, u_ref, o_ref):
    gain = g_ref[...]
    for r in range(0, QKV_TM, QKV_ROWS):
        for c in range(r, r + QKV_ROWS, NORM_ROWS):
            rs = slice(c, c + NORM_ROWS)
            u_ref[rs, :] = (_rms_scale(h_ref[rs, :]) * gain).astype(BF16)
        rows = slice(r, r + QKV_ROWS)
        o_ref[rows, :] = jnp.dot(u_ref[rows, :], w_ref[...], preferred_element_type=F32).astype(BF16)


def _qkv_proj(h, gain, w_qkv):
    t = h.shape[0]
    return pl.pallas_call(
        _qkv_kernel,
        grid=(t // QKV_TM,),
        in_specs=[
            pl.BlockSpec((QKV_TM, D_MODEL), lambda i: (i, 0)),
            pl.BlockSpec((1, D_MODEL), lambda i: (0, 0)),
            pl.BlockSpec((D_MODEL, QKV_W), lambda i: (0, 0), pipeline_mode=pl.Buffered(1)),
        ],
        out_specs=[
            pl.BlockSpec((QKV_TM, D_MODEL), lambda i: (i, 0)),
            pl.BlockSpec((QKV_TM, QKV_W), lambda i: (i, 0)),
        ],
        out_shape=[jax.ShapeDtypeStruct((t, D_MODEL), BF16),
                   jax.ShapeDtypeStruct((t, QKV_W), BF16)],
        compiler_params=_compiler_params(1),
        name="qkv_proj",
    )(h, gain, w_qkv)


def _build_win_bias(tbl_ref):
    span = 3 * A_BLOCK
    qi = lax.broadcasted_iota(jnp.int32, (A_BLOCK, span), 0)
    kj = lax.broadcasted_iota(jnp.int32, (A_BLOCK, span), 1)
    absd_i = jnp.abs(qi + A_WINDOW - kj)
    absd = absd_i.astype(F32)
    in_window = absd_i <= A_WINDOW
    valid = (in_window & (kj >= A_BLOCK), in_window, in_window & (kj < 2 * A_BLOCK))
    for h, slope in enumerate(_alibi_slopes(A_Q_HEADS)):
        bias = (-slope * LOG2E) * absd
        for variant in range(3):
            tbl_ref[variant, h] = jnp.where(valid[variant], bias, NEG_INF)


def _attn_win_kernel(sink_ref, q_ref, kp_ref, kc_ref, kn_ref, vp_ref, vc_ref, vn_ref, u_ref, wg_ref, wup_ref,
                     bg_ref, o_ref, k_scr, v_scr, tbl_ref, a_scr):
    m = pl.program_id(1)

    @pl.when((pl.program_id(0) == 0) & (m == 0))
    def _():
        _build_win_bias(tbl_ref)

    k_scr[0:A_BLOCK, :] = kp_ref[...]
    k_scr[A_BLOCK:A_BLOCK + A_STEP, :] = kc_ref[...]
    k_scr[A_BLOCK + A_STEP:, :] = kn_ref[...]
    v_scr[0:A_BLOCK, :] = vp_ref[...]
    v_scr[A_BLOCK:A_BLOCK + A_STEP, :] = vc_ref[...]
    v_scr[A_BLOCK + A_STEP:, :] = vn_ref[...]
    span = 3 * A_BLOCK
    last_block = SEQ // A_BLOCK - 1
    piece = D_MODEL // (A_QB * A_KV_HEADS)

    def gate_piece(i):
        cols = slice(i * piece, (i + 1) * piece)
        gate = jnp.dot(u_ref[...], wg_ref[:, cols], preferred_element_type=F32) + bg_ref[:, cols]
        o_ref[:, cols] = jax.nn.sigmoid(gate)

    def sub_block(t):
        r = t * A_BLOCK
        block = m * A_QB + t
        variant = jnp.where(block == 0, 0, jnp.where(block == last_block, 2, 1))
        for g in range(A_KV_HEADS):
            k = k_scr[pl.ds(r, span), g * HEAD_DIM:(g + 1) * HEAD_DIM]
            v = v_scr[pl.ds(r, span), g * HEAD_DIM:(g + 1) * HEAD_DIM]
            heads = [g * A_GROUP + e for e in range(A_GROUP)]
            qs = jnp.concatenate(
                [q_ref[pl.ds(r, A_BLOCK), h * HEAD_DIM:(h + 1) * HEAD_DIM] for h in heads], axis=0)
            s = lax.dot_general(qs, k, (((1,), (1,)), ((), ())), preferred_element_type=F32)
            gate_piece(t * A_KV_HEADS + g)
            probs, dens = [], []
            for e, h in enumerate(heads):
                logits = s[e * A_BLOCK:(e + 1) * A_BLOCK, :] + tbl_ref[variant, h]
                sink = sink_ref[h] * LOG2E
                mx = jnp.maximum(jnp.max(logits, axis=-1, keepdims=True), sink)
                p = jnp.exp2(logits - mx)
                dens.append(jnp.sum(p, axis=-1, keepdims=True) + jnp.exp2(sink - mx))
                probs.append(p.astype(BF16))
            o = jnp.dot(jnp.concatenate(probs, axis=0), v, preferred_element_type=F32)
            for e, h in enumerate(heads):
                a_scr[pl.ds(r, A_BLOCK), h * HEAD_DIM:(h + 1) * HEAD_DIM] = (
                    o[e * A_BLOCK:(e + 1) * A_BLOCK, :] / dens[e]).astype(BF16)

    for t in range(A_QB):
        sub_block(t)
    attn = a_scr[...]
    for c in range(D_MODEL // GATE_TC):
        cols = slice(c * GATE_TC, (c + 1) * GATE_TC)
        o_ref[:, cols] = o_ref[:, cols] * jnp.dot(attn, wup_ref[:, cols], preferred_element_type=F32)


def _attn_win(qkv, sink, col_q, col_k, col_v, u, w_gate, w_up, b_gate):
    b = qkv.shape[0]
    n_steps = SEQ // A_STEP
    n_blocks = SEQ // A_BLOCK
    kcol, vcol = col_k // A_KV_W, col_v // A_KV_W

    def edge(col, shift):
        def index_map(bi, m):
            blk = jnp.clip(m * A_QB + shift, 0, n_blocks - 1)
            return (bi, blk, col)
        return pl.BlockSpec((None, A_BLOCK, A_KV_W), index_map)

    def centre(col):
        return pl.BlockSpec((None, A_STEP, A_KV_W), lambda bi, m: (bi, m, col))

    return pl.pallas_call(
        _attn_win_kernel,
        grid=(b, n_steps),
        in_specs=[
            pl.BlockSpec(memory_space=pltpu.SMEM),
            pl.BlockSpec((None, A_STEP, A_Q_W), lambda bi, m: (bi, m, col_q // A_Q_W)),
            edge(kcol, -1), centre(kcol), edge(kcol, A_QB),
            edge(vcol, -1), centre(vcol), edge(vcol, A_QB),
            pl.BlockSpec((None, A_STEP, D_MODEL), lambda bi, m: (bi, m, 0)),
            pl.BlockSpec((D_MODEL, D_MODEL), lambda bi, m: (0, 0)),
            pl.BlockSpec((A_Q_W, D_MODEL), lambda bi, m: (0, 0)),
            pl.BlockSpec((1, D_MODEL), lambda bi, m: (0, 0)),
        ],
        out_specs=pl.BlockSpec((None, A_STEP, D_MODEL), lambda bi, m: (bi, m, 0)),
        out_shape=jax.ShapeDtypeStruct((b, SEQ, D_MODEL), F32),
        scratch_shapes=[pltpu.VMEM((A_STEP + 2 * A_BLOCK, A_KV_W), BF16),
                        pltpu.VMEM((A_STEP + 2 * A_BLOCK, A_KV_W), BF16),
                        pltpu.VMEM((3, A_Q_HEADS, A_BLOCK, 3 * A_BLOCK), F32),
                        pltpu.VMEM((A_STEP, A_Q_W), BF16)],
        compiler_params=_compiler_params(2),
        name="attn_win",
    )(sink, qkv, qkv, qkv, qkv, qkv, qkv, qkv, u, w_gate, w_up, b_gate)


def _nbr_row_start(row):
    return min(max(row - B_WIN_R // 2, 0), GRID_ROWS - B_WIN_R)


def _nbr_key_row0(pair):
    return min(_nbr_row_start(pair * B_PAIR_ROWS), GRID_ROWS - B_KEY_ROWS)


def _build_nbr_bias(rpb_ref, tbl_ref, toep_l, toep_r):
    lanes = 2 * GRID_W
    qc = lax.broadcasted_iota(jnp.int32, (GRID_W, lanes), 0)
    lane = lax.broadcasted_iota(jnp.int32, (GRID_W, lanes), 1)
    col_start = jnp.clip(qc - B_WIN_C // 2, 0, GRID_W - B_WIN_C)
    left = lane < GRID_W
    neg = jnp.full((GRID_W, lanes), NEG_INF, F32)
    n_rel_r, n_rel_c = 2 * B_WIN_R - 1, 2 * B_WIN_C - 1

    def per_head(h, carry):
        for half, dst in ((0, toep_l), (1, toep_r)):
            kc = lane - half * GRID_W
            rel = kc - qc + (B_WIN_C - 1)
            col_ok = (kc >= col_start) & (kc < col_start + B_WIN_C) & (left if half == 0 else ~left)
            for a in range(n_rel_r):
                t = jnp.zeros((GRID_W, lanes), F32)
                for c in range(n_rel_c):
                    t = jnp.where(rel == c, rpb_ref[h, a * n_rel_c + c] * LOG2E, t)
                dst[a] = jnp.where(col_ok, t, neg)
        for vi, pair in enumerate(B_VARIANT_PAIRS):
            row0 = _nbr_key_row0(pair)
            for ql in range(B_PAIR_ROWS):
                q_row = pair * B_PAIR_ROWS + ql
                start = _nbr_row_start(q_row)
                for mt in range(B_KEY_ROWS // 2):
                    halves = []
                    for half, src in ((0, toep_l), (1, toep_r)):
                        k_row = row0 + 2 * mt + half
                        if start <= k_row < start + B_WIN_R:
                            halves.append(src[k_row - q_row + B_WIN_R - 1])
                        else:
                            halves.append(neg)
                    tbl_ref[vi, h, ql * GRID_W:(ql + 1) * GRID_W, mt * lanes:(mt + 1) * lanes] = (
                        jnp.where(left, halves[0], halves[1]))
        return carry

    lax.fori_loop(0, B_HEADS, per_head, 0)


def _nbr_window_block(step):
    return jnp.clip(step * (B_STEP // B_WINDOW_ALIGN) - 1, 0, (SEQ - B_WINDOW_TOK) // B_WINDOW_ALIGN)


def _attn_nbr_kernel(rpb_ref, q_ref, k_ref, v_ref, u_ref, wg_ref, wup_ref, bg_ref, sa_ref, o_ref,
                     tbl_ref, toep_l, toep_r, attn_scr, sig_scr):
    step = pl.program_id(1)

    @pl.when((pl.program_id(0) == 0) & (step == 0))
    def _():
        _build_nbr_bias(rpb_ref, tbl_ref, toep_l, toep_r)

    window_row0 = _nbr_window_block(step) * (B_WINDOW_ALIGN // GRID_W)
    piece = D_MODEL // B_PAIRS_PER_STEP
    head_cols = [slice(h * HEAD_DIM, (h + 1) * HEAD_DIM) for h in range(B_HEADS)]

    def gate_piece(i):
        cols = slice(i * piece, (i + 1) * piece)
        gate = jnp.dot(u_ref[...], wg_ref[:, cols], preferred_element_type=F32) + bg_ref[:, cols]
        sig_scr[:, cols] = jax.nn.sigmoid(gate)

    for pp in range(B_PAIRS_PER_STEP):
        pair = step * B_PAIRS_PER_STEP + pp
        row0 = jnp.clip(pair * B_PAIR_ROWS - B_WIN_R // 2, 0, GRID_ROWS - B_KEY_ROWS)
        variant = jnp.where(pair < 2, pair, jnp.where(pair >= B_N_PAIRS - 2, pair - (B_N_PAIRS - 5), 2))
        k0 = pl.multiple_of((row0 - window_row0) * GRID_W, GRID_W)
        q_rows = slice(pp * B_PAIR_TOK, (pp + 1) * B_PAIR_TOK)
        scores = [
            lax.dot_general(q_ref[q_rows, cols], k_ref[0, pl.ds(k0, B_KEY_TOK), cols],
                            (((1,), (1,)), ((), ())), preferred_element_type=F32)
            for cols in head_cols]
        gate_piece(pp)
        s = jnp.concatenate(scores, axis=0)
        logits = s + tbl_ref[variant].reshape(B_HEADS * B_PAIR_TOK, B_KEY_TOK)
        mx = jnp.max(logits, axis=-1, keepdims=True)
        p = jnp.exp2(logits - mx)
        inv_den = 1.0 / jnp.sum(p, axis=-1, keepdims=True)
        p = p.astype(BF16)
        for h, cols in enumerate(head_cols):
            rows = slice(h * B_PAIR_TOK, (h + 1) * B_PAIR_TOK)
            o = jnp.dot(p[rows, :], v_ref[0, pl.ds(k0, B_KEY_TOK), cols], preferred_element_type=F32)
            attn_scr[q_rows, cols] = (o * inv_den[rows, :]).astype(BF16)

    attn = attn_scr[...]
    for c in range(D_MODEL // GATE_TC):
        cols = slice(c * GATE_TC, (c + 1) * GATE_TC)
        up = jnp.dot(attn, wup_ref[:, cols], preferred_element_type=F32)
        o_ref[:, cols] = (sa_ref[:, cols] + sig_scr[:, cols] * up).astype(BF16)


def _attn_nbr(qkv, rpb, col_q, col_k, col_v, u, w_gate, w_up, b_gate, sa):
    b = qkv.shape[0]
    assert B_WINDOW_ALIGN == (B_WIN_R // 2) * GRID_W and B_STEP % B_WINDOW_ALIGN == 0

    def window(col):
        return pl.BlockSpec((pl.Element(1), pl.Element(B_WINDOW_TOK), pl.Element(B_W)),
                            lambda bi, s: (bi, _nbr_window_block(s) * B_WINDOW_ALIGN, col))

    resident = dict(pipeline_mode=pl.Buffered(1))
    return pl.pallas_call(
        _attn_nbr_kernel,
        grid=(b, SEQ // B_STEP),
        in_specs=[
            pl.BlockSpec(memory_space=pltpu.SMEM),
            pl.BlockSpec((None, B_STEP, B_W), lambda bi, s: (bi, s, col_q // B_W)),
            window(col_k), window(col_v),
            pl.BlockSpec((None, B_STEP, D_MODEL), lambda bi, s: (bi, s, 0)),
            pl.BlockSpec((D_MODEL, D_MODEL), lambda bi, s: (0, 1), **resident),
            pl.BlockSpec((B_W, D_MODEL), lambda bi, s: (0, 0), **resident),
            pl.BlockSpec((1, D_MODEL), lambda bi, s: (0, 1)),
            pl.BlockSpec((None, B_STEP, D_MODEL), lambda bi, s: (bi, s, 0)),
        ],
        out_specs=pl.BlockSpec((None, B_STEP, D_MODEL), lambda bi, s: (bi, s, 0)),
        out_shape=jax.ShapeDtypeStruct((b, SEQ, D_MODEL), BF16),
        scratch_shapes=[
            pltpu.VMEM((len(B_VARIANT_PAIRS), B_HEADS, B_PAIR_TOK, B_KEY_TOK), F32),
            pltpu.VMEM((2 * B_WIN_R - 1, GRID_W, 2 * GRID_W), F32),
            pltpu.VMEM((2 * B_WIN_R - 1, GRID_W, 2 * GRID_W), F32),
            pltpu.VMEM((B_STEP, B_W), BF16),
            pltpu.VMEM((B_STEP, D_MODEL), F32),
        ],
        compiler_params=_compiler_params(2),
        name="attn_nbr",
    )(rpb, qkv, qkv, qkv, u, w_gate, w_up, b_gate, sa)


def _out_proj_kernel(m_ref, h_ref, w_ref, post_g_ref, o_ref):
    gain = post_g_ref[...]
    for r in range(OUT_TM // OUT_ROWS):
        y = jnp.dot(m_ref[r * OUT_ROWS:(r + 1) * OUT_ROWS, :], w_ref[...], preferred_element_type=F32)
        for c in range(OUT_ROWS // NORM_ROWS):
            rs = slice(r * OUT_ROWS + c * NORM_ROWS, r * OUT_ROWS + (c + 1) * NORM_ROWS)
            o_ref[rs, :] = h_ref[rs, :] + _rms_scale(y[c * NORM_ROWS:(c + 1) * NORM_ROWS, :]) * gain


def _out_proj(m, h, w_out, post_g):
    t = m.shape[0]
    return pl.pallas_call(
        _out_proj_kernel,
        grid=(t // OUT_TM,),
        in_specs=[
            pl.BlockSpec((OUT_TM, D_MODEL), lambda i: (i, 0)),
            pl.BlockSpec((OUT_TM, D_MODEL), lambda i: (i, 0)),
            pl.BlockSpec((D_MODEL, D_MODEL), lambda i: (0, 0), pipeline_mode=pl.Buffered(1)),
            pl.BlockSpec((1, D_MODEL), lambda i: (0, 0)),
        ],
        out_specs=pl.BlockSpec((OUT_TM, D_MODEL), lambda i: (i, 0)),
        out_shape=jax.ShapeDtypeStruct((t, D_MODEL), F32),
        compiler_params=_compiler_params(1),
        name="out_proj",
    )(m, h, w_out, post_g)


def _cast_mixer_w_in(src_ref, qkv_ref, gate_ref):
    for name in QKV_ORDER:
        lo, hi = QKV_SRC[name]
        blk = src_ref[:, lo:hi]
        if name in ("qa", "qb"):
            blk = blk * Q_PRESCALE
        qkv_ref[:, QKV_COL[name]:QKV_COL[name] + hi - lo] = blk.astype(BF16)
    gate_ref[...] = src_ref[:, QKV_W:].astype(BF16)


def kernel(x, ffn1_pre_g, ffn1_w_in, ffn1_w_out, ffn1_post_g, mix_pre_g, w_in, b_gate, sink_a, rpb_b,
           w_up_a, w_up_b, w_out, mix_post_g, ffn2_pre_g, ffn2_w_in, ffn2_w_out, ffn2_post_g):
    batch, seq, d = x.shape
    assert (seq, d) == (SEQ, D_MODEL)
    depth = ffn1_w_in.shape[0]
    tokens = batch * seq
    n_tiles, nf = tokens // FFN_TM, D_FF // FFN_FC
    h = x.reshape(tokens, d)

    def row(v):
        return v.reshape(1, -1).astype(F32)

    def grid_job(src, block, index_map):
        spec = pl.BlockSpec(block, index_map)
        return CastJob(src, spec, (spec,), (jax.ShapeDtypeStruct(src.shape, BF16),), _cast_block)

    for l in range(depth):
        jobs = (
            grid_job(ffn2_w_in[l], (D_MODEL // n_tiles, 2 * D_FF // nf), lambda i, j: (i, j)),
            grid_job(ffn2_w_out[l], (D_FF // nf, D_MODEL // n_tiles), lambda i, j: (j, i)),
            _tile_rows_job(w_in[l], n_tiles, nf, _cast_mixer_w_in, (QKV_W, 2 * D_MODEL)),
            _tile_rows_job(w_out[l], n_tiles, nf),
            _tile_rows_job(w_up_a[l], n_tiles, nf),
            _tile_rows_job(w_up_b[l], n_tiles, nf),
        )
        h, w2_in, w2_out, w_qkv, w_gate, w_o, w_ua, w_ub = _ffn(
            h, row(ffn1_pre_g[l]), ffn1_w_in[l].astype(BF16), ffn1_w_out[l].astype(BF16),
            row(ffn1_post_g[l]), jobs)
        u, qkv = _qkv_proj(h, row(mix_pre_g[l]), w_qkv)
        qkv = qkv.reshape(batch, seq, QKV_W)
        u = u.reshape(batch, seq, d)
        sa = _attn_win(qkv, sink_a[l].astype(F32), QKV_COL["qa"], QKV_COL["ka"], QKV_COL["va"],
                       u, w_gate, w_ua, row(b_gate[l]))
        mixed = _attn_nbr(qkv, rpb_b[l].astype(F32).reshape(B_HEADS, -1),
                          QKV_COL["qb"], QKV_COL["kb"], QKV_COL["vb"],
                          u, w_gate, w_ub, row(b_gate[l]), sa)
        h = _out_proj(mixed.reshape(tokens, d), h, w_o, row(mix_post_g[l]))
        h, = _ffn(h, row(ffn2_pre_g[l]), w2_in, w2_out, row(ffn2_post_g[l]))
    return h.reshape(batch, seq, d)
```

```python
import functools
import math
from typing import Callable, NamedTuple

import jax
import jax.numpy as jnp
import numpy as np
from jax import lax
from jax.experimental import pallas as pl
from jax.experimental.pallas import tpu as pltpu

D_MODEL = 2048
SEQ = 4096
HEAD_DIM = 128
A_Q_HEADS = 8
A_KV_HEADS = 2
A_GROUP = A_Q_HEADS // A_KV_HEADS
A_WINDOW = 128
A_BLOCK = 128
B_HEADS = 8
GRID_W = 64
GRID_ROWS = SEQ // GRID_W
B_WIN_R = 8
B_WIN_C = 16
D_FF = 5632
MACARON_W = 0.5
RMS_EPS = 1e-6
NEG_INF = -1e30
SCALE = HEAD_DIM ** -0.5
LOG2E = math.log2(math.e)
Q_PRESCALE = SCALE * LOG2E

A_Q_W = A_Q_HEADS * HEAD_DIM
A_KV_W = A_KV_HEADS * HEAD_DIM
B_W = B_HEADS * HEAD_DIM
QKV_W = A_Q_W + 2 * A_KV_W + 3 * B_W

_cuts = np.cumsum([0, A_Q_W, A_KV_W, A_KV_W, B_W, B_W, B_W]).tolist()
QKV_SRC = {name: (_cuts[i], _cuts[i + 1]) for i, name in enumerate(("qa", "ka", "va", "qb", "kb", "vb"))}
QKV_ORDER = ("qa", "qb", "kb", "vb", "ka", "va")
QKV_COL = {}
for _name in QKV_ORDER:
    QKV_COL[_name] = sum(QKV_SRC[n][1] - QKV_SRC[n][0] for n in QKV_ORDER[:QKV_ORDER.index(_name)])

F32 = jnp.float32
BF16 = jnp.bfloat16

VMEM_LIMIT_BYTES = 61 * 1024 * 1024

FFN_TM = 1024
FFN_FC = 512
FFN_ROWS = 256
QKV_TM = 512
QKV_ROWS = 256
OUT_TM = 1024
OUT_ROWS = 256
NORM_ROWS = 32
CAST_ROWS = 16

GATE_TC = 512
A_QB = 4
A_STEP = A_QB * A_BLOCK
B_PAIR_ROWS = 2
B_PAIR_TOK = B_PAIR_ROWS * GRID_W
B_KEY_ROWS = 10
B_KEY_TOK = B_KEY_ROWS * GRID_W
B_PAIRS_PER_STEP = 2
B_STEP = B_PAIRS_PER_STEP * B_PAIR_TOK
B_N_PAIRS = GRID_ROWS // B_PAIR_ROWS
B_WINDOW_ALIGN = 256
B_WINDOW_TOK = B_STEP + 2 * B_WINDOW_ALIGN
B_VARIANT_PAIRS = (0, 1, 2, B_N_PAIRS - 2, B_N_PAIRS - 1)


def _alibi_slopes(n_heads):
    return [2.0 ** (-8.0 * (i + 1) / n_heads) for i in range(n_heads)]


def _compiler_params(n_axes):
    return pltpu.CompilerParams(dimension_semantics=("arbitrary",) * n_axes,
                                vmem_limit_bytes=VMEM_LIMIT_BYTES)


def _rms_scale(x):
    return x * lax.rsqrt(jnp.mean(x * x, axis=-1, keepdims=True) + RMS_EPS)


class CastJob(NamedTuple):
    srcs: tuple
    in_specs: tuple
    out_specs: tuple
    out_shapes: tuple
    body: Callable


def _cast_block(src_ref, dst_ref):
    dst_ref[...] = src_ref[...].astype(BF16)


def _cast_gate_up(gate_ref, up_ref, dst_ref):
    dst_ref[:, :FFN_FC] = gate_ref[...].astype(BF16)
    dst_ref[:, FFN_FC:] = up_ref[...].astype(BF16)


def _tile_rows_job(src, n_tiles, n_steps, body=_cast_block, out_cols=None):
    rows, cols = src.shape
    per_tile = rows // (n_tiles * CAST_ROWS)
    assert per_tile * n_tiles * CAST_ROWS == rows and per_tile <= n_steps
    index_map = lambda i, j: (i * per_tile + jnp.minimum(j, per_tile - 1), 0)
    out_cols = (cols,) if out_cols is None else out_cols
    return CastJob((src,), (pl.BlockSpec((CAST_ROWS, cols), index_map),),
                   tuple(pl.BlockSpec((CAST_ROWS, c), index_map) for c in out_cols),
                   tuple(jax.ShapeDtypeStruct((rows, c), BF16) for c in out_cols), body)


def _ffn_kernel(*refs, jobs):
    x_ref, pre_g_ref, wgu_ref, wo_ref, post_g_ref = refs[:5]
    n_srcs = sum(len(job.srcs) for job in jobs)
    job_srcs = refs[5:5 + n_srcs]
    o_ref = refs[5 + n_srcs]
    job_dsts = refs[6 + n_srcs:-1]
    n_ref = refs[-1]
    j = pl.program_id(1)
    last = pl.num_programs(1) - 1
    row_chunks = [slice(r * FFN_ROWS, (r + 1) * FFN_ROWS) for r in range(FFN_TM // FFN_ROWS)]

    def swiglu(n):
        gu = jnp.dot(n, wgu_ref[...], preferred_element_type=F32)
        g, u = gu[:, :FFN_FC], gu[:, FFN_FC:]
        a = (g * jax.nn.sigmoid(g) * u).astype(BF16)
        return jnp.dot(a, wo_ref[...], preferred_element_type=F32)

    def run_jobs():
        src0 = dst0 = 0
        for job in jobs:
            job.body(*job_srcs[src0:src0 + len(job.srcs)], *job_dsts[dst0:dst0 + len(job.out_specs)])
            src0 += len(job.srcs)
            dst0 += len(job.out_specs)

    @pl.when(j == 0)
    def _():
        gain = pre_g_ref[...]
        for rows in row_chunks:
            for c in range(rows.start, rows.stop, NORM_ROWS):
                rs = slice(c, c + NORM_ROWS)
                n_ref[rs, :] = (_rms_scale(x_ref[rs, :]) * gain).astype(BF16)
            o_ref[rows, :] = swiglu(n_ref[rows, :])
        run_jobs()

    @pl.when((j > 0) & (j < last))
    def _():
        o_ref[...] += swiglu(n_ref[...])
        run_jobs()

    @pl.when(j == last)
    def _():
        gain = MACARON_W * post_g_ref[...]
        for rows in row_chunks:
            f = o_ref[rows, :] + swiglu(n_ref[rows, :])
            for c in range(0, FFN_ROWS, NORM_ROWS):
                rs = slice(rows.start + c, rows.start + c + NORM_ROWS)
                o_ref[rs, :] = x_ref[rs, :] + _rms_scale(f[c:c + NORM_ROWS, :]) * gain
        run_jobs()


def _ffn(x, pre_g, w_in, w_out, post_g, jobs=()):
    t = x.shape[0]
    nf = D_FF // FFN_FC
    assert nf >= 2
    out_specs = [pl.BlockSpec((FFN_TM, D_MODEL), lambda i, j: (i, 0))]
    out_shapes = [jax.ShapeDtypeStruct((t, D_MODEL), F32)]
    for job in jobs:
        out_specs.extend(job.out_specs)
        out_shapes.extend(job.out_shapes)
    return pl.pallas_call(
        functools.partial(_ffn_kernel, jobs=jobs),
        grid=(t // FFN_TM, nf),
        in_specs=[
            pl.BlockSpec((FFN_TM, D_MODEL), lambda i, j: (i, 0)),
            pl.BlockSpec((1, D_MODEL), lambda i, j: (0, 0)),
            pl.BlockSpec((None, D_MODEL, 2 * FFN_FC), lambda i, j: (j, 0, 0)),
            pl.BlockSpec((FFN_FC, D_MODEL), lambda i, j: (j, 0)),
            pl.BlockSpec((1, D_MODEL), lambda i, j: (0, 0)),
        ] + [spec for job in jobs for spec in job.in_specs],
        out_specs=out_specs,
        out_shape=out_shapes,
        scratch_shapes=[pltpu.VMEM((FFN_TM, D_MODEL), BF16)],
        compiler_params=_compiler_params(2),
        name="ffn",
    )(x, pre_g, w_in, w_out, post_g, *[src for job in jobs for src in job.srcs])


def _qkv_kernel(h_ref, g_ref, w_ref, u_ref, o_ref):
    gain = g_ref[...]
    for r in range(0, QKV_TM, QKV_ROWS):
        for c in range(r, r + QKV_ROWS, NORM_ROWS):
            rs = slice(c, c + NORM_ROWS)
            u_ref[rs, :] = (_rms_scale(h_ref[rs, :]) * gain).astype(BF16)
        rows = slice(r, r + QKV_ROWS)
        o_ref[rows, :] = jnp.dot(u_ref[rows, :], w_ref[...], preferred_element_type=F32).astype(BF16)


def _qkv_proj(h, gain, w_qkv):
    t = h.shape[0]
    return pl.pallas_call(
        _qkv_kernel,
        grid=(t // QKV_TM,),
        in_specs=[
            pl.BlockSpec((QKV_TM, D_MODEL), lambda i: (i, 0)),
            pl.BlockSpec((1, D_MODEL), lambda i: (0, 0)),
            pl.BlockSpec((D_MODEL, QKV_W), lambda i: (0, 0), pipeline_mode=pl.Buffered(1)),
        ],
        out_specs=[
            pl.BlockSpec((QKV_TM, D_MODEL), lambda i: (i, 0)),
            pl.BlockSpec((QKV_TM, QKV_W), lambda i: (i, 0)),
        ],
        out_shape=[jax.ShapeDtypeStruct((t, D_MODEL), BF16),
                   jax.ShapeDtypeStruct((t, QKV_W), BF16)],
        compiler_params=_compiler_params(1),
        name="qkv_proj",
    )(h, gain, w_qkv)


def _build_win_bias(tbl_ref):
    span = 3 * A_BLOCK
    qi = lax.broadcasted_iota(jnp.int32, (A_BLOCK, span), 0)
    kj = lax.broadcasted_iota(jnp.int32, (A_BLOCK, span), 1)
    absd_i = jnp.abs(qi + A_WINDOW - kj)
    absd = absd_i.astype(F32)
    in_window = absd_i <= A_WINDOW
    valid = (in_window & (kj >= A_BLOCK), in_window, in_window & (kj < 2 * A_BLOCK))
    for h, slope in enumerate(_alibi_slopes(A_Q_HEADS)):
        bias = (-slope * LOG2E) * absd
        for variant in range(3):
            tbl_ref[variant, h] = jnp.where(valid[variant], bias, NEG_INF)


def _attn_win_kernel(sink_ref, q_ref, kp_ref, kc_ref, kn_ref, vp_ref, vc_ref, vn_ref, u_ref, wg_ref, wup_ref,
                     bg_ref, o_ref, k_scr, v_scr, tbl_ref, a_scr):
    m = pl.program_id(1)

    @pl.when((pl.program_id(0) == 0) & (m == 0))
    def _():
        _build_win_bias(tbl_ref)

    k_scr[0:A_BLOCK, :] = kp_ref[...]
    k_scr[A_BLOCK:A_BLOCK + A_STEP, :] = kc_ref[...]
    k_scr[A_BLOCK + A_STEP:, :] = kn_ref[...]
    v_scr[0:A_BLOCK, :] = vp_ref[...]
    v_scr[A_BLOCK:A_BLOCK + A_STEP, :] = vc_ref[...]
    v_scr[A_BLOCK + A_STEP:, :] = vn_ref[...]
    span = 3 * A_BLOCK
    last_block = SEQ // A_BLOCK - 1
    piece = D_MODEL // (A_QB * A_KV_HEADS)

    def gate_piece(i):
        cols = slice(i * piece, (i + 1) * piece)
        gate = jnp.dot(u_ref[...], wg_ref[:, cols], preferred_element_type=F32) + bg_ref[:, cols]
        o_ref[:, cols] = jax.nn.sigmoid(gate)

    def sub_block(t):
        r = t * A_BLOCK
        block = m * A_QB + t
        variant = jnp.where(block == 0, 0, jnp.where(block == last_block, 2, 1))
        for g in range(A_KV_HEADS):
            k = k_scr[pl.ds(r, span), g * HEAD_DIM:(g + 1) * HEAD_DIM]
            v = v_scr[pl.ds(r, span), g * HEAD_DIM:(g + 1) * HEAD_DIM]
            heads = [g * A_GROUP + e for e in range(A_GROUP)]
            qs = jnp.concatenate(
                [q_ref[pl.ds(r, A_BLOCK), h * HEAD_DIM:(h + 1) * HEAD_DIM] for h in heads], axis=0)
            s = lax.dot_general(qs, k, (((1,), (1,)), ((), ())), preferred_element_type=F32)
            gate_piece(t * A_KV_HEADS + g)
            probs, dens = [], []
            for e, h in enumerate(heads):
                logits = s[e * A_BLOCK:(e + 1) * A_BLOCK, :] + tbl_ref[variant, h]
                sink = sink_ref[h] * LOG2E
                mx = jnp.maximum(jnp.max(logits, axis=-1, keepdims=True), sink)
                p = jnp.exp2(logits - mx)
                dens.append(jnp.sum(p, axis=-1, keepdims=True) + jnp.exp2(sink - mx))
                probs.append(p.astype(BF16))
            o = jnp.dot(jnp.concatenate(probs, axis=0), v, preferred_element_type=F32)
            for e, h in enumerate(heads):
                a_scr[pl.ds(r, A_BLOCK), h * HEAD_DIM:(h + 1) * HEAD_DIM] = (
                    o[e * A_BLOCK:(e + 1) * A_BLOCK, :] / dens[e]).astype(BF16)

    for t in range(A_QB):
        sub_block(t)
    attn = a_scr[...]
    for c in range(D_MODEL // GATE_TC):
        cols = slice(c * GATE_TC, (c + 1) * GATE_TC)
        o_ref[:, cols] = o_ref[:, cols] * jnp.dot(attn, wup_ref[:, cols], preferred_element_type=F32)


def _attn_win(qkv, sink, col_q, col_k, col_v, u, w_gate, w_up, b_gate):
    b = qkv.shape[0]
    n_steps = SEQ // A_STEP
    n_blocks = SEQ // A_BLOCK
    kcol, vcol = col_k // A_KV_W, col_v // A_KV_W

    def edge(col, shift):
        def index_map(bi, m):
            blk = jnp.clip(m * A_QB + shift, 0, n_blocks - 1)
            return (bi, blk, col)
        return pl.BlockSpec((None, A_BLOCK, A_KV_W), index_map)

    def centre(col):
        return pl.BlockSpec((None, A_STEP, A_KV_W), lambda bi, m: (bi, m, col))

    return pl.pallas_call(
        _attn_win_kernel,
        grid=(b, n_steps),
        in_specs=[
            pl.BlockSpec(memory_space=pltpu.SMEM),
            pl.BlockSpec((None, A_STEP, A_Q_W), lambda bi, m: (bi, m, col_q // A_Q_W)),
            edge(kcol, -1), centre(kcol), edge(kcol, A_QB),
            edge(vcol, -1), centre(vcol), edge(vcol, A_QB),
            pl.BlockSpec((None, A_STEP, D_MODEL), lambda bi, m: (bi, m, 0)),
            pl.BlockSpec((D_MODEL, D_MODEL), lambda bi, m: (0, 0)),
            pl.BlockSpec((A_Q_W, D_MODEL), lambda bi, m: (0, 0)),
            pl.BlockSpec((1, D_MODEL), lambda bi, m: (0, 0)),
        ],
        out_specs=pl.BlockSpec((None, A_STEP, D_MODEL), lambda bi, m: (bi, m, 0)),
        out_shape=jax.ShapeDtypeStruct((b, SEQ, D_MODEL), F32),
        scratch_shapes=[pltpu.VMEM((A_STEP + 2 * A_BLOCK, A_KV_W), BF16),
                        pltpu.VMEM((A_STEP + 2 * A_BLOCK, A_KV_W), BF16),
                        pltpu.VMEM((3, A_Q_HEADS, A_BLOCK, 3 * A_BLOCK), F32),
                        pltpu.VMEM((A_STEP, A_Q_W), BF16)],
        compiler_params=_compiler_params(2),
        name="attn_win",
    )(sink, qkv, qkv, qkv, qkv, qkv, qkv, qkv, u, w_gate, w_up, b_gate)


def _nbr_row_start(row):
    return min(max(row - B_WIN_R // 2, 0), GRID_ROWS - B_WIN_R)


def _nbr_key_row0(pair):
    return min(_nbr_row_start(pair * B_PAIR_ROWS), GRID_ROWS - B_KEY_ROWS)


def _build_nbr_bias(rpb_ref, tbl_ref, toep_l, toep_r):
    lanes = 2 * GRID_W
    qc = lax.broadcasted_iota(jnp.int32, (GRID_W, lanes), 0)
    lane = lax.broadcasted_iota(jnp.int32, (GRID_W, lanes), 1)
    col_start = jnp.clip(qc - B_WIN_C // 2, 0, GRID_W - B_WIN_C)
    left = lane < GRID_W
    neg = jnp.full((GRID_W, lanes), NEG_INF, F32)
    n_rel_r, n_rel_c = 2 * B_WIN_R - 1, 2 * B_WIN_C - 1

    def per_head(h, carry):
        for half, dst in ((0, toep_l), (1, toep_r)):
            kc = lane - half * GRID_W
            rel = kc - qc + (B_WIN_C - 1)
            col_ok = (kc >= col_start) & (kc < col_start + B_WIN_C) & (left if half == 0 else ~left)
            for a in range(n_rel_r):
                t = jnp.zeros((GRID_W, lanes), F32)
                for c in range(n_rel_c):
                    t = jnp.where(rel == c, rpb_ref[h, a * n_rel_c + c] * LOG2E, t)
                dst[a] = jnp.where(col_ok, t, neg)
        for vi, pair in enumerate(B_VARIANT_PAIRS):
            row0 = _nbr_key_row0(pair)
            for ql in range(B_PAIR_ROWS):
                q_row = pair * B_PAIR_ROWS + ql
                start = _nbr_row_start(q_row)
                for mt in range(B_KEY_ROWS // 2):
                    halves = []
                    for half, src in ((0, toep_l), (1, toep_r)):
                        k_row = row0 + 2 * mt + half
                        if start <= k_row < start + B_WIN_R:
                            halves.append(src[k_row - q_row + B_WIN_R - 1])
                        else:
                            halves.append(neg)
                    tbl_ref[vi, h, ql * GRID_W:(ql + 1) * GRID_W, mt * lanes:(mt + 1) * lanes] = (
                        jnp.where(left, halves[0], halves[1]))
        return carry

    lax.fori_loop(0, B_HEADS, per_head, 0)


def _nbr_window_block(step):
    return jnp.clip(step * (B_STEP // B_WINDOW_ALIGN) - 1, 0, (SEQ - B_WINDOW_TOK) // B_WINDOW_ALIGN)


def _attn_nbr_kernel(rpb_ref, q_ref, k_ref, v_ref, u_ref, wg_ref, wup_ref, bg_ref, sa_ref, o_ref,
                     tbl_ref, toep_l, toep_r, attn_scr, sig_scr):
    step = pl.program_id(1)

    @pl.when((pl.program_id(0) == 0) & (step == 0))
    def _():
        _build_nbr_bias(rpb_ref, tbl_ref, toep_l, toep_r)

    window_row0 = _nbr_window_block(step) * (B_WINDOW_ALIGN // GRID_W)
    piece = D_MODEL // B_PAIRS_PER_STEP
    head_cols = [slice(h * HEAD_DIM, (h + 1) * HEAD_DIM) for h in range(B_HEADS)]

    def gate_piece(i):
        cols = slice(i * piece, (i + 1) * piece)
        gate = jnp.dot(u_ref[...], wg_ref[:, cols], preferred_element_type=F32) + bg_ref[:, cols]
        sig_scr[:, cols] = jax.nn.sigmoid(gate)

    for pp in range(B_PAIRS_PER_STEP):
        pair = step * B_PAIRS_PER_STEP + pp
        row0 = jnp.clip(pair * B_PAIR_ROWS - B_WIN_R // 2, 0, GRID_ROWS - B_KEY_ROWS)
        variant = jnp.where(pair < 2, pair, jnp.where(pair >= B_N_PAIRS - 2, pair - (B_N_PAIRS - 5), 2))
        k0 = pl.multiple_of((row0 - window_row0) * GRID_W, GRID_W)
        q_rows = slice(pp * B_PAIR_TOK, (pp + 1) * B_PAIR_TOK)
        scores = [
            lax.dot_general(q_ref[q_rows, cols], k_ref[0, pl.ds(k0, B_KEY_TOK), cols],
                            (((1,), (1,)), ((), ())), preferred_element_type=F32)
            for cols in head_cols]
        gate_piece(pp)
        s = jnp.concatenate(scores, axis=0)
        logits = s + tbl_ref[variant].reshape(B_HEADS * B_PAIR_TOK, B_KEY_TOK)
        mx = jnp.max(logits, axis=-1, keepdims=True)
        p = jnp.exp2(logits - mx)
        inv_den = 1.0 / jnp.sum(p, axis=-1, keepdims=True)
        p = p.astype(BF16)
        for h, cols in enumerate(head_cols):
            rows = slice(h * B_PAIR_TOK, (h + 1) * B_PAIR_TOK)
            o = jnp.dot(p[rows, :], v_ref[0, pl.ds(k0, B_KEY_TOK), cols], preferred_element_type=F32)
            attn_scr[q_rows, cols] = (o * inv_den[rows, :]).astype(BF16)

    attn = attn_scr[...]
    for c in range(D_MODEL // GATE_TC):
        cols = slice(c * GATE_TC, (c + 1) * GATE_TC)
        up = jnp.dot(attn, wup_ref[:, cols], preferred_element_type=F32)
        o_ref[:, cols] = (sa_ref[:, cols] + sig_scr[:, cols] * up).astype(BF16)


def _attn_nbr(qkv, rpb, col_q, col_k, col_v, u, w_gate, w_up, b_gate, sa):
    b = qkv.shape[0]
    assert B_WINDOW_ALIGN == (B_WIN_R // 2) * GRID_W and B_STEP % B_WINDOW_ALIGN == 0

    def window(col):
        return pl.BlockSpec((pl.Element(1), pl.Element(B_WINDOW_TOK), pl.Element(B_W)),
                            lambda bi, s: (bi, _nbr_window_block(s) * B_WINDOW_ALIGN, col))

    resident = dict(pipeline_mode=pl.Buffered(1))
    return pl.pallas_call(
        _attn_nbr_kernel,
        grid=(b, SEQ // B_STEP),
        in_specs=[
            pl.BlockSpec(memory_space=pltpu.SMEM),
            pl.BlockSpec((None, B_STEP, B_W), lambda bi, s: (bi, s, col_q // B_W)),
            window(col_k), window(col_v),
            pl.BlockSpec((None, B_STEP, D_MODEL), lambda bi, s: (bi, s, 0)),
            pl.BlockSpec((D_MODEL, D_MODEL), lambda bi, s: (0, 1), **resident),
            pl.BlockSpec((B_W, D_MODEL), lambda bi, s: (0, 0), **resident),
            pl.BlockSpec((1, D_MODEL), lambda bi, s: (0, 1)),
            pl.BlockSpec((None, B_STEP, D_MODEL), lambda bi, s: (bi, s, 0)),
        ],
        out_specs=pl.BlockSpec((None, B_STEP, D_MODEL), lambda bi, s: (bi, s, 0)),
        out_shape=jax.ShapeDtypeStruct((b, SEQ, D_MODEL), BF16),
        scratch_shapes=[
            pltpu.VMEM((len(B_VARIANT_PAIRS), B_HEADS, B_PAIR_TOK, B_KEY_TOK), F32),
            pltpu.VMEM((2 * B_WIN_R - 1, GRID_W, 2 * GRID_W), F32),
            pltpu.VMEM((2 * B_WIN_R - 1, GRID_W, 2 * GRID_W), F32),
            pltpu.VMEM((B_STEP, B_W), BF16),
            pltpu.VMEM((B_STEP, D_MODEL), F32),
        ],
        compiler_params=_compiler_params(2),
        name="attn_nbr",
    )(rpb, qkv, qkv, qkv, u, w_gate, w_up, b_gate, sa)


def _out_proj_kernel(m_ref, h_ref, w_ref, post_g_ref, o_ref):
    gain = post_g_ref[...]
    for r in range(OUT_TM // OUT_ROWS):
        y = jnp.dot(m_ref[r * OUT_ROWS:(r + 1) * OUT_ROWS, :], w_ref[...], preferred_element_type=F32)
        for c in range(OUT_ROWS // NORM_ROWS):
            rs = slice(r * OUT_ROWS + c * NORM_ROWS, r * OUT_ROWS + (c + 1) * NORM_ROWS)
            o_ref[rs, :] = h_ref[rs, :] + _rms_scale(y[c * NORM_ROWS:(c + 1) * NORM_ROWS, :]) * gain


def _out_proj(m, h, w_out, post_g):
    t = m.shape[0]
    return pl.pallas_call(
        _out_proj_kernel,
        grid=(t // OUT_TM,),
        in_specs=[
            pl.BlockSpec((OUT_TM, D_MODEL), lambda i: (i, 0)),
            pl.BlockSpec((OUT_TM, D_MODEL), lambda i: (i, 0)),
            pl.BlockSpec((D_MODEL, D_MODEL), lambda i: (0, 0), pipeline_mode=pl.Buffered(1)),
            pl.BlockSpec((1, D_MODEL), lambda i: (0, 0)),
        ],
        out_specs=pl.BlockSpec((OUT_TM, D_MODEL), lambda i: (i, 0)),
        out_shape=jax.ShapeDtypeStruct((t, D_MODEL), F32),
        compiler_params=_compiler_params(1),
        name="out_proj",
    )(m, h, w_out, post_g)


def _cast_mixer_w_in(src_ref, qkv_ref, gate_ref):
    for name in QKV_ORDER:
        lo, hi = QKV_SRC[name]
        blk = src_ref[:, lo:hi]
        if name in ("qa", "qb"):
            blk = blk * Q_PRESCALE
        qkv_ref[:, QKV_COL[name]:QKV_COL[name] + hi - lo] = blk.astype(BF16)
    gate_ref[...] = src_ref[:, QKV_W:].astype(BF16)


def kernel(x, ffn1_pre_g, ffn1_w_in, ffn1_w_out, ffn1_post_g, mix_pre_g, w_in, b_gate, sink_a, rpb_b,
           w_up_a, w_up_b, w_out, mix_post_g, ffn2_pre_g, ffn2_w_in, ffn2_w_out, ffn2_post_g):
    batch, seq, d = x.shape
    assert (seq, d) == (SEQ, D_MODEL)
    depth = ffn1_w_in.shape[0]
    tokens = batch * seq
    n_tiles, nf = tokens // FFN_TM, D_FF // FFN_FC
    h = x.reshape(tokens, d)

    def row(v):
        return v.reshape(1, -1).astype(F32)

    def grid_job(src, block, index_map):
        spec = pl.BlockSpec(block, index_map)
        return CastJob((src,), (spec,), (spec,), (jax.ShapeDtypeStruct(src.shape, BF16),), _cast_block)

    def ffn_w_in_job(src):
        rows = D_MODEL // n_tiles
        return CastJob(
            (src, src),
            (pl.BlockSpec((rows, FFN_FC), lambda i, j: (i, j)), pl.BlockSpec((rows, FFN_FC), lambda i, j: (i, j + nf))),
            (pl.BlockSpec((None, rows, 2 * FFN_FC), lambda i, j: (j, i, 0)),),
            (jax.ShapeDtypeStruct((nf, D_MODEL, 2 * FFN_FC), BF16),), _cast_gate_up)

    def ffn_w_in_layout(w):
        w = w.reshape(D_MODEL, 2, nf, FFN_FC).transpose(2, 0, 1, 3)
        return w.reshape(nf, D_MODEL, 2 * FFN_FC).astype(BF16)

    for l in range(depth):
        jobs = (
            ffn_w_in_job(ffn2_w_in[l]),
            grid_job(ffn2_w_out[l], (D_FF // nf, D_MODEL // n_tiles), lambda i, j: (j, i)),
            _tile_rows_job(w_in[l], n_tiles, nf, _cast_mixer_w_in, (QKV_W, 2 * D_MODEL)),
            _tile_rows_job(w_out[l], n_tiles, nf),
            _tile_rows_job(w_up_a[l], n_tiles, nf),
            _tile_rows_job(w_up_b[l], n_tiles, nf),
        )
        h, w2_in, w2_out, w_qkv, w_gate, w_o, w_ua, w_ub = _ffn(
            h, row(ffn1_pre_g[l]), ffn_w_in_layout(ffn1_w_in[l]), ffn1_w_out[l].astype(BF16),
            row(ffn1_post_g[l]), jobs)
        u, qkv = _qkv_proj(h, row(mix_pre_g[l]), w_qkv)
        qkv = qkv.reshape(batch, seq, QKV_W)
        u = u.reshape(batch, seq, d)
        sa = _attn_win(qkv, sink_a[l].astype(F32), QKV_COL["qa"], QKV_COL["ka"], QKV_COL["va"],
                       u, w_gate, w_ua, row(b_gate[l]))
        mixed = _attn_nbr(qkv, rpb_b[l].astype(F32).reshape(B_HEADS, -1),
                          QKV_COL["qb"], QKV_COL["kb"], QKV_COL["vb"],
                          u, w_gate, w_ub, row(b_gate[l]), sa)
        h = _out_proj(mixed.reshape(tokens, d), h, w_o, row(mix_post_g[l]))
        h, = _ffn(h, row(ffn2_pre_g[l]), w2_in, w2_out, row(ffn2_post_g[l]))
    return h.reshape(batch, seq, d)
```

```python
import functools
import math
from typing import Callable, NamedTuple

import jax
import jax.numpy as jnp
import numpy as np
from jax import lax
from jax.experimental import pallas as pl
from jax.experimental.pallas import tpu as pltpu

D_MODEL = 2048
SEQ = 4096
HEAD_DIM = 128
A_Q_HEADS = 8
A_KV_HEADS = 2
A_GROUP = A_Q_HEADS // A_KV_HEADS
A_WINDOW = 128
A_BLOCK = 128
B_HEADS = 8
GRID_W = 64
GRID_ROWS = SEQ // GRID_W
B_WIN_R = 8
B_WIN_C = 16
D_FF = 5632
MACARON_W = 0.5
RMS_EPS = 1e-6
NEG_INF = -1e30
SCALE = HEAD_DIM ** -0.5
LOG2E = math.log2(math.e)
Q_PRESCALE = SCALE * LOG2E

A_Q_W = A_Q_HEADS * HEAD_DIM
A_KV_W = A_KV_HEADS * HEAD_DIM
B_W = B_HEADS * HEAD_DIM
QKV_W = A_Q_W + 2 * A_KV_W + 3 * B_W

_cuts = np.cumsum([0, A_Q_W, A_KV_W, A_KV_W, B_W, B_W, B_W]).tolist()
QKV_SRC = {name: (_cuts[i], _cuts[i + 1]) for i, name in enumerate(("qa", "ka", "va", "qb", "kb", "vb"))}
QKV_ORDER = ("qa", "qb", "kb", "vb", "ka", "va")
QKV_COL = {}
for _name in QKV_ORDER:
    QKV_COL[_name] = sum(QKV_SRC[n][1] - QKV_SRC[n][0] for n in QKV_ORDER[:QKV_ORDER.index(_name)])

F32 = jnp.float32
BF16 = jnp.bfloat16

VMEM_LIMIT_BYTES = 61 * 1024 * 1024

FFN_TM = 1024
FFN_FC = 512
FFN_ROWS = 256
QKV_TM = 512
QKV_ROWS = 256
OUT_TM = 1024
OUT_ROWS = 256
NORM_ROWS = 32
CAST_ROWS = 16

GATE_TC = 512
A_QB = 4
A_STEP = A_QB * A_BLOCK
B_PAIR_ROWS = 2
B_PAIR_TOK = B_PAIR_ROWS * GRID_W
B_KEY_ROWS = 10
B_KEY_TOK = B_KEY_ROWS * GRID_W
B_PAIRS_PER_STEP = 2
B_STEP = B_PAIRS_PER_STEP * B_PAIR_TOK
B_N_PAIRS = GRID_ROWS // B_PAIR_ROWS
B_WINDOW_ALIGN = 256
B_WINDOW_TOK = B_STEP + 2 * B_WINDOW_ALIGN
B_VARIANT_PAIRS = (0, 1, 2, B_N_PAIRS - 2, B_N_PAIRS - 1)


def _alibi_slopes(n_heads):
    return [2.0 ** (-8.0 * (i + 1) / n_heads) for i in range(n_heads)]


def _compiler_params(n_axes):
    return pltpu.CompilerParams(dimension_semantics=("arbitrary",) * n_axes,
                                vmem_limit_bytes=VMEM_LIMIT_BYTES)


def _rms_scale(x):
    return x * lax.rsqrt(jnp.mean(x * x, axis=-1, keepdims=True) + RMS_EPS)


class CastJob(NamedTuple):
    src: jax.Array
    in_spec: pl.BlockSpec
    out_specs: tuple
    out_shapes: tuple
    body: Callable


def _cast_block(src_ref, dst_ref):
    dst_ref[...] = src_ref[...].astype(BF16)


def _tile_rows_job(src, n_tiles, n_steps, body=_cast_block, out_cols=None):
    rows, cols = src.shape
    per_tile = rows // (n_tiles * CAST_ROWS)
    assert per_tile * n_tiles * CAST_ROWS == rows and per_tile <= n_steps
    index_map = lambda i, j: (i * per_tile + jnp.minimum(j, per_tile - 1), 0)
    out_cols = (cols,) if out_cols is None else out_cols
    return CastJob(src, pl.BlockSpec((CAST_ROWS, cols), index_map),
                   tuple(pl.BlockSpec((CAST_ROWS, c), index_map) for c in out_cols),
                   tuple(jax.ShapeDtypeStruct((rows, c), BF16) for c in out_cols), body)


def _row_slab_job(src, n_steps):
    rows, cols = src.shape
    slab = rows // n_steps
    assert slab * n_steps == rows and slab % CAST_ROWS == 0
    spec = pl.BlockSpec((slab, cols), lambda i: (i, 0))
    return CastJob(src, spec, (spec,), (jax.ShapeDtypeStruct((rows, cols), BF16),), _cast_block)


def _run_jobs(jobs, src_refs, dst_refs):
    first = 0
    for job, src_ref in zip(jobs, src_refs):
        job.body(src_ref, *dst_refs[first:first + len(job.out_specs)])
        first += len(job.out_specs)


def _ffn_kernel(*refs, jobs):
    x_ref, pre_g_ref, wg_ref, wu_ref, wo_ref, post_g_ref = refs[:6]
    job_srcs = refs[6:6 + len(jobs)]
    o_ref = refs[6 + len(jobs)]
    job_dsts = refs[7 + len(jobs):-1]
    n_ref = refs[-1]
    j = pl.program_id(1)
    last = pl.num_programs(1) - 1
    row_chunks = [slice(r * FFN_ROWS, (r + 1) * FFN_ROWS) for r in range(FFN_TM // FFN_ROWS)]

    def swiglu(n):
        g = jnp.dot(n, wg_ref[...], preferred_element_type=F32)
        u = jnp.dot(n, wu_ref[...], preferred_element_type=F32)
        a = (g * jax.nn.sigmoid(g) * u).astype(BF16)
        return jnp.dot(a, wo_ref[...], preferred_element_type=F32)

    def run_jobs():
        _run_jobs(jobs, job_srcs, job_dsts)

    @pl.when(j == 0)
    def _():
        gain = pre_g_ref[...]
        for rows in row_chunks:
            for c in range(rows.start, rows.stop, NORM_ROWS):
                rs = slice(c, c + NORM_ROWS)
                n_ref[rs, :] = (_rms_scale(x_ref[rs, :]) * gain).astype(BF16)
            o_ref[rows, :] = swiglu(n_ref[rows, :])
        run_jobs()

    @pl.when((j > 0) & (j < last))
    def _():
        o_ref[...] += swiglu(n_ref[...])
        run_jobs()

    @pl.when(j == last)
    def _():
        gain = MACARON_W * post_g_ref[...]
        for rows in row_chunks:
            f = o_ref[rows, :] + swiglu(n_ref[rows, :])
            for c in range(0, FFN_ROWS, NORM_ROWS):
                rs = slice(rows.start + c, rows.start + c + NORM_ROWS)
                o_ref[rs, :] = x_ref[rs, :] + _rms_scale(f[c:c + NORM_ROWS, :]) * gain
        run_jobs()


def _ffn(x, pre_g, w_in, w_out, post_g, jobs=()):
    t = x.shape[0]
    nf = D_FF // FFN_FC
    assert nf >= 2
    out_specs = [pl.BlockSpec((FFN_TM, D_MODEL), lambda i, j: (i, 0))]
    out_shapes = [jax.ShapeDtypeStruct((t, D_MODEL), F32)]
    for job in jobs:
        out_specs.extend(job.out_specs)
        out_shapes.extend(job.out_shapes)
    return pl.pallas_call(
        functools.partial(_ffn_kernel, jobs=jobs),
        grid=(t // FFN_TM, nf),
        in_specs=[
            pl.BlockSpec((FFN_TM, D_MODEL), lambda i, j: (i, 0)),
            pl.BlockSpec((1, D_MODEL), lambda i, j: (0, 0)),
            pl.BlockSpec((D_MODEL, FFN_FC), lambda i, j: (0, j)),
            pl.BlockSpec((D_MODEL, FFN_FC), lambda i, j: (0, j + nf)),
            pl.BlockSpec((FFN_FC, D_MODEL), lambda i, j: (j, 0)),
            pl.BlockSpec((1, D_MODEL), lambda i, j: (0, 0)),
        ] + [job.in_spec for job in jobs],
        out_specs=out_specs,
        out_shape=out_shapes,
        scratch_shapes=[pltpu.VMEM((FFN_TM, D_MODEL), BF16)],
        compiler_params=_compiler_params(2),
        name="ffn",
    )(x, pre_g, w_in, w_in, w_out, post_g, *[job.src for job in jobs])


def _qkv_kernel(*refs, jobs):
    h_ref, g_ref, w_ref = refs[:3]
    job_srcs = refs[3:3 + len(jobs)]
    u_ref, o_ref = refs[3 + len(jobs):5 + len(jobs)]
    gain = g_ref[...]
    for r in range(0, QKV_TM, QKV_ROWS):
        for c in range(r, r + QKV_ROWS, NORM_ROWS):
            rs = slice(c, c + NORM_ROWS)
            u_ref[rs, :] = (_rms_scale(h_ref[rs, :]) * gain).astype(BF16)
        rows = slice(r, r + QKV_ROWS)
        o_ref[rows, :] = jnp.dot(u_ref[rows, :], w_ref[...], preferred_element_type=F32).astype(BF16)
    _run_jobs(jobs, job_srcs, refs[5 + len(jobs):])


def _qkv_proj(h, gain, w_qkv, make_jobs):
    t = h.shape[0]
    n_steps = t // QKV_TM
    jobs = make_jobs(n_steps)
    return pl.pallas_call(
        functools.partial(_qkv_kernel, jobs=jobs),
        grid=(n_steps,),
        in_specs=[
            pl.BlockSpec((QKV_TM, D_MODEL), lambda i: (i, 0)),
            pl.BlockSpec((1, D_MODEL), lambda i: (0, 0)),
            pl.BlockSpec((D_MODEL, QKV_W), lambda i: (0, 0), pipeline_mode=pl.Buffered(1)),
        ] + [job.in_spec for job in jobs],
        out_specs=[
            pl.BlockSpec((QKV_TM, D_MODEL), lambda i: (i, 0)),
            pl.BlockSpec((QKV_TM, QKV_W), lambda i: (i, 0)),
        ] + [spec for job in jobs for spec in job.out_specs],
        out_shape=[jax.ShapeDtypeStruct((t, D_MODEL), BF16),
                   jax.ShapeDtypeStruct((t, QKV_W), BF16)] + [sh for job in jobs for sh in job.out_shapes],
        compiler_params=_compiler_params(1),
        name="qkv_proj",
    )(h, gain, w_qkv, *[job.src for job in jobs])


def _build_win_bias(tbl_ref):
    span = 3 * A_BLOCK
    qi = lax.broadcasted_iota(jnp.int32, (A_BLOCK, span), 0)
    kj = lax.broadcasted_iota(jnp.int32, (A_BLOCK, span), 1)
    absd_i = jnp.abs(qi + A_WINDOW - kj)
    absd = absd_i.astype(F32)
    in_window = absd_i <= A_WINDOW
    valid = (in_window & (kj >= A_BLOCK), in_window, in_window & (kj < 2 * A_BLOCK))
    for h, slope in enumerate(_alibi_slopes(A_Q_HEADS)):
        bias = (-slope * LOG2E) * absd
        for variant in range(3):
            tbl_ref[variant, h] = jnp.where(valid[variant], bias, NEG_INF)


def _attn_win_kernel(sink_ref, q_ref, kp_ref, kc_ref, kn_ref, vp_ref, vc_ref, vn_ref, u_ref, wg_ref, wup_ref,
                     bg_ref, o_ref, k_scr, v_scr, tbl_ref, a_scr):
    m = pl.program_id(1)

    @pl.when((pl.program_id(0) == 0) & (m == 0))
    def _():
        _build_win_bias(tbl_ref)

    k_scr[0:A_BLOCK, :] = kp_ref[...]
    k_scr[A_BLOCK:A_BLOCK + A_STEP, :] = kc_ref[...]
    k_scr[A_BLOCK + A_STEP:, :] = kn_ref[...]
    v_scr[0:A_BLOCK, :] = vp_ref[...]
    v_scr[A_BLOCK:A_BLOCK + A_STEP, :] = vc_ref[...]
    v_scr[A_BLOCK + A_STEP:, :] = vn_ref[...]
    span = 3 * A_BLOCK
    last_block = SEQ // A_BLOCK - 1
    piece = D_MODEL // (A_QB * A_KV_HEADS)

    def gate_piece(i):
        cols = slice(i * piece, (i + 1) * piece)
        gate = jnp.dot(u_ref[...], wg_ref[:, cols], preferred_element_type=F32) + bg_ref[:, cols]
        o_ref[:, cols] = jax.nn.sigmoid(gate)

    def sub_block(t):
        r = t * A_BLOCK
        block = m * A_QB + t
        variant = jnp.where(block == 0, 0, jnp.where(block == last_block, 2, 1))
        for g in range(A_KV_HEADS):
            k = k_scr[pl.ds(r, span), g * HEAD_DIM:(g + 1) * HEAD_DIM]
            v = v_scr[pl.ds(r, span), g * HEAD_DIM:(g + 1) * HEAD_DIM]
            heads = [g * A_GROUP + e for e in range(A_GROUP)]
            qs = jnp.concatenate(
                [q_ref[pl.ds(r, A_BLOCK), h * HEAD_DIM:(h + 1) * HEAD_DIM] for h in heads], axis=0)
            s = lax.dot_general(qs, k, (((1,), (1,)), ((), ())), preferred_element_type=F32)
            gate_piece(t * A_KV_HEADS + g)
            probs, dens = [], []
            for e, h in enumerate(heads):
                logits = s[e * A_BLOCK:(e + 1) * A_BLOCK, :] + tbl_ref[variant, h]
                sink = sink_ref[h] * LOG2E
                mx = jnp.maximum(jnp.max(logits, axis=-1, keepdims=True), sink)
                p = jnp.exp2(logits - mx)
                dens.append(jnp.sum(p, axis=-1, keepdims=True) + jnp.exp2(sink - mx))
                probs.append(p.astype(BF16))
            o = jnp.dot(jnp.concatenate(probs, axis=0), v, preferred_element_type=F32)
            for e, h in enumerate(heads):
                a_scr[pl.ds(r, A_BLOCK), h * HEAD_DIM:(h + 1) * HEAD_DIM] = (
                    o[e * A_BLOCK:(e + 1) * A_BLOCK, :] / dens[e]).astype(BF16)

    for t in range(A_QB):
        sub_block(t)
    attn = a_scr[...]
    for c in range(D_MODEL // GATE_TC):
        cols = slice(c * GATE_TC, (c + 1) * GATE_TC)
        o_ref[:, cols] = o_ref[:, cols] * jnp.dot(attn, wup_ref[:, cols], preferred_element_type=F32)


def _attn_win(qkv, sink, col_q, col_k, col_v, u, w_gate, w_up, b_gate):
    b = qkv.shape[0]
    n_steps = SEQ // A_STEP
    n_blocks = SEQ // A_BLOCK
    kcol, vcol = col_k // A_KV_W, col_v // A_KV_W

    def edge(col, shift):
        def index_map(bi, m):
            blk = jnp.clip(m * A_QB + shift, 0, n_blocks - 1)
            return (bi, blk, col)
        return pl.BlockSpec((None, A_BLOCK, A_KV_W), index_map)

    def centre(col):
        return pl.BlockSpec((None, A_STEP, A_KV_W), lambda bi, m: (bi, m, col))

    return pl.pallas_call(
        _attn_win_kernel,
        grid=(b, n_steps),
        in_specs=[
            pl.BlockSpec(memory_space=pltpu.SMEM),
            pl.BlockSpec((None, A_STEP, A_Q_W), lambda bi, m: (bi, m, col_q // A_Q_W)),
            edge(kcol, -1), centre(kcol), edge(kcol, A_QB),
            edge(vcol, -1), centre(vcol), edge(vcol, A_QB),
            pl.BlockSpec((None, A_STEP, D_MODEL), lambda bi, m: (bi, m, 0)),
            pl.BlockSpec((D_MODEL, D_MODEL), lambda bi, m: (0, 0)),
            pl.BlockSpec((A_Q_W, D_MODEL), lambda bi, m: (0, 0)),
            pl.BlockSpec((1, D_MODEL), lambda bi, m: (0, 0)),
        ],
        out_specs=pl.BlockSpec((None, A_STEP, D_MODEL), lambda bi, m: (bi, m, 0)),
        out_shape=jax.ShapeDtypeStruct((b, SEQ, D_MODEL), F32),
        scratch_shapes=[pltpu.VMEM((A_STEP + 2 * A_BLOCK, A_KV_W), BF16),
                        pltpu.VMEM((A_STEP + 2 * A_BLOCK, A_KV_W), BF16),
                        pltpu.VMEM((3, A_Q_HEADS, A_BLOCK, 3 * A_BLOCK), F32),
                        pltpu.VMEM((A_STEP, A_Q_W), BF16)],
        compiler_params=_compiler_params(2),
        name="attn_win",
    )(sink, qkv, qkv, qkv, qkv, qkv, qkv, qkv, u, w_gate, w_up, b_gate)


def _nbr_row_start(row):
    return min(max(row - B_WIN_R // 2, 0), GRID_ROWS - B_WIN_R)


def _nbr_key_row0(pair):
    return min(_nbr_row_start(pair * B_PAIR_ROWS), GRID_ROWS - B_KEY_ROWS)


def _build_nbr_bias(rpb_ref, tbl_ref, toep_l, toep_r):
    lanes = 2 * GRID_W
    qc = lax.broadcasted_iota(jnp.int32, (GRID_W, lanes), 0)
    lane = lax.broadcasted_iota(jnp.int32, (GRID_W, lanes), 1)
    col_start = jnp.clip(qc - B_WIN_C // 2, 0, GRID_W - B_WIN_C)
    left = lane < GRID_W
    neg = jnp.full((GRID_W, lanes), NEG_INF, F32)
    n_rel_r, n_rel_c = 2 * B_WIN_R - 1, 2 * B_WIN_C - 1

    def per_head(h, carry):
        for half, dst in ((0, toep_l), (1, toep_r)):
            kc = lane - half * GRID_W
            rel = kc - qc + (B_WIN_C - 1)
            col_ok = (kc >= col_start) & (kc < col_start + B_WIN_C) & (left if half == 0 else ~left)
            for a in range(n_rel_r):
                t = jnp.zeros((GRID_W, lanes), F32)
                for c in range(n_rel_c):
                    t = jnp.where(rel == c, rpb_ref[h, a * n_rel_c + c] * LOG2E, t)
                dst[a] = jnp.where(col_ok, t, neg)
        for vi, pair in enumerate(B_VARIANT_PAIRS):
            row0 = _nbr_key_row0(pair)
            for ql in range(B_PAIR_ROWS):
                q_row = pair * B_PAIR_ROWS + ql
                start = _nbr_row_start(q_row)
                for mt in range(B_KEY_ROWS // 2):
                    halves = []
                    for half, src in ((0, toep_l), (1, toep_r)):
                        k_row = row0 + 2 * mt + half
                        if start <= k_row < start + B_WIN_R:
                            halves.append(src[k_row - q_row + B_WIN_R - 1])
                        else:
                            halves.append(neg)
                    tbl_ref[vi, h, ql * GRID_W:(ql + 1) * GRID_W, mt * lanes:(mt + 1) * lanes] = (
                        jnp.where(left, halves[0], halves[1]))
        return carry

    lax.fori_loop(0, B_HEADS, per_head, 0)


def _nbr_window_block(step):
    return jnp.clip(step * (B_STEP // B_WINDOW_ALIGN) - 1, 0, (SEQ - B_WINDOW_TOK) // B_WINDOW_ALIGN)


def _attn_nbr_kernel(rpb_ref, q_ref, k_ref, v_ref, u_ref, wg_ref, wup_ref, bg_ref, sa_ref, o_ref,
                     tbl_ref, toep_l, toep_r, attn_scr, sig_scr):
    step = pl.program_id(1)

    @pl.when((pl.program_id(0) == 0) & (step == 0))
    def _():
        _build_nbr_bias(rpb_ref, tbl_ref, toep_l, toep_r)

    window_row0 = _nbr_window_block(step) * (B_WINDOW_ALIGN // GRID_W)
    piece = D_MODEL // B_PAIRS_PER_STEP
    head_cols = [slice(h * HEAD_DIM, (h + 1) * HEAD_DIM) for h in range(B_HEADS)]

    def gate_piece(i):
        cols = slice(i * piece, (i + 1) * piece)
        gate = jnp.dot(u_ref[...], wg_ref[:, cols], preferred_element_type=F32) + bg_ref[:, cols]
        sig_scr[:, cols] = jax.nn.sigmoid(gate)

    for pp in range(B_PAIRS_PER_STEP):
        pair = step * B_PAIRS_PER_STEP + pp
        row0 = jnp.clip(pair * B_PAIR_ROWS - B_WIN_R // 2, 0, GRID_ROWS - B_KEY_ROWS)
        variant = jnp.where(pair < 2, pair, jnp.where(pair >= B_N_PAIRS - 2, pair - (B_N_PAIRS - 5), 2))
        k0 = pl.multiple_of((row0 - window_row0) * GRID_W, GRID_W)
        q_rows = slice(pp * B_PAIR_TOK, (pp + 1) * B_PAIR_TOK)
        scores = [
            lax.dot_general(q_ref[q_rows, cols], k_ref[0, pl.ds(k0, B_KEY_TOK), cols],
                            (((1,), (1,)), ((), ())), preferred_element_type=F32)
            for cols in head_cols]
        gate_piece(pp)
        s = jnp.concatenate(scores, axis=0)
        logits = s + tbl_ref[variant].reshape(B_HEADS * B_PAIR_TOK, B_KEY_TOK)
        mx = jnp.max(logits, axis=-1, keepdims=True)
        p = jnp.exp2(logits - mx)
        inv_den = 1.0 / jnp.sum(p, axis=-1, keepdims=True)
        p = p.astype(BF16)
        for h, cols in enumerate(head_cols):
            rows = slice(h * B_PAIR_TOK, (h + 1) * B_PAIR_TOK)
            o = jnp.dot(p[rows, :], v_ref[0, pl.ds(k0, B_KEY_TOK), cols], preferred_element_type=F32)
            attn_scr[q_rows, cols] = (o * inv_den[rows, :]).astype(BF16)

    attn = attn_scr[...]
    for c in range(D_MODEL // GATE_TC):
        cols = slice(c * GATE_TC, (c + 1) * GATE_TC)
        up = jnp.dot(attn, wup_ref[:, cols], preferred_element_type=F32)
        o_ref[:, cols] = (sa_ref[:, cols] + sig_scr[:, cols] * up).astype(BF16)


def _attn_nbr(qkv, rpb, col_q, col_k, col_v, u, w_gate, w_up, b_gate, sa):
    b = qkv.shape[0]
    assert B_WINDOW_ALIGN == (B_WIN_R // 2) * GRID_W and B_STEP % B_WINDOW_ALIGN == 0

    def window(col):
        return pl.BlockSpec((pl.Element(1), pl.Element(B_WINDOW_TOK), pl.Element(B_W)),
                            lambda bi, s: (bi, _nbr_window_block(s) * B_WINDOW_ALIGN, col))

    resident = dict(pipeline_mode=pl.Buffered(1))
    return pl.pallas_call(
        _attn_nbr_kernel,
        grid=(b, SEQ // B_STEP),
        in_specs=[
            pl.BlockSpec(memory_space=pltpu.SMEM),
            pl.BlockSpec((None, B_STEP, B_W), lambda bi, s: (bi, s, col_q // B_W)),
            window(col_k), window(col_v),
            pl.BlockSpec((None, B_STEP, D_MODEL), lambda bi, s: (bi, s, 0)),
            pl.BlockSpec((D_MODEL, D_MODEL), lambda bi, s: (0, 1), **resident),
            pl.BlockSpec((B_W, D_MODEL), lambda bi, s: (0, 0), **resident),
            pl.BlockSpec((1, D_MODEL), lambda bi, s: (0, 1)),
            pl.BlockSpec((None, B_STEP, D_MODEL), lambda bi, s: (bi, s, 0)),
        ],
        out_specs=pl.BlockSpec((None, B_STEP, D_MODEL), lambda bi, s: (bi, s, 0)),
        out_shape=jax.ShapeDtypeStruct((b, SEQ, D_MODEL), BF16),
        scratch_shapes=[
            pltpu.VMEM((len(B_VARIANT_PAIRS), B_HEADS, B_PAIR_TOK, B_KEY_TOK), F32),
            pltpu.VMEM((2 * B_WIN_R - 1, GRID_W, 2 * GRID_W), F32),
            pltpu.VMEM((2 * B_WIN_R - 1, GRID_W, 2 * GRID_W), F32),
            pltpu.VMEM((B_STEP, B_W), BF16),
            pltpu.VMEM((B_STEP, D_MODEL), F32),
        ],
        compiler_params=_compiler_params(2),
        name="attn_nbr",
    )(rpb, qkv, qkv, qkv, u, w_gate, w_up, b_gate, sa)


def _out_proj_kernel(m_ref, h_ref, w_ref, post_g_ref, o_ref):
    gain = post_g_ref[...]
    for r in range(OUT_TM // OUT_ROWS):
        y = jnp.dot(m_ref[r * OUT_ROWS:(r + 1) * OUT_ROWS, :], w_ref[...], preferred_element_type=F32)
        for c in range(OUT_ROWS // NORM_ROWS):
            rs = slice(r * OUT_ROWS + c * NORM_ROWS, r * OUT_ROWS + (c + 1) * NORM_ROWS)
            o_ref[rs, :] = h_ref[rs, :] + _rms_scale(y[c * NORM_ROWS:(c + 1) * NORM_ROWS, :]) * gain


def _out_proj(m, h, w_out, post_g):
    t = m.shape[0]
    return pl.pallas_call(
        _out_proj_kernel,
        grid=(t // OUT_TM,),
        in_specs=[
            pl.BlockSpec((OUT_TM, D_MODEL), lambda i: (i, 0)),
            pl.BlockSpec((OUT_TM, D_MODEL), lambda i: (i, 0)),
            pl.BlockSpec((D_MODEL, D_MODEL), lambda i: (0, 0), pipeline_mode=pl.Buffered(1)),
            pl.BlockSpec((1, D_MODEL), lambda i: (0, 0)),
        ],
        out_specs=pl.BlockSpec((OUT_TM, D_MODEL), lambda i: (i, 0)),
        out_shape=jax.ShapeDtypeStruct((t, D_MODEL), F32),
        compiler_params=_compiler_params(1),
        name="out_proj",
    )(m, h, w_out, post_g)


def _cast_mixer_w_in(src_ref, qkv_ref, gate_ref):
    for name in QKV_ORDER:
        lo, hi = QKV_SRC[name]
        blk = src_ref[:, lo:hi]
        if name in ("qa", "qb"):
            blk = blk * Q_PRESCALE
        qkv_ref[:, QKV_COL[name]:QKV_COL[name] + hi - lo] = blk.astype(BF16)
    gate_ref[...] = src_ref[:, QKV_W:].astype(BF16)


def kernel(x, ffn1_pre_g, ffn1_w_in, ffn1_w_out, ffn1_post_g, mix_pre_g, w_in, b_gate, sink_a, rpb_b,
           w_up_a, w_up_b, w_out, mix_post_g, ffn2_pre_g, ffn2_w_in, ffn2_w_out, ffn2_post_g):
    batch, seq, d = x.shape
    assert (seq, d) == (SEQ, D_MODEL)
    depth = ffn1_w_in.shape[0]
    tokens = batch * seq
    n_tiles, nf = tokens // FFN_TM, D_FF // FFN_FC
    h = x.reshape(tokens, d)

    def row(v):
        return v.reshape(1, -1).astype(F32)

    for l in range(depth):
        jobs = (_tile_rows_job(w_in[l], n_tiles, nf, _cast_mixer_w_in, (QKV_W, 2 * D_MODEL)),)
        h, w_qkv, w_gate = _ffn(h, row(ffn1_pre_g[l]), ffn1_w_in[l].astype(BF16), ffn1_w_out[l].astype(BF16),
                                row(ffn1_post_g[l]), jobs)
        later = (ffn2_w_in[l], ffn2_w_out[l], w_out[l], w_up_a[l], w_up_b[l])
        u, qkv, w2_in, w2_out, w_o, w_ua, w_ub = _qkv_proj(
            h, row(mix_pre_g[l]), w_qkv, lambda n_steps: tuple(_row_slab_job(w, n_steps) for w in later))
        qkv = qkv.reshape(batch, seq, QKV_W)
        u = u.reshape(batch, seq, d)
        sa = _attn_win(qkv, sink_a[l].astype(F32), QKV_COL["qa"], QKV_COL["ka"], QKV_COL["va"],
                       u, w_gate, w_ua, row(b_gate[l]))
        mixed = _attn_nbr(qkv, rpb_b[l].astype(F32).reshape(B_HEADS, -1),
                          QKV_COL["qb"], QKV_COL["kb"], QKV_COL["vb"],
                          u, w_gate, w_ub, row(b_gate[l]), sa)
        h = _out_proj(mixed.reshape(tokens, d), h, w_o, row(mix_post_g[l]))
        h, = _ffn(h, row(ffn2_pre_g[l]), w2_in, w2_out, row(ffn2_post_g[l]))
    return h.reshape(batch, seq, d)
```

```python
import functools
import math
from typing import Callable, NamedTuple

import jax
import jax.numpy as jnp
import numpy as np
from jax import lax
from jax.experimental import pallas as pl
from jax.experimental.pallas import tpu as pltpu

D_MODEL = 2048
SEQ = 4096
HEAD_DIM = 128
A_Q_HEADS = 8
A_KV_HEADS = 2
A_GROUP = A_Q_HEADS // A_KV_HEADS
A_WINDOW = 128
A_BLOCK = 128
B_HEADS = 8
GRID_W = 64
GRID_ROWS = SEQ // GRID_W
B_WIN_R = 8
B_WIN_C = 16
D_FF = 5632
MACARON_W = 0.5
RMS_EPS = 1e-6
NEG_INF = -1e30
SCALE = HEAD_DIM ** -0.5
LOG2E = math.log2(math.e)
Q_PRESCALE = SCALE * LOG2E

A_Q_W = A_Q_HEADS * HEAD_DIM
A_KV_W = A_KV_HEADS * HEAD_DIM
B_W = B_HEADS * HEAD_DIM
QKV_W = A_Q_W + 2 * A_KV_W + 3 * B_W

_cuts = np.cumsum([0, A_Q_W, A_KV_W, A_KV_W, B_W, B_W, B_W]).tolist()
QKV_SRC = {name: (_cuts[i], _cuts[i + 1]) for i, name in enumerate(("qa", "ka", "va", "qb", "kb", "vb"))}
QKV_ORDER = ("qa", "qb", "kb", "vb", "ka", "va")
QKV_COL = {}
for _name in QKV_ORDER:
    QKV_COL[_name] = sum(QKV_SRC[n][1] - QKV_SRC[n][0] for n in QKV_ORDER[:QKV_ORDER.index(_name)])

F32 = jnp.float32
BF16 = jnp.bfloat16

VMEM_LIMIT_BYTES = 61 * 1024 * 1024

FFN_TM = 1024
FFN_FC = 512
FFN_ROWS = 256
HEAD_FC = 256
QKV_TM = 512
QKV_ROWS = 256
OUT_TM = 1024
OUT_ROWS = 256
NORM_ROWS = 32
CAST_ROWS = 16

GATE_TC = 512
A_QB = 4
A_STEP = A_QB * A_BLOCK
B_PAIR_ROWS = 2
B_PAIR_TOK = B_PAIR_ROWS * GRID_W
B_KEY_ROWS = 10
B_KEY_TOK = B_KEY_ROWS * GRID_W
B_PAIRS_PER_STEP = 2
B_STEP = B_PAIRS_PER_STEP * B_PAIR_TOK
B_N_PAIRS = GRID_ROWS // B_PAIR_ROWS
B_WINDOW_ALIGN = 256
B_WINDOW_TOK = B_STEP + 2 * B_WINDOW_ALIGN
B_VARIANT_PAIRS = (0, 1, 2, B_N_PAIRS - 2, B_N_PAIRS - 1)


def _alibi_slopes(n_heads):
    return [2.0 ** (-8.0 * (i + 1) / n_heads) for i in range(n_heads)]


def _compiler_params(n_axes):
    return pltpu.CompilerParams(dimension_semantics=("arbitrary",) * n_axes,
                                vmem_limit_bytes=VMEM_LIMIT_BYTES)


def _rms_scale(x):
    return x * lax.rsqrt(jnp.mean(x * x, axis=-1, keepdims=True) + RMS_EPS)


class CastJob(NamedTuple):
    src: jax.Array
    in_spec: pl.BlockSpec
    out_specs: tuple
    out_shapes: tuple
    body: Callable


def _cast_block(src_ref, dst_ref):
    dst_ref[...] = src_ref[...].astype(BF16)


def _tile_rows_job(src, n_tiles, n_steps, body=_cast_block, out_cols=None):
    rows, cols = src.shape
    per_tile = rows // (n_tiles * CAST_ROWS)
    assert per_tile * n_tiles * CAST_ROWS == rows and per_tile <= n_steps
    index_map = lambda i, j: (i * per_tile + jnp.minimum(j, per_tile - 1), 0)
    out_cols = (cols,) if out_cols is None else out_cols
    return CastJob(src, pl.BlockSpec((CAST_ROWS, cols), index_map),
                   tuple(pl.BlockSpec((CAST_ROWS, c), index_map) for c in out_cols),
                   tuple(jax.ShapeDtypeStruct((rows, c), BF16) for c in out_cols), body)


def _row_slab_job(src, n_steps):
    rows, cols = src.shape
    slab = rows // n_steps
    assert slab * n_steps == rows and slab % CAST_ROWS == 0
    spec = pl.BlockSpec((slab, cols), lambda i: (i, 0))
    return CastJob(src, spec, (spec,), (jax.ShapeDtypeStruct((rows, cols), BF16),), _cast_block)


def _run_jobs(jobs, src_refs, dst_refs):
    first = 0
    for job, src_ref in zip(jobs, src_refs):
        job.body(src_ref, *dst_refs[first:first + len(job.out_specs)])
        first += len(job.out_specs)


def _ffn_kernel(*refs, jobs, has_head):
    x_ref, pre_g_ref, wg_ref, wu_ref, wo_ref, post_g_ref = refs[:6]
    n_in = 6 + has_head
    job_srcs = refs[n_in:n_in + len(jobs)]
    o_ref = refs[n_in + len(jobs)]
    n_scratch = 1 + has_head
    job_dsts = refs[n_in + 1 + len(jobs):-n_scratch]
    n_ref = refs[-n_scratch]
    i = pl.program_id(0)
    j = pl.program_id(1)
    last = pl.num_programs(1) - 1
    active = (i > 0) if has_head else True
    row_chunks = [slice(r * FFN_ROWS, (r + 1) * FFN_ROWS) for r in range(FFN_TM // FFN_ROWS)]

    def swiglu(n):
        g = jnp.dot(n, wg_ref[...], preferred_element_type=F32)
        u = jnp.dot(n, wu_ref[...], preferred_element_type=F32)
        a = (g * jax.nn.sigmoid(g) * u).astype(BF16)
        return jnp.dot(a, wo_ref[...], preferred_element_type=F32)

    def run_jobs():
        _run_jobs(jobs, job_srcs, job_dsts)

    @pl.when((j == 0) & active)
    def _():
        gain = pre_g_ref[...]
        for rows in row_chunks:
            for c in range(rows.start, rows.stop, NORM_ROWS):
                rs = slice(c, c + NORM_ROWS)
                n_ref[rs, :] = (_rms_scale(x_ref[rs, :]) * gain).astype(BF16)
            o_ref[rows, :] = swiglu(n_ref[rows, :])
        run_jobs()

    @pl.when((j > 0) & (j < last) & active)
    def _():
        o_ref[...] += swiglu(n_ref[...])
        run_jobs()

    if has_head:
        head_ref, copy_sem = refs[6], refs[-1]

        @pl.when(i == 0)
        def _():
            run_jobs()

        @pl.when((i == 0) & (j == 0))
        def _():
            copy = pltpu.make_async_copy(head_ref, o_ref, copy_sem)
            copy.start()
            copy.wait()

    @pl.when((j == last) & active)
    def _():
        gain = MACARON_W * post_g_ref[...]
        for rows in row_chunks:
            f = o_ref[rows, :] + swiglu(n_ref[rows, :])
            for c in range(0, FFN_ROWS, NORM_ROWS):
                rs = slice(rows.start + c, rows.start + c + NORM_ROWS)
                o_ref[rs, :] = x_ref[rs, :] + _rms_scale(f[c:c + NORM_ROWS, :]) * gain
        run_jobs()


def _ffn_head_kernel(x_ref, pre_g_ref, wg_ref, wu_ref, wo_ref, post_g_ref,
                     o_ref, wg_bf_ref, wu_bf_ref, wo_bf_ref, n_ref):
    j = pl.program_id(0)
    row_chunks = [slice(r, r + NORM_ROWS) for r in range(0, FFN_TM, NORM_ROWS)]

    @pl.when(j == 0)
    def _():
        gain = pre_g_ref[...]
        for rs in row_chunks:
            n_ref[rs, :] = (_rms_scale(x_ref[rs, :]) * gain).astype(BF16)
        o_ref[...] = jnp.zeros_like(o_ref)

    wg, wu, wo = (ref[...].astype(BF16) for ref in (wg_ref, wu_ref, wo_ref))
    wg_bf_ref[...] = wg
    wu_bf_ref[...] = wu
    wo_bf_ref[...] = wo
    n = n_ref[...]
    g = jnp.dot(n, wg, preferred_element_type=F32)
    u = jnp.dot(n, wu, preferred_element_type=F32)
    a = (g * jax.nn.sigmoid(g) * u).astype(BF16)
    o_ref[...] += jnp.dot(a, wo, preferred_element_type=F32)

    @pl.when(j == pl.num_programs(0) - 1)
    def _():
        gain = MACARON_W * post_g_ref[...]
        for rs in row_chunks:
            o_ref[rs, :] = x_ref[rs, :] + _rms_scale(o_ref[rs, :]) * gain


def _ffn_head(x, pre_g, w_in, w_out, post_g):
    n_steps = D_FF // HEAD_FC
    return pl.pallas_call(
        _ffn_head_kernel,
        grid=(n_steps,),
        in_specs=[
            pl.BlockSpec((FFN_TM, D_MODEL), lambda j: (0, 0)),
            pl.BlockSpec((1, D_MODEL), lambda j: (0, 0)),
            pl.BlockSpec((D_MODEL, HEAD_FC), lambda j: (0, j)),
            pl.BlockSpec((D_MODEL, HEAD_FC), lambda j: (0, j + n_steps)),
            pl.BlockSpec((HEAD_FC, D_MODEL), lambda j: (j, 0)),
            pl.BlockSpec((1, D_MODEL), lambda j: (0, 0)),
        ],
        out_specs=[
            pl.BlockSpec((FFN_TM, D_MODEL), lambda j: (0, 0)),
            pl.BlockSpec((D_MODEL, HEAD_FC), lambda j: (0, j)),
            pl.BlockSpec((D_MODEL, HEAD_FC), lambda j: (0, j)),
            pl.BlockSpec((HEAD_FC, D_MODEL), lambda j: (j, 0)),
        ],
        out_shape=[jax.ShapeDtypeStruct((FFN_TM, D_MODEL), F32),
                   jax.ShapeDtypeStruct((D_MODEL, D_FF), BF16),
                   jax.ShapeDtypeStruct((D_MODEL, D_FF), BF16),
                   jax.ShapeDtypeStruct((D_FF, D_MODEL), BF16)],
        scratch_shapes=[pltpu.VMEM((FFN_TM, D_MODEL), BF16)],
        compiler_params=_compiler_params(1),
        name="ffn_head",
    )(x, pre_g, w_in, w_in, w_out, post_g)


def _ffn(x, pre_g, w_gate, w_up, up_block0, w_out, post_g, jobs=(), head=None):
    t = x.shape[0]
    nf = D_FF // FFN_FC
    assert nf >= 2
    has_head = head is not None
    chunk = (lambda i, j: jnp.where(i == 0, 0, j)) if has_head else (lambda i, j: j)
    out_specs = [pl.BlockSpec((FFN_TM, D_MODEL), lambda i, j: (i, 0))]
    out_shapes = [jax.ShapeDtypeStruct((t, D_MODEL), F32)]
    for job in jobs:
        out_specs.extend(job.out_specs)
        out_shapes.extend(job.out_shapes)
    return pl.pallas_call(
        functools.partial(_ffn_kernel, jobs=jobs, has_head=has_head),
        grid=(t // FFN_TM, nf),
        in_specs=[
            pl.BlockSpec((FFN_TM, D_MODEL), lambda i, j: (i, 0)),
            pl.BlockSpec((1, D_MODEL), lambda i, j: (0, 0)),
            pl.BlockSpec((D_MODEL, FFN_FC), lambda i, j: (0, chunk(i, j))),
            pl.BlockSpec((D_MODEL, FFN_FC), lambda i, j: (0, up_block0 + chunk(i, j))),
            pl.BlockSpec((FFN_FC, D_MODEL), lambda i, j: (chunk(i, j), 0)),
            pl.BlockSpec((1, D_MODEL), lambda i, j: (0, 0)),
        ] + ([pl.BlockSpec(memory_space=pl.ANY)] if has_head else []) + [job.in_spec for job in jobs],
        out_specs=out_specs,
        out_shape=out_shapes,
        scratch_shapes=[pltpu.VMEM((FFN_TM, D_MODEL), BF16)] + ([pltpu.SemaphoreType.DMA(())] if has_head else []),
        compiler_params=_compiler_params(2),
        name="ffn",
    )(x, pre_g, w_gate, w_up, w_out, post_g, *([head] if has_head else []), *[job.src for job in jobs])


def _qkv_kernel(*refs, jobs):
    h_ref, g_ref, w_ref = refs[:3]
    job_srcs = refs[3:3 + len(jobs)]
    u_ref, o_ref = refs[3 + len(jobs):5 + len(jobs)]
    gain = g_ref[...]
    for r in range(0, QKV_TM, QKV_ROWS):
        for c in range(r, r + QKV_ROWS, NORM_ROWS):
            rs = slice(c, c + NORM_ROWS)
            u_ref[rs, :] = (_rms_scale(h_ref[rs, :]) * gain).astype(BF16)
        rows = slice(r, r + QKV_ROWS)
        o_ref[rows, :] = jnp.dot(u_ref[rows, :], w_ref[...], preferred_element_type=F32).astype(BF16)
    _run_jobs(jobs, job_srcs, refs[5 + len(jobs):])


def _qkv_proj(h, gain, w_qkv, make_jobs):
    t = h.shape[0]
    n_steps = t // QKV_TM
    jobs = make_jobs(n_steps)
    return pl.pallas_call(
        functools.partial(_qkv_kernel, jobs=jobs),
        grid=(n_steps,),
        in_specs=[
            pl.BlockSpec((QKV_TM, D_MODEL), lambda i: (i, 0)),
            pl.BlockSpec((1, D_MODEL), lambda i: (0, 0)),
            pl.BlockSpec((D_MODEL, QKV_W), lambda i: (0, 0), pipeline_mode=pl.Buffered(1)),
        ] + [job.in_spec for job in jobs],
        out_specs=[
            pl.BlockSpec((QKV_TM, D_MODEL), lambda i: (i, 0)),
            pl.BlockSpec((QKV_TM, QKV_W), lambda i: (i, 0)),
        ] + [spec for job in jobs for spec in job.out_specs],
        out_shape=[jax.ShapeDtypeStruct((t, D_MODEL), BF16),
                   jax.ShapeDtypeStruct((t, QKV_W), BF16)] + [sh for job in jobs for sh in job.out_shapes],
        compiler_params=_compiler_params(1),
        name="qkv_proj",
    )(h, gain, w_qkv, *[job.src for job in jobs])


def _build_win_bias(tbl_ref):
    span = 3 * A_BLOCK
    qi = lax.broadcasted_iota(jnp.int32, (A_BLOCK, span), 0)
    kj = lax.broadcasted_iota(jnp.int32, (A_BLOCK, span), 1)
    absd_i = jnp.abs(qi + A_WINDOW - kj)
    absd = absd_i.astype(F32)
    in_window = absd_i <= A_WINDOW
    valid = (in_window & (kj >= A_BLOCK), in_window, in_window & (kj < 2 * A_BLOCK))
    for h, slope in enumerate(_alibi_slopes(A_Q_HEADS)):
        bias = (-slope * LOG2E) * absd
        for variant in range(3):
            tbl_ref[variant, h] = jnp.where(valid[variant], bias, NEG_INF)


def _attn_win_kernel(sink_ref, q_ref, kp_ref, kc_ref, kn_ref, vp_ref, vc_ref, vn_ref, u_ref, wg_ref, wup_ref,
                     bg_ref, o_ref, k_scr, v_scr, tbl_ref, a_scr):
    m = pl.program_id(1)

    @pl.when((pl.program_id(0) == 0) & (m == 0))
    def _():
        _build_win_bias(tbl_ref)

    k_scr[0:A_BLOCK, :] = kp_ref[...]
    k_scr[A_BLOCK:A_BLOCK + A_STEP, :] = kc_ref[...]
    k_scr[A_BLOCK + A_STEP:, :] = kn_ref[...]
    v_scr[0:A_BLOCK, :] = vp_ref[...]
    v_scr[A_BLOCK:A_BLOCK + A_STEP, :] = vc_ref[...]
    v_scr[A_BLOCK + A_STEP:, :] = vn_ref[...]
    span = 3 * A_BLOCK
    last_block = SEQ // A_BLOCK - 1
    piece = D_MODEL // (A_QB * A_KV_HEADS)

    def gate_piece(i):
        cols = slice(i * piece, (i + 1) * piece)
        gate = jnp.dot(u_ref[...], wg_ref[:, cols], preferred_element_type=F32) + bg_ref[:, cols]
        o_ref[:, cols] = jax.nn.sigmoid(gate)

    def sub_block(t):
        r = t * A_BLOCK
        block = m * A_QB + t
        variant = jnp.where(block == 0, 0, jnp.where(block == last_block, 2, 1))
        for g in range(A_KV_HEADS):
            k = k_scr[pl.ds(r, span), g * HEAD_DIM:(g + 1) * HEAD_DIM]
            v = v_scr[pl.ds(r, span), g * HEAD_DIM:(g + 1) * HEAD_DIM]
            heads = [g * A_GROUP + e for e in range(A_GROUP)]
            qs = jnp.concatenate(
                [q_ref[pl.ds(r, A_BLOCK), h * HEAD_DIM:(h + 1) * HEAD_DIM] for h in heads], axis=0)
            s = lax.dot_general(qs, k, (((1,), (1,)), ((), ())), preferred_element_type=F32)
            gate_piece(t * A_KV_HEADS + g)
            probs, dens = [], []
            for e, h in enumerate(heads):
                logits = s[e * A_BLOCK:(e + 1) * A_BLOCK, :] + tbl_ref[variant, h]
                sink = sink_ref[h] * LOG2E
                mx = jnp.maximum(jnp.max(logits, axis=-1, keepdims=True), sink)
                p = jnp.exp2(logits - mx)
                dens.append(jnp.sum(p, axis=-1, keepdims=True) + jnp.exp2(sink - mx))
                probs.append(p.astype(BF16))
            o = jnp.dot(jnp.concatenate(probs, axis=0), v, preferred_element_type=F32)
            for e, h in enumerate(heads):
                a_scr[pl.ds(r, A_BLOCK), h * HEAD_DIM:(h + 1) * HEAD_DIM] = (
                    o[e * A_BLOCK:(e + 1) * A_BLOCK, :] / dens[e]).astype(BF16)

    for t in range(A_QB):
        sub_block(t)
    attn = a_scr[...]
    for c in range(D_MODEL // GATE_TC):
        cols = slice(c * GATE_TC, (c + 1) * GATE_TC)
        o_ref[:, cols] = o_ref[:, cols] * jnp.dot(attn, wup_ref[:, cols], preferred_element_type=F32)


def _attn_win(qkv, sink, col_q, col_k, col_v, u, w_gate, w_up, b_gate):
    b = qkv.shape[0]
    n_steps = SEQ // A_STEP
    n_blocks = SEQ // A_BLOCK
    kcol, vcol = col_k // A_KV_W, col_v // A_KV_W

    def edge(col, shift):
        def index_map(bi, m):
            blk = jnp.clip(m * A_QB + shift, 0, n_blocks - 1)
            return (bi, blk, col)
        return pl.BlockSpec((None, A_BLOCK, A_KV_W), index_map)

    def centre(col):
        return pl.BlockSpec((None, A_STEP, A_KV_W), lambda bi, m: (bi, m, col))

    return pl.pallas_call(
        _attn_win_kernel,
        grid=(b, n_steps),
        in_specs=[
            pl.BlockSpec(memory_space=pltpu.SMEM),
            pl.BlockSpec((None, A_STEP, A_Q_W), lambda bi, m: (bi, m, col_q // A_Q_W)),
            edge(kcol, -1), centre(kcol), edge(kcol, A_QB),
            edge(vcol, -1), centre(vcol), edge(vcol, A_QB),
            pl.BlockSpec((None, A_STEP, D_MODEL), lambda bi, m: (bi, m, 0)),
            pl.BlockSpec((D_MODEL, D_MODEL), lambda bi, m: (0, 0)),
            pl.BlockSpec((A_Q_W, D_MODEL), lambda bi, m: (0, 0)),
            pl.BlockSpec((1, D_MODEL), lambda bi, m: (0, 0)),
        ],
        out_specs=pl.BlockSpec((None, A_STEP, D_MODEL), lambda bi, m: (bi, m, 0)),
        out_shape=jax.ShapeDtypeStruct((b, SEQ, D_MODEL), F32),
        scratch_shapes=[pltpu.VMEM((A_STEP + 2 * A_BLOCK, A_KV_W), BF16),
                        pltpu.VMEM((A_STEP + 2 * A_BLOCK, A_KV_W), BF16),
                        pltpu.VMEM((3, A_Q_HEADS, A_BLOCK, 3 * A_BLOCK), F32),
                        pltpu.VMEM((A_STEP, A_Q_W), BF16)],
        compiler_params=_compiler_params(2),
        name="attn_win",
    )(sink, qkv, qkv, qkv, qkv, qkv, qkv, qkv, u, w_gate, w_up, b_gate)


def _nbr_row_start(row):
    return min(max(row - B_WIN_R // 2, 0), GRID_ROWS - B_WIN_R)


def _nbr_key_row0(pair):
    return min(_nbr_row_start(pair * B_PAIR_ROWS), GRID_ROWS - B_KEY_ROWS)


def _build_nbr_bias(rpb_ref, tbl_ref, toep_l, toep_r):
    lanes = 2 * GRID_W
    qc = lax.broadcasted_iota(jnp.int32, (GRID_W, lanes), 0)
    lane = lax.broadcasted_iota(jnp.int32, (GRID_W, lanes), 1)
    col_start = jnp.clip(qc - B_WIN_C // 2, 0, GRID_W - B_WIN_C)
    left = lane < GRID_W
    neg = jnp.full((GRID_W, lanes), NEG_INF, F32)
    n_rel_r, n_rel_c = 2 * B_WIN_R - 1, 2 * B_WIN_C - 1

    def per_head(h, carry):
        for half, dst in ((0, toep_l), (1, toep_r)):
            kc = lane - half * GRID_W
            rel = kc - qc + (B_WIN_C - 1)
            col_ok = (kc >= col_start) & (kc < col_start + B_WIN_C) & (left if half == 0 else ~left)
            for a in range(n_rel_r):
                t = jnp.zeros((GRID_W, lanes), F32)
                for c in range(n_rel_c):
                    t = jnp.where(rel == c, rpb_ref[h, a * n_rel_c + c] * LOG2E, t)
                dst[a] = jnp.where(col_ok, t, neg)
        for vi, pair in enumerate(B_VARIANT_PAIRS):
            row0 = _nbr_key_row0(pair)
            for ql in range(B_PAIR_ROWS):
                q_row = pair * B_PAIR_ROWS + ql
                start = _nbr_row_start(q_row)
                for mt in range(B_KEY_ROWS // 2):
                    halves = []
                    for half, src in ((0, toep_l), (1, toep_r)):
                        k_row = row0 + 2 * mt + half
                        if start <= k_row < start + B_WIN_R:
                            halves.append(src[k_row - q_row + B_WIN_R - 1])
                        else:
                            halves.append(neg)
                    tbl_ref[vi, h, ql * GRID_W:(ql + 1) * GRID_W, mt * lanes:(mt + 1) * lanes] = (
                        jnp.where(left, halves[0], halves[1]))
        return carry

    lax.fori_loop(0, B_HEADS, per_head, 0)


def _nbr_window_block(step):
    return jnp.clip(step * (B_STEP // B_WINDOW_ALIGN) - 1, 0, (SEQ - B_WINDOW_TOK) // B_WINDOW_ALIGN)


def _attn_nbr_kernel(rpb_ref, q_ref, k_ref, v_ref, u_ref, wg_ref, wup_ref, bg_ref, sa_ref, o_ref,
                     tbl_ref, toep_l, toep_r, attn_scr, sig_scr):
    step = pl.program_id(1)

    @pl.when((pl.program_id(0) == 0) & (step == 0))
    def _():
        _build_nbr_bias(rpb_ref, tbl_ref, toep_l, toep_r)

    window_row0 = _nbr_window_block(step) * (B_WINDOW_ALIGN // GRID_W)
    piece = D_MODEL // B_PAIRS_PER_STEP
    head_cols = [slice(h * HEAD_DIM, (h + 1) * HEAD_DIM) for h in range(B_HEADS)]

    def gate_piece(i):
        cols = slice(i * piece, (i + 1) * piece)
        gate = jnp.dot(u_ref[...], wg_ref[:, cols], preferred_element_type=F32) + bg_ref[:, cols]
        sig_scr[:, cols] = jax.nn.sigmoid(gate)

    for pp in range(B_PAIRS_PER_STEP):
        pair = step * B_PAIRS_PER_STEP + pp
        row0 = jnp.clip(pair * B_PAIR_ROWS - B_WIN_R // 2, 0, GRID_ROWS - B_KEY_ROWS)
        variant = jnp.where(pair < 2, pair, jnp.where(pair >= B_N_PAIRS - 2, pair - (B_N_PAIRS - 5), 2))
        k0 = pl.multiple_of((row0 - window_row0) * GRID_W, GRID_W)
        q_rows = slice(pp * B_PAIR_TOK, (pp + 1) * B_PAIR_TOK)
        scores = [
            lax.dot_general(q_ref[q_rows, cols], k_ref[0, pl.ds(k0, B_KEY_TOK), cols],
                            (((1,), (1,)), ((), ())), preferred_element_type=F32)
            for cols in head_cols]
        gate_piece(pp)
        s = jnp.concatenate(scores, axis=0)
        logits = s + tbl_ref[variant].reshape(B_HEADS * B_PAIR_TOK, B_KEY_TOK)
        mx = jnp.max(logits, axis=-1, keepdims=True)
        p = jnp.exp2(logits - mx)
        inv_den = 1.0 / jnp.sum(p, axis=-1, keepdims=True)
        p = p.astype(BF16)
        for h, cols in enumerate(head_cols):
            rows = slice(h * B_PAIR_TOK, (h + 1) * B_PAIR_TOK)
            o = jnp.dot(p[rows, :], v_ref[0, pl.ds(k0, B_KEY_TOK), cols], preferred_element_type=F32)
            attn_scr[q_rows, cols] = (o * inv_den[rows, :]).astype(BF16)

    attn = attn_scr[...]
    for c in range(D_MODEL // GATE_TC):
        cols = slice(c * GATE_TC, (c + 1) * GATE_TC)
        up = jnp.dot(attn, wup_ref[:, cols], preferred_element_type=F32)
        o_ref[:, cols] = (sa_ref[:, cols] + sig_scr[:, cols] * up).astype(BF16)


def _attn_nbr(qkv, rpb, col_q, col_k, col_v, u, w_gate, w_up, b_gate, sa):
    b = qkv.shape[0]
    assert B_WINDOW_ALIGN == (B_WIN_R // 2) * GRID_W and B_STEP % B_WINDOW_ALIGN == 0

    def window(col):
        return pl.BlockSpec((pl.Element(1), pl.Element(B_WINDOW_TOK), pl.Element(B_W)),
                            lambda bi, s: (bi, _nbr_window_block(s) * B_WINDOW_ALIGN, col))

    resident = dict(pipeline_mode=pl.Buffered(1))
    return pl.pallas_call(
        _attn_nbr_kernel,
        grid=(b, SEQ // B_STEP),
        in_specs=[
            pl.BlockSpec(memory_space=pltpu.SMEM),
            pl.BlockSpec((None, B_STEP, B_W), lambda bi, s: (bi, s, col_q // B_W)),
            window(col_k), window(col_v),
            pl.BlockSpec((None, B_STEP, D_MODEL), lambda bi, s: (bi, s, 0)),
            pl.BlockSpec((D_MODEL, D_MODEL), lambda bi, s: (0, 1), **resident),
            pl.BlockSpec((B_W, D_MODEL), lambda bi, s: (0, 0), **resident),
            pl.BlockSpec((1, D_MODEL), lambda bi, s: (0, 1)),
            pl.BlockSpec((None, B_STEP, D_MODEL), lambda bi, s: (bi, s, 0)),
        ],
        out_specs=pl.BlockSpec((None, B_STEP, D_MODEL), lambda bi, s: (bi, s, 0)),
        out_shape=jax.ShapeDtypeStruct((b, SEQ, D_MODEL), BF16),
        scratch_shapes=[
            pltpu.VMEM((len(B_VARIANT_PAIRS), B_HEADS, B_PAIR_TOK, B_KEY_TOK), F32),
            pltpu.VMEM((2 * B_WIN_R - 1, GRID_W, 2 * GRID_W), F32),
            pltpu.VMEM((2 * B_WIN_R - 1, GRID_W, 2 * GRID_W), F32),
            pltpu.VMEM((B_STEP, B_W), BF16),
            pltpu.VMEM((B_STEP, D_MODEL), F32),
        ],
        compiler_params=_compiler_params(2),
        name="attn_nbr",
    )(rpb, qkv, qkv, qkv, u, w_gate, w_up, b_gate, sa)


def _out_proj_kernel(m_ref, h_ref, w_ref, post_g_ref, o_ref):
    gain = post_g_ref[...]
    for r in range(OUT_TM // OUT_ROWS):
        y = jnp.dot(m_ref[r * OUT_ROWS:(r + 1) * OUT_ROWS, :], w_ref[...], preferred_element_type=F32)
        for c in range(OUT_ROWS // NORM_ROWS):
            rs = slice(r * OUT_ROWS + c * NORM_ROWS, r * OUT_ROWS + (c + 1) * NORM_ROWS)
            o_ref[rs, :] = h_ref[rs, :] + _rms_scale(y[c * NORM_ROWS:(c + 1) * NORM_ROWS, :]) * gain


def _out_proj(m, h, w_out, post_g):
    t = m.shape[0]
    return pl.pallas_call(
        _out_proj_kernel,
        grid=(t // OUT_TM,),
        in_specs=[
            pl.BlockSpec((OUT_TM, D_MODEL), lambda i: (i, 0)),
            pl.BlockSpec((OUT_TM, D_MODEL), lambda i: (i, 0)),
            pl.BlockSpec((D_MODEL, D_MODEL), lambda i: (0, 0), pipeline_mode=pl.Buffered(1)),
            pl.BlockSpec((1, D_MODEL), lambda i: (0, 0)),
        ],
        out_specs=pl.BlockSpec((OUT_TM, D_MODEL), lambda i: (i, 0)),
        out_shape=jax.ShapeDtypeStruct((t, D_MODEL), F32),
        compiler_params=_compiler_params(1),
        name="out_proj",
    )(m, h, w_out, post_g)


def _cast_mixer_w_in(src_ref, qkv_ref, gate_ref):
    for name in QKV_ORDER:
        lo, hi = QKV_SRC[name]
        blk = src_ref[:, lo:hi]
        if name in ("qa", "qb"):
            blk = blk * Q_PRESCALE
        qkv_ref[:, QKV_COL[name]:QKV_COL[name] + hi - lo] = blk.astype(BF16)
    gate_ref[...] = src_ref[:, QKV_W:].astype(BF16)


def kernel(x, ffn1_pre_g, ffn1_w_in, ffn1_w_out, ffn1_post_g, mix_pre_g, w_in, b_gate, sink_a, rpb_b,
           w_up_a, w_up_b, w_out, mix_post_g, ffn2_pre_g, ffn2_w_in, ffn2_w_out, ffn2_post_g):
    batch, seq, d = x.shape
    assert (seq, d) == (SEQ, D_MODEL)
    depth = ffn1_w_in.shape[0]
    tokens = batch * seq
    n_tiles, nf = tokens // FFN_TM, D_FF // FFN_FC
    h = x.reshape(tokens, d)

    def row(v):
        return v.reshape(1, -1).astype(F32)

    for l in range(depth):
        head, w1_gate, w1_up, w1_out = _ffn_head(h, row(ffn1_pre_g[l]), ffn1_w_in[l], ffn1_w_out[l],
                                                 row(ffn1_post_g[l]))
        jobs = (_tile_rows_job(w_in[l], n_tiles, nf, _cast_mixer_w_in, (QKV_W, 2 * D_MODEL)),)
        h, w_qkv, w_gate = _ffn(h, row(ffn1_pre_g[l]), w1_gate, w1_up, 0, w1_out, row(ffn1_post_g[l]), jobs, head)
        later = (ffn2_w_in[l], ffn2_w_out[l], w_out[l], w_up_a[l], w_up_b[l])
        u, qkv, w2_in, w2_out, w_o, w_ua, w_ub = _qkv_proj(
            h, row(mix_pre_g[l]), w_qkv, lambda n_steps: tuple(_row_slab_job(w, n_steps) for w in later))
        qkv = qkv.reshape(batch, seq, QKV_W)
        u = u.reshape(batch, seq, d)
        sa = _attn_win(qkv, sink_a[l].astype(F32), QKV_COL["qa"], QKV_COL["ka"], QKV_COL["va"],
                       u, w_gate, w_ua, row(b_gate[l]))
        mixed = _attn_nbr(qkv, rpb_b[l].astype(F32).reshape(B_HEADS, -1),
                          QKV_COL["qb"], QKV_COL["kb"], QKV_COL["vb"],
                          u, w_gate, w_ub, row(b_gate[l]), sa)
        h = _out_proj(mixed.reshape(tokens, d), h, w_o, row(mix_post_g[l]))
        h, = _ffn(h, row(ffn2_pre_g[l]), w2_in, w2_in, nf, w2_out, row(ffn2_post_g[l]))
    return h.reshape(batch, seq, d)
```

```python
import functools
import math
from typing import Callable, NamedTuple

import jax
import jax.numpy as jnp
import numpy as np
from jax import lax
from jax.experimental import pallas as pl
from jax.experimental.pallas import tpu as pltpu

D_MODEL = 2048
SEQ = 4096
HEAD_DIM = 128
A_Q_HEADS = 8
A_KV_HEADS = 2
A_GROUP = A_Q_HEADS // A_KV_HEADS
A_WINDOW = 128
A_BLOCK = 128
B_HEADS = 8
GRID_W = 64
GRID_ROWS = SEQ // GRID_W
B_WIN_R = 8
B_WIN_C = 16
D_FF = 5632
MACARON_W = 0.5
RMS_EPS = 1e-6
NEG_INF = -1e30
SCALE = HEAD_DIM ** -0.5
LOG2E = math.log2(math.e)
Q_PRESCALE = SCALE * LOG2E

A_Q_W = A_Q_HEADS * HEAD_DIM
A_KV_W = A_KV_HEADS * HEAD_DIM
B_W = B_HEADS * HEAD_DIM
QKV_W = A_Q_W + 2 * A_KV_W + 3 * B_W

_cuts = np.cumsum([0, A_Q_W, A_KV_W, A_KV_W, B_W, B_W, B_W]).tolist()
QKV_SRC = {name: (_cuts[i], _cuts[i + 1]) for i, name in enumerate(("qa", "ka", "va", "qb", "kb", "vb"))}
QKV_ORDER = ("qa", "qb", "kb", "vb", "ka", "va")
QKV_COL = {}
for _name in QKV_ORDER:
    QKV_COL[_name] = sum(QKV_SRC[n][1] - QKV_SRC[n][0] for n in QKV_ORDER[:QKV_ORDER.index(_name)])

F32 = jnp.float32
BF16 = jnp.bfloat16

VMEM_LIMIT_BYTES = 63 * 1024 * 1024

FFN_TM = 1024
FFN_FC = 512
FFN_ROWS = 256
HEAD_FC = 512
QKV_TM = 512
QKV_ROWS = 256
OUT_TM = 1024
OUT_ROWS = 256
NORM_ROWS = 32
CAST_ROWS = 16

GATE_TC = 512
A_QB = 4
A_STEP = A_QB * A_BLOCK
B_PAIR_ROWS = 2
B_PAIR_TOK = B_PAIR_ROWS * GRID_W
B_KEY_ROWS = 10
B_KEY_TOK = B_KEY_ROWS * GRID_W
B_PAIRS_PER_STEP = 2
B_STEP = B_PAIRS_PER_STEP * B_PAIR_TOK
B_N_PAIRS = GRID_ROWS // B_PAIR_ROWS
B_WINDOW_ALIGN = 256
B_WINDOW_TOK = B_STEP + 2 * B_WINDOW_ALIGN
B_EDGE_PAIRS = B_WIN_R // 2 // B_PAIR_ROWS
B_VARIANT_PAIRS = (tuple(range(B_EDGE_PAIRS)) + (B_EDGE_PAIRS,)
                   + tuple(range(B_N_PAIRS - B_EDGE_PAIRS, B_N_PAIRS)))


def _alibi_slopes(n_heads):
    return [2.0 ** (-8.0 * (i + 1) / n_heads) for i in range(n_heads)]


def _compiler_params(n_axes):
    return pltpu.CompilerParams(dimension_semantics=("arbitrary",) * n_axes,
                                vmem_limit_bytes=VMEM_LIMIT_BYTES)


def _rms_scale(x):
    return x * lax.rsqrt(jnp.mean(x * x, axis=-1, keepdims=True) + RMS_EPS)


class CastJob(NamedTuple):
    src: jax.Array
    in_spec: pl.BlockSpec
    out_specs: tuple
    out_shapes: tuple
    body: Callable


def _cast_block(src_ref, dst_ref):
    dst_ref[...] = src_ref[...].astype(BF16)


def _tile_rows_job(src, n_tiles, n_steps, body=_cast_block, out_cols=None):
    rows, cols = src.shape
    per_tile = rows // (n_tiles * CAST_ROWS)
    assert per_tile * n_tiles * CAST_ROWS == rows and per_tile <= n_steps
    index_map = lambda i, j: (i * per_tile + jnp.minimum(j, per_tile - 1), 0)
    out_cols = (cols,) if out_cols is None else out_cols
    return CastJob(src, pl.BlockSpec((CAST_ROWS, cols), index_map),
                   tuple(pl.BlockSpec((CAST_ROWS, c), index_map) for c in out_cols),
                   tuple(jax.ShapeDtypeStruct((rows, c), BF16) for c in out_cols), body)


def _row_slab_job(src, n_steps):
    rows, cols = src.shape
    slab = rows // n_steps
    assert slab * n_steps == rows and slab % CAST_ROWS == 0
    spec = pl.BlockSpec((slab, cols), lambda i: (i, 0))
    return CastJob(src, spec, (spec,), (jax.ShapeDtypeStruct((rows, cols), BF16),), _cast_block)


def _run_jobs(jobs, src_refs, dst_refs):
    first = 0
    for job, src_ref in zip(jobs, src_refs):
        job.body(src_ref, *dst_refs[first:first + len(job.out_specs)])
        first += len(job.out_specs)


def _ffn_kernel(*refs, jobs, has_head):
    x_ref, pre_g_ref, wg_ref, wu_ref, wo_ref, post_g_ref = refs[:6]
    n_in = 6 + has_head
    job_srcs = refs[n_in:n_in + len(jobs)]
    o_ref = refs[n_in + len(jobs)]
    n_scratch = 1 + has_head
    job_dsts = refs[n_in + 1 + len(jobs):-n_scratch]
    n_ref = refs[-n_scratch]
    i = pl.program_id(0)
    j = pl.program_id(1)
    last = pl.num_programs(1) - 1
    active = (i > 0) if has_head else True
    row_chunks = [slice(r * FFN_ROWS, (r + 1) * FFN_ROWS) for r in range(FFN_TM // FFN_ROWS)]

    def swiglu(n):
        g = jnp.dot(n, wg_ref[...], preferred_element_type=F32)
        u = jnp.dot(n, wu_ref[...], preferred_element_type=F32)
        a = (g * jax.nn.sigmoid(g) * u).astype(BF16)
        return jnp.dot(a, wo_ref[...], preferred_element_type=F32)

    def run_jobs():
        _run_jobs(jobs, job_srcs, job_dsts)

    @pl.when((j == 0) & active)
    def _():
        gain = pre_g_ref[...]
        for rows in row_chunks:
            for c in range(rows.start, rows.stop, NORM_ROWS):
                rs = slice(c, c + NORM_ROWS)
                n_ref[rs, :] = (_rms_scale(x_ref[rs, :]) * gain).astype(BF16)
            o_ref[rows, :] = swiglu(n_ref[rows, :])
        run_jobs()

    @pl.when((j > 0) & (j < last) & active)
    def _():
        o_ref[...] += swiglu(n_ref[...])
        run_jobs()

    if has_head:
        head_ref, copy_sem = refs[6], refs[-1]

        @pl.when(i == 0)
        def _():
            run_jobs()

        @pl.when((i == 0) & (j == 0))
        def _():
            copy = pltpu.make_async_copy(head_ref, o_ref, copy_sem)
            copy.start()
            copy.wait()

    @pl.when((j == last) & active)
    def _():
        gain = MACARON_W * post_g_ref[...]
        for rows in row_chunks:
            f = o_ref[rows, :] + swiglu(n_ref[rows, :])
            for c in range(0, FFN_ROWS, NORM_ROWS):
                rs = slice(rows.start + c, rows.start + c + NORM_ROWS)
                o_ref[rs, :] = x_ref[rs, :] + _rms_scale(f[c:c + NORM_ROWS, :]) * gain
        run_jobs()


def _ffn_head_kernel(x_ref, pre_g_ref, wg_ref, wu_ref, wo_ref, post_g_ref,
                     o_ref, wg_bf_ref, wu_bf_ref, wo_bf_ref, n_ref):
    j = pl.program_id(0)
    row_chunks = [slice(r, r + NORM_ROWS) for r in range(0, FFN_TM, NORM_ROWS)]

    @pl.when(j == 0)
    def _():
        gain = pre_g_ref[...]
        for rs in row_chunks:
            n_ref[rs, :] = (_rms_scale(x_ref[rs, :]) * gain).astype(BF16)
        o_ref[...] = jnp.zeros_like(o_ref)

    wg, wu, wo = (ref[...].astype(BF16) for ref in (wg_ref, wu_ref, wo_ref))
    wg_bf_ref[...] = wg
    wu_bf_ref[...] = wu
    wo_bf_ref[...] = wo
    n = n_ref[...]
    g = jnp.dot(n, wg, preferred_element_type=F32)
    u = jnp.dot(n, wu, preferred_element_type=F32)
    a = (g * jax.nn.sigmoid(g) * u).astype(BF16)
    o_ref[...] += jnp.dot(a, wo, preferred_element_type=F32)

    @pl.when(j == pl.num_programs(0) - 1)
    def _():
        gain = MACARON_W * post_g_ref[...]
        for rs in row_chunks:
            o_ref[rs, :] = x_ref[rs, :] + _rms_scale(o_ref[rs, :]) * gain


def _ffn_head(x, pre_g, w_in, w_out, post_g):
    n_steps = D_FF // HEAD_FC
    return pl.pallas_call(
        _ffn_head_kernel,
        grid=(n_steps,),
        in_specs=[
            pl.BlockSpec((FFN_TM, D_MODEL), lambda j: (0, 0), pipeline_mode=pl.Buffered(1)),
            pl.BlockSpec((1, D_MODEL), lambda j: (0, 0)),
            pl.BlockSpec((D_MODEL, HEAD_FC), lambda j: (0, j)),
            pl.BlockSpec((D_MODEL, HEAD_FC), lambda j: (0, j + n_steps)),
            pl.BlockSpec((HEAD_FC, D_MODEL), lambda j: (j, 0)),
            pl.BlockSpec((1, D_MODEL), lambda j: (0, 0)),
        ],
        out_specs=[
            pl.BlockSpec((FFN_TM, D_MODEL), lambda j: (0, 0), pipeline_mode=pl.Buffered(1)),
            pl.BlockSpec((D_MODEL, HEAD_FC), lambda j: (0, j)),
            pl.BlockSpec((D_MODEL, HEAD_FC), lambda j: (0, j)),
            pl.BlockSpec((HEAD_FC, D_MODEL), lambda j: (j, 0)),
        ],
        out_shape=[jax.ShapeDtypeStruct((FFN_TM, D_MODEL), F32),
                   jax.ShapeDtypeStruct((D_MODEL, D_FF), BF16),
                   jax.ShapeDtypeStruct((D_MODEL, D_FF), BF16),
                   jax.ShapeDtypeStruct((D_FF, D_MODEL), BF16)],
        scratch_shapes=[pltpu.VMEM((FFN_TM, D_MODEL), BF16)],
        compiler_params=_compiler_params(1),
        name="ffn_head",
    )(x, pre_g, w_in, w_in, w_out, post_g)


def _ffn(x, pre_g, w_gate, w_up, up_block0, w_out, post_g, jobs=(), head=None):
    t = x.shape[0]
    nf = D_FF // FFN_FC
    assert nf >= 2
    has_head = head is not None
    chunk = (lambda i, j: jnp.where(i == 0, 0, j)) if has_head else (lambda i, j: j)
    out_specs = [pl.BlockSpec((FFN_TM, D_MODEL), lambda i, j: (i, 0))]
    out_shapes = [jax.ShapeDtypeStruct((t, D_MODEL), F32)]
    for job in jobs:
        out_specs.extend(job.out_specs)
        out_shapes.extend(job.out_shapes)
    return pl.pallas_call(
        functools.partial(_ffn_kernel, jobs=jobs, has_head=has_head),
        grid=(t // FFN_TM, nf),
        in_specs=[
            pl.BlockSpec((FFN_TM, D_MODEL), lambda i, j: (i, 0)),
            pl.BlockSpec((1, D_MODEL), lambda i, j: (0, 0)),
            pl.BlockSpec((D_MODEL, FFN_FC), lambda i, j: (0, chunk(i, j))),
            pl.BlockSpec((D_MODEL, FFN_FC), lambda i, j: (0, up_block0 + chunk(i, j))),
            pl.BlockSpec((FFN_FC, D_MODEL), lambda i, j: (chunk(i, j), 0)),
            pl.BlockSpec((1, D_MODEL), lambda i, j: (0, 0)),
        ] + ([pl.BlockSpec(memory_space=pl.ANY)] if has_head else []) + [job.in_spec for job in jobs],
        out_specs=out_specs,
        out_shape=out_shapes,
        scratch_shapes=[pltpu.VMEM((FFN_TM, D_MODEL), BF16)] + ([pltpu.SemaphoreType.DMA(())] if has_head else []),
        compiler_params=_compiler_params(2),
        name="ffn",
    )(x, pre_g, w_gate, w_up, w_out, post_g, *([head] if has_head else []), *[job.src for job in jobs])


def _qkv_kernel(*refs, jobs):
    h_ref, g_ref, w_ref = refs[:3]
    job_srcs = refs[3:3 + len(jobs)]
    u_ref, o_ref = refs[3 + len(jobs):5 + len(jobs)]
    gain = g_ref[...]
    for r in range(0, QKV_TM, QKV_ROWS):
        for c in range(r, r + QKV_ROWS, NORM_ROWS):
            rs = slice(c, c + NORM_ROWS)
            u_ref[rs, :] = (_rms_scale(h_ref[rs, :]) * gain).astype(BF16)
        rows = slice(r, r + QKV_ROWS)
        o_ref[rows, :] = jnp.dot(u_ref[rows, :], w_ref[...], preferred_element_type=F32).astype(BF16)
    _run_jobs(jobs, job_srcs, refs[5 + len(jobs):])


def _qkv_proj(h, gain, w_qkv, make_jobs):
    t = h.shape[0]
    n_steps = t // QKV_TM
    jobs = make_jobs(n_steps)
    return pl.pallas_call(
        functools.partial(_qkv_kernel, jobs=jobs),
        grid=(n_steps,),
        in_specs=[
            pl.BlockSpec((QKV_TM, D_MODEL), lambda i: (i, 0)),
            pl.BlockSpec((1, D_MODEL), lambda i: (0, 0)),
            pl.BlockSpec((D_MODEL, QKV_W), lambda i: (0, 0), pipeline_mode=pl.Buffered(1)),
        ] + [job.in_spec for job in jobs],
        out_specs=[
            pl.BlockSpec((QKV_TM, D_MODEL), lambda i: (i, 0)),
            pl.BlockSpec((QKV_TM, QKV_W), lambda i: (i, 0)),
        ] + [spec for job in jobs for spec in job.out_specs],
        out_shape=[jax.ShapeDtypeStruct((t, D_MODEL), BF16),
                   jax.ShapeDtypeStruct((t, QKV_W), BF16)] + [sh for job in jobs for sh in job.out_shapes],
        compiler_params=_compiler_params(1),
        name="qkv_proj",
    )(h, gain, w_qkv, *[job.src for job in jobs])


def _build_win_bias(tbl_ref):
    span = 3 * A_BLOCK
    qi = lax.broadcasted_iota(jnp.int32, (A_BLOCK, span), 0)
    kj = lax.broadcasted_iota(jnp.int32, (A_BLOCK, span), 1)
    absd_i = jnp.abs(qi + A_WINDOW - kj)
    absd = absd_i.astype(F32)
    in_window = absd_i <= A_WINDOW
    valid = (in_window & (kj >= A_BLOCK), in_window, in_window & (kj < 2 * A_BLOCK))
    for h, slope in enumerate(_alibi_slopes(A_Q_HEADS)):
        bias = (-slope * LOG2E) * absd
        for variant in range(3):
            tbl_ref[variant, h] = jnp.where(valid[variant], bias, NEG_INF)


def _attn_win_kernel(sink_ref, q_ref, kp_ref, kc_ref, kn_ref, vp_ref, vc_ref, vn_ref, u_ref, wg_ref, wup_ref,
                     bg_ref, o_ref, k_scr, v_scr, tbl_ref, a_scr):
    m = pl.program_id(1)

    @pl.when((pl.program_id(0) == 0) & (m == 0))
    def _():
        _build_win_bias(tbl_ref)

    k_scr[0:A_BLOCK, :] = kp_ref[...]
    k_scr[A_BLOCK:A_BLOCK + A_STEP, :] = kc_ref[...]
    k_scr[A_BLOCK + A_STEP:, :] = kn_ref[...]
    v_scr[0:A_BLOCK, :] = vp_ref[...]
    v_scr[A_BLOCK:A_BLOCK + A_STEP, :] = vc_ref[...]
    v_scr[A_BLOCK + A_STEP:, :] = vn_ref[...]
    span = 3 * A_BLOCK
    last_block = SEQ // A_BLOCK - 1
    piece = D_MODEL // (A_QB * A_KV_HEADS)

    def gate_piece(i):
        cols = slice(i * piece, (i + 1) * piece)
        gate = jnp.dot(u_ref[...], wg_ref[:, cols], preferred_element_type=F32) + bg_ref[:, cols]
        o_ref[:, cols] = jax.nn.sigmoid(gate)

    def sub_block(t):
        r = t * A_BLOCK
        block = m * A_QB + t
        variant = jnp.where(block == 0, 0, jnp.where(block == last_block, 2, 1))
        for g in range(A_KV_HEADS):
            k = k_scr[pl.ds(r, span), g * HEAD_DIM:(g + 1) * HEAD_DIM]
            v = v_scr[pl.ds(r, span), g * HEAD_DIM:(g + 1) * HEAD_DIM]
            heads = [g * A_GROUP + e for e in range(A_GROUP)]
            qs = jnp.concatenate(
                [q_ref[pl.ds(r, A_BLOCK), h * HEAD_DIM:(h + 1) * HEAD_DIM] for h in heads], axis=0)
            s = lax.dot_general(qs, k, (((1,), (1,)), ((), ())), preferred_element_type=F32)
            gate_piece(t * A_KV_HEADS + g)
            probs, dens = [], []
            for e, h in enumerate(heads):
                logits = s[e * A_BLOCK:(e + 1) * A_BLOCK, :] + tbl_ref[variant, h]
                sink = sink_ref[h] * LOG2E
                mx = jnp.maximum(jnp.max(logits, axis=-1, keepdims=True), sink)
                p = jnp.exp2(logits - mx)
                dens.append(jnp.sum(p, axis=-1, keepdims=True) + jnp.exp2(sink - mx))
                probs.append(p.astype(BF16))
            o = jnp.dot(jnp.concatenate(probs, axis=0), v, preferred_element_type=F32)
            for e, h in enumerate(heads):
                a_scr[pl.ds(r, A_BLOCK), h * HEAD_DIM:(h + 1) * HEAD_DIM] = (
                    o[e * A_BLOCK:(e + 1) * A_BLOCK, :] / dens[e]).astype(BF16)

    for t in range(A_QB):
        sub_block(t)
    attn = a_scr[...]
    for c in range(D_MODEL // GATE_TC):
        cols = slice(c * GATE_TC, (c + 1) * GATE_TC)
        o_ref[:, cols] = o_ref[:, cols] * jnp.dot(attn, wup_ref[:, cols], preferred_element_type=F32)


def _attn_win(qkv, sink, col_q, col_k, col_v, u, w_gate, w_up, b_gate):
    b = qkv.shape[0]
    n_steps = SEQ // A_STEP
    n_blocks = SEQ // A_BLOCK
    kcol, vcol = col_k // A_KV_W, col_v // A_KV_W

    def edge(col, shift):
        def index_map(bi, m):
            blk = jnp.clip(m * A_QB + shift, 0, n_blocks - 1)
            return (bi, blk, col)
        return pl.BlockSpec((None, A_BLOCK, A_KV_W), index_map)

    def centre(col):
        return pl.BlockSpec((None, A_STEP, A_KV_W), lambda bi, m: (bi, m, col))

    return pl.pallas_call(
        _attn_win_kernel,
        grid=(b, n_steps),
        in_specs=[
            pl.BlockSpec(memory_space=pltpu.SMEM),
            pl.BlockSpec((None, A_STEP, A_Q_W), lambda bi, m: (bi, m, col_q // A_Q_W)),
            edge(kcol, -1), centre(kcol), edge(kcol, A_QB),
            edge(vcol, -1), centre(vcol), edge(vcol, A_QB),
            pl.BlockSpec((None, A_STEP, D_MODEL), lambda bi, m: (bi, m, 0)),
            pl.BlockSpec((D_MODEL, D_MODEL), lambda bi, m: (0, 0)),
            pl.BlockSpec((A_Q_W, D_MODEL), lambda bi, m: (0, 0)),
            pl.BlockSpec((1, D_MODEL), lambda bi, m: (0, 0)),
        ],
        out_specs=pl.BlockSpec((None, A_STEP, D_MODEL), lambda bi, m: (bi, m, 0)),
        out_shape=jax.ShapeDtypeStruct((b, SEQ, D_MODEL), F32),
        scratch_shapes=[pltpu.VMEM((A_STEP + 2 * A_BLOCK, A_KV_W), BF16),
                        pltpu.VMEM((A_STEP + 2 * A_BLOCK, A_KV_W), BF16),
                        pltpu.VMEM((3, A_Q_HEADS, A_BLOCK, 3 * A_BLOCK), F32),
                        pltpu.VMEM((A_STEP, A_Q_W), BF16)],
        compiler_params=_compiler_params(2),
        name="attn_win",
    )(sink, qkv, qkv, qkv, qkv, qkv, qkv, qkv, u, w_gate, w_up, b_gate)


def _nbr_row_start(row):
    return min(max(row - B_WIN_R // 2, 0), GRID_ROWS - B_WIN_R)


def _nbr_key_row0(pair):
    return min(_nbr_row_start(pair * B_PAIR_ROWS), GRID_ROWS - B_KEY_ROWS)


def _build_nbr_bias(rpb_ref, tbl_ref, toep_l, toep_r):
    lanes = 2 * GRID_W
    qc = lax.broadcasted_iota(jnp.int32, (GRID_W, lanes), 0)
    lane = lax.broadcasted_iota(jnp.int32, (GRID_W, lanes), 1)
    col_start = jnp.clip(qc - B_WIN_C // 2, 0, GRID_W - B_WIN_C)
    left = lane < GRID_W
    neg = jnp.full((GRID_W, lanes), NEG_INF, F32)
    n_rel_r, n_rel_c = 2 * B_WIN_R - 1, 2 * B_WIN_C - 1

    def per_head(h, carry):
        for half, dst in ((0, toep_l), (1, toep_r)):
            kc = lane - half * GRID_W
            rel = kc - qc + (B_WIN_C - 1)
            col_ok = (kc >= col_start) & (kc < col_start + B_WIN_C) & (left if half == 0 else ~left)
            for a in range(n_rel_r):
                t = jnp.zeros((GRID_W, lanes), F32)
                for c in range(n_rel_c):
                    t = jnp.where(rel == c, rpb_ref[h, a * n_rel_c + c] * LOG2E, t)
                dst[a] = jnp.where(col_ok, t, neg)
        for vi, pair in enumerate(B_VARIANT_PAIRS):
            row0 = _nbr_key_row0(pair)
            for ql in range(B_PAIR_ROWS):
                q_row = pair * B_PAIR_ROWS + ql
                start = _nbr_row_start(q_row)
                for mt in range(B_KEY_ROWS // 2):
                    halves = []
                    for half, src in ((0, toep_l), (1, toep_r)):
                        k_row = row0 + 2 * mt + half
                        if start <= k_row < start + B_WIN_R:
                            halves.append(src[k_row - q_row + B_WIN_R - 1])
                        else:
                            halves.append(neg)
                    tbl_ref[vi, h, ql * GRID_W:(ql + 1) * GRID_W, mt * lanes:(mt + 1) * lanes] = (
                        jnp.where(left, halves[0], halves[1]))
        return carry

    lax.fori_loop(0, B_HEADS, per_head, 0)


def _nbr_window_block(step):
    return jnp.clip(step * (B_STEP // B_WINDOW_ALIGN) - 1, 0, (SEQ - B_WINDOW_TOK) // B_WINDOW_ALIGN)


def _attn_nbr_kernel(rpb_ref, q_ref, k_ref, v_ref, u_ref, wg_ref, wup_ref, bg_ref, sa_ref, o_ref,
                     tbl_ref, toep_l, toep_r, attn_scr, sig_scr):
    step = pl.program_id(1)

    @pl.when((pl.program_id(0) == 0) & (step == 0))
    def _():
        _build_nbr_bias(rpb_ref, tbl_ref, toep_l, toep_r)

    window_row0 = _nbr_window_block(step) * (B_WINDOW_ALIGN // GRID_W)
    piece = D_MODEL // B_PAIRS_PER_STEP
    head_cols = [slice(h * HEAD_DIM, (h + 1) * HEAD_DIM) for h in range(B_HEADS)]

    def gate_piece(i):
        cols = slice(i * piece, (i + 1) * piece)
        gate = jnp.dot(u_ref[...], wg_ref[:, cols], preferred_element_type=F32) + bg_ref[:, cols]
        sig_scr[:, cols] = jax.nn.sigmoid(gate)

    for pp in range(B_PAIRS_PER_STEP):
        pair = step * B_PAIRS_PER_STEP + pp
        row0 = jnp.clip(pair * B_PAIR_ROWS - B_WIN_R // 2, 0, GRID_ROWS - B_KEY_ROWS)
        variant = jnp.where(pair < B_EDGE_PAIRS, pair,
                            jnp.where(pair >= B_N_PAIRS - B_EDGE_PAIRS,
                                      pair - (B_N_PAIRS - len(B_VARIANT_PAIRS)), B_EDGE_PAIRS))
        k0 = pl.multiple_of((row0 - window_row0) * GRID_W, GRID_W)
        q_rows = slice(pp * B_PAIR_TOK, (pp + 1) * B_PAIR_TOK)
        scores = [
            lax.dot_general(q_ref[q_rows, cols], k_ref[0, pl.ds(k0, B_KEY_TOK), cols],
                            (((1,), (1,)), ((), ())), preferred_element_type=F32)
            for cols in head_cols]
        gate_piece(pp)
        s = jnp.concatenate(scores, axis=0)
        logits = s + tbl_ref[variant].reshape(B_HEADS * B_PAIR_TOK, B_KEY_TOK)
        mx = jnp.max(logits, axis=-1, keepdims=True)
        p = jnp.exp2(logits - mx)
        inv_den = 1.0 / jnp.sum(p, axis=-1, keepdims=True)
        p = p.astype(BF16)
        for h, cols in enumerate(head_cols):
            rows = slice(h * B_PAIR_TOK, (h + 1) * B_PAIR_TOK)
            o = jnp.dot(p[rows, :], v_ref[0, pl.ds(k0, B_KEY_TOK), cols], preferred_element_type=F32)
            attn_scr[q_rows, cols] = (o * inv_den[rows, :]).astype(BF16)

    attn = attn_scr[...]
    for c in range(D_MODEL // GATE_TC):
        cols = slice(c * GATE_TC, (c + 1) * GATE_TC)
        up = jnp.dot(attn, wup_ref[:, cols], preferred_element_type=F32)
        o_ref[:, cols] = (sa_ref[:, cols] + sig_scr[:, cols] * up).astype(BF16)


def _attn_nbr(qkv, rpb, col_q, col_k, col_v, u, w_gate, w_up, b_gate, sa):
    b = qkv.shape[0]
    assert B_WINDOW_ALIGN == (B_WIN_R // 2) * GRID_W and B_STEP % B_WINDOW_ALIGN == 0

    def window(col):
        return pl.BlockSpec((pl.Element(1), pl.Element(B_WINDOW_TOK), pl.Element(B_W)),
                            lambda bi, s: (bi, _nbr_window_block(s) * B_WINDOW_ALIGN, col))

    resident = dict(pipeline_mode=pl.Buffered(1))
    return pl.pallas_call(
        _attn_nbr_kernel,
        grid=(b, SEQ // B_STEP),
        in_specs=[
            pl.BlockSpec(memory_space=pltpu.SMEM),
            pl.BlockSpec((None, B_STEP, B_W), lambda bi, s: (bi, s, col_q // B_W)),
            window(col_k), window(col_v),
            pl.BlockSpec((None, B_STEP, D_MODEL), lambda bi, s: (bi, s, 0)),
            pl.BlockSpec((D_MODEL, D_MODEL), lambda bi, s: (0, 1), **resident),
            pl.BlockSpec((B_W, D_MODEL), lambda bi, s: (0, 0), **resident),
            pl.BlockSpec((1, D_MODEL), lambda bi, s: (0, 1)),
            pl.BlockSpec((None, B_STEP, D_MODEL), lambda bi, s: (bi, s, 0)),
        ],
        out_specs=pl.BlockSpec((None, B_STEP, D_MODEL), lambda bi, s: (bi, s, 0)),
        out_shape=jax.ShapeDtypeStruct((b, SEQ, D_MODEL), BF16),
        scratch_shapes=[
            pltpu.VMEM((len(B_VARIANT_PAIRS), B_HEADS, B_PAIR_TOK, B_KEY_TOK), F32),
            pltpu.VMEM((2 * B_WIN_R - 1, GRID_W, 2 * GRID_W), F32),
            pltpu.VMEM((2 * B_WIN_R - 1, GRID_W, 2 * GRID_W), F32),
            pltpu.VMEM((B_STEP, B_W), BF16),
            pltpu.VMEM((B_STEP, D_MODEL), F32),
        ],
        compiler_params=_compiler_params(2),
        name="attn_nbr",
    )(rpb, qkv, qkv, qkv, u, w_gate, w_up, b_gate, sa)


def _out_proj_kernel(m_ref, h_ref, w_ref, post_g_ref, o_ref):
    gain = post_g_ref[...]
    for r in range(OUT_TM // OUT_ROWS):
        y = jnp.dot(m_ref[r * OUT_ROWS:(r + 1) * OUT_ROWS, :], w_ref[...], preferred_element_type=F32)
        for c in range(OUT_ROWS // NORM_ROWS):
            rs = slice(r * OUT_ROWS + c * NORM_ROWS, r * OUT_ROWS + (c + 1) * NORM_ROWS)
            o_ref[rs, :] = h_ref[rs, :] + _rms_scale(y[c * NORM_ROWS:(c + 1) * NORM_ROWS, :]) * gain


def _out_proj(m, h, w_out, post_g):
    t = m.shape[0]
    return pl.pallas_call(
        _out_proj_kernel,
        grid=(t // OUT_TM,),
        in_specs=[
            pl.BlockSpec((OUT_TM, D_MODEL), lambda i: (i, 0)),
            pl.BlockSpec((OUT_TM, D_MODEL), lambda i: (i, 0)),
            pl.BlockSpec((D_MODEL, D_MODEL), lambda i: (0, 0), pipeline_mode=pl.Buffered(1)),
            pl.BlockSpec((1, D_MODEL), lambda i: (0, 0)),
        ],
        out_specs=pl.BlockSpec((OUT_TM, D_MODEL), lambda i: (i, 0)),
        out_shape=jax.ShapeDtypeStruct((t, D_MODEL), F32),
        compiler_params=_compiler_params(1),
        name="out_proj",
    )(m, h, w_out, post_g)


def _cast_mixer_w_in(src_ref, qkv_ref, gate_ref):
    for name in QKV_ORDER:
        lo, hi = QKV_SRC[name]
        blk = src_ref[:, lo:hi]
        if name in ("qa", "qb"):
            blk = blk * Q_PRESCALE
        qkv_ref[:, QKV_COL[name]:QKV_COL[name] + hi - lo] = blk.astype(BF16)
    gate_ref[...] = src_ref[:, QKV_W:].astype(BF16)


def kernel(x, ffn1_pre_g, ffn1_w_in, ffn1_w_out, ffn1_post_g, mix_pre_g, w_in, b_gate, sink_a, rpb_b,
           w_up_a, w_up_b, w_out, mix_post_g, ffn2_pre_g, ffn2_w_in, ffn2_w_out, ffn2_post_g):
    batch, seq, d = x.shape
    assert (seq, d) == (SEQ, D_MODEL)
    depth = ffn1_w_in.shape[0]
    tokens = batch * seq
    n_tiles, nf = tokens // FFN_TM, D_FF // FFN_FC
    h = x.reshape(tokens, d)

    def row(v):
        return v.reshape(1, -1).astype(F32)

    for l in range(depth):
        head, w1_gate, w1_up, w1_out = _ffn_head(h, row(ffn1_pre_g[l]), ffn1_w_in[l], ffn1_w_out[l],
                                                 row(ffn1_post_g[l]))
        jobs = (_tile_rows_job(w_in[l], n_tiles, nf, _cast_mixer_w_in, (QKV_W, 2 * D_MODEL)),)
        h, w_qkv, w_gate = _ffn(h, row(ffn1_pre_g[l]), w1_gate, w1_up, 0, w1_out, row(ffn1_post_g[l]), jobs, head)
        later = (ffn2_w_in[l], ffn2_w_out[l], w_out[l], w_up_a[l], w_up_b[l])
        u, qkv, w2_in, w2_out, w_o, w_ua, w_ub = _qkv_proj(
            h, row(mix_pre_g[l]), w_qkv, lambda n_steps: tuple(_row_slab_job(w, n_steps) for w in later))
        qkv = qkv.reshape(batch, seq, QKV_W)
        u = u.reshape(batch, seq, d)
        sa = _attn_win(qkv, sink_a[l].astype(F32), QKV_COL["qa"], QKV_COL["ka"], QKV_COL["va"],
                       u, w_gate, w_ua, row(b_gate[l]))
        mixed = _attn_nbr(qkv, rpb_b[l].astype(F32).reshape(B_HEADS, -1),
                          QKV_COL["qb"], QKV_COL["kb"], QKV_COL["vb"],
                          u, w_gate, w_ub, row(b_gate[l]), sa)
        h = _out_proj(mixed.reshape(tokens, d), h, w_o, row(mix_post_g[l]))
        h, = _ffn(h, row(ffn2_pre_g[l]), w2_in, w2_in, nf, w2_out, row(ffn2_post_g[l]))
    return h.reshape(batch, seq, d)
```

```python
import functools
import math
from typing import Callable, NamedTuple

import jax
import jax.numpy as jnp
import numpy as np
from jax import lax
from jax.experimental import pallas as pl
from jax.experimental.pallas import tpu as pltpu

D_MODEL = 2048
SEQ = 4096
HEAD_DIM = 128
A_Q_HEADS = 8
A_KV_HEADS = 2
A_GROUP = A_Q_HEADS // A_KV_HEADS
A_WINDOW = 128
A_BLOCK = 128
B_HEADS = 8
GRID_W = 64
GRID_ROWS = SEQ // GRID_W
B_WIN_R = 8
B_WIN_C = 16
D_FF = 5632
MACARON_W = 0.5
RMS_EPS = 1e-6
NEG_INF = -1e30
SCALE = HEAD_DIM ** -0.5
LOG2E = math.log2(math.e)
Q_PRESCALE = SCALE * LOG2E

A_Q_W = A_Q_HEADS * HEAD_DIM
A_KV_W = A_KV_HEADS * HEAD_DIM
B_W = B_HEADS * HEAD_DIM
QKV_W = A_Q_W + 2 * A_KV_W + 3 * B_W

_cuts = np.cumsum([0, A_Q_W, A_KV_W, A_KV_W, B_W, B_W, B_W]).tolist()
QKV_SRC = {name: (_cuts[i], _cuts[i + 1]) for i, name in enumerate(("qa", "ka", "va", "qb", "kb", "vb"))}
QKV_ORDER = ("qa", "qb", "kb", "vb", "ka", "va")
QKV_COL = {}
for _name in QKV_ORDER:
    QKV_COL[_name] = sum(QKV_SRC[n][1] - QKV_SRC[n][0] for n in QKV_ORDER[:QKV_ORDER.index(_name)])

F32 = jnp.float32
BF16 = jnp.bfloat16

VMEM_LIMIT_BYTES = 61 * 1024 * 1024
RING_VMEM_LIMIT_BYTES = 63 * 1024 * 1024

FFN_TM = 1024
FFN_FC = 512
FFN_ROWS = 256
RING_SLOTS = 3
HEAD_FC = 256
QKV_TM = 512
QKV_ROWS = 256
OUT_TM = 1024
OUT_ROWS = 256
NORM_ROWS = 32
CAST_ROWS = 16

GATE_TC = 512
A_QB = 4
A_STEP = A_QB * A_BLOCK
B_PAIR_ROWS = 2
B_PAIR_TOK = B_PAIR_ROWS * GRID_W
B_KEY_ROWS = 10
B_KEY_TOK = B_KEY_ROWS * GRID_W
B_PAIRS_PER_STEP = 2
B_STEP = B_PAIRS_PER_STEP * B_PAIR_TOK
B_N_PAIRS = GRID_ROWS // B_PAIR_ROWS
B_WINDOW_ALIGN = 256
B_WINDOW_TOK = B_STEP + 2 * B_WINDOW_ALIGN
B_EDGE_PAIRS = B_WIN_R // 2 // B_PAIR_ROWS
B_VARIANT_PAIRS = (tuple(range(B_EDGE_PAIRS)) + (B_EDGE_PAIRS,)
                   + tuple(range(B_N_PAIRS - B_EDGE_PAIRS, B_N_PAIRS)))


def _alibi_slopes(n_heads):
    return [2.0 ** (-8.0 * (i + 1) / n_heads) for i in range(n_heads)]


def _compiler_params(n_axes, vmem_limit_bytes=VMEM_LIMIT_BYTES):
    return pltpu.CompilerParams(dimension_semantics=("arbitrary",) * n_axes,
                                vmem_limit_bytes=vmem_limit_bytes)


def _rms_scale(x):
    return x * lax.rsqrt(jnp.mean(x * x, axis=-1, keepdims=True) + RMS_EPS)


class CastJob(NamedTuple):
    src: jax.Array
    in_spec: pl.BlockSpec
    out_specs: tuple
    out_shapes: tuple
    body: Callable


def _cast_block(src_ref, dst_ref):
    dst_ref[...] = src_ref[...].astype(BF16)


def _tile_rows_job(src, n_tiles, n_steps, body=_cast_block, out_cols=None):
    rows, cols = src.shape
    per_tile = rows // (n_tiles * CAST_ROWS)
    assert per_tile * n_tiles * CAST_ROWS == rows and per_tile <= n_steps
    index_map = lambda i, j: (i * per_tile + jnp.minimum(j, per_tile - 1), 0)
    out_cols = (cols,) if out_cols is None else out_cols
    return CastJob(src, pl.BlockSpec((CAST_ROWS, cols), index_map),
                   tuple(pl.BlockSpec((CAST_ROWS, c), index_map) for c in out_cols),
                   tuple(jax.ShapeDtypeStruct((rows, c), BF16) for c in out_cols), body)


def _row_slab_job(src, n_steps):
    rows, cols = src.shape
    slab = rows // n_steps
    assert slab * n_steps == rows and slab % CAST_ROWS == 0
    spec = pl.BlockSpec((slab, cols), lambda i: (i, 0))
    return CastJob(src, spec, (spec,), (jax.ShapeDtypeStruct((rows, cols), BF16),), _cast_block)


def _run_jobs(jobs, src_refs, dst_refs):
    first = 0
    for job, src_ref in zip(jobs, src_refs):
        job.body(src_ref, *dst_refs[first:first + len(job.out_specs)])
        first += len(job.out_specs)


def _ffn_kernel(*refs, jobs, has_head):
    x_ref, pre_g_ref, wg_ref, wu_ref, wo_ref, post_g_ref = refs[:6]
    n_in = 6 + has_head
    job_srcs = refs[n_in:n_in + len(jobs)]
    o_ref = refs[n_in + len(jobs)]
    n_scratch = 1 + has_head
    job_dsts = refs[n_in + 1 + len(jobs):-n_scratch]
    n_ref = refs[-n_scratch]
    i = pl.program_id(0)
    j = pl.program_id(1)
    last = pl.num_programs(1) - 1
    active = (i > 0) if has_head else True
    row_chunks = [slice(r * FFN_ROWS, (r + 1) * FFN_ROWS) for r in range(FFN_TM // FFN_ROWS)]

    def swiglu(n):
        g = jnp.dot(n, wg_ref[...], preferred_element_type=F32)
        u = jnp.dot(n, wu_ref[...], preferred_element_type=F32)
        a = (g * jax.nn.sigmoid(g) * u).astype(BF16)
        return jnp.dot(a, wo_ref[...], preferred_element_type=F32)

    def run_jobs():
        _run_jobs(jobs, job_srcs, job_dsts)

    @pl.when((j == 0) & active)
    def _():
        gain = pre_g_ref[...]
        for rows in row_chunks:
            for c in range(rows.start, rows.stop, NORM_ROWS):
                rs = slice(c, c + NORM_ROWS)
                n_ref[rs, :] = (_rms_scale(x_ref[rs, :]) * gain).astype(BF16)
            o_ref[rows, :] = swiglu(n_ref[rows, :])
        run_jobs()

    @pl.when((j > 0) & (j < last) & active)
    def _():
        o_ref[...] += swiglu(n_ref[...])
        run_jobs()

    if has_head:
        head_ref, copy_sem = refs[6], refs[-1]

        @pl.when(i == 0)
        def _():
            run_jobs()

        @pl.when((i == 0) & (j == 0))
        def _():
            copy = pltpu.make_async_copy(head_ref, o_ref, copy_sem)
            copy.start()
            copy.wait()

    @pl.when((j == last) & active)
    def _():
        gain = MACARON_W * post_g_ref[...]
        for rows in row_chunks:
            f = o_ref[rows, :] + swiglu(n_ref[rows, :])
            for c in range(0, FFN_ROWS, NORM_ROWS):
                rs = slice(rows.start + c, rows.start + c + NORM_ROWS)
                o_ref[rs, :] = x_ref[rs, :] + _rms_scale(f[c:c + NORM_ROWS, :]) * gain
        run_jobs()


def _ffn_head_kernel(x_ref, pre_g_ref, wg_ref, wu_ref, wo_ref, post_g_ref,
                     o_ref, wg_bf_ref, wu_bf_ref, wo_bf_ref, n_ref):
    j = pl.program_id(0)
    row_chunks = [slice(r, r + NORM_ROWS) for r in range(0, FFN_TM, NORM_ROWS)]

    @pl.when(j == 0)
    def _():
        gain = pre_g_ref[...]
        for rs in row_chunks:
            n_ref[rs, :] = (_rms_scale(x_ref[rs, :]) * gain).astype(BF16)
        o_ref[...] = jnp.zeros_like(o_ref)

    wg, wu, wo = (ref[...].astype(BF16) for ref in (wg_ref, wu_ref, wo_ref))
    wg_bf_ref[...] = wg
    wu_bf_ref[...] = wu
    wo_bf_ref[...] = wo
    n = n_ref[...]
    g = jnp.dot(n, wg, preferred_element_type=F32)
    u = jnp.dot(n, wu, preferred_element_type=F32)
    a = (g * jax.nn.sigmoid(g) * u).astype(BF16)
    o_ref[...] += jnp.dot(a, wo, preferred_element_type=F32)

    @pl.when(j == pl.num_programs(0) - 1)
    def _():
        gain = MACARON_W * post_g_ref[...]
        for rs in row_chunks:
            o_ref[rs, :] = x_ref[rs, :] + _rms_scale(o_ref[rs, :]) * gain


def _ffn_head(x, pre_g, w_in, w_out, post_g):
    n_steps = D_FF // HEAD_FC
    return pl.pallas_call(
        _ffn_head_kernel,
        grid=(n_steps,),
        in_specs=[
            pl.BlockSpec((FFN_TM, D_MODEL), lambda j: (0, 0)),
            pl.BlockSpec((1, D_MODEL), lambda j: (0, 0)),
            pl.BlockSpec((D_MODEL, HEAD_FC), lambda j: (0, j)),
            pl.BlockSpec((D_MODEL, HEAD_FC), lambda j: (0, j + n_steps)),
            pl.BlockSpec((HEAD_FC, D_MODEL), lambda j: (j, 0)),
            pl.BlockSpec((1, D_MODEL), lambda j: (0, 0)),
        ],
        out_specs=[
            pl.BlockSpec((FFN_TM, D_MODEL), lambda j: (0, 0)),
            pl.BlockSpec((D_MODEL, HEAD_FC), lambda j: (0, j)),
            pl.BlockSpec((D_MODEL, HEAD_FC), lambda j: (0, j)),
            pl.BlockSpec((HEAD_FC, D_MODEL), lambda j: (j, 0)),
        ],
        out_shape=[jax.ShapeDtypeStruct((FFN_TM, D_MODEL), F32),
                   jax.ShapeDtypeStruct((D_MODEL, D_FF), BF16),
                   jax.ShapeDtypeStruct((D_MODEL, D_FF), BF16),
                   jax.ShapeDtypeStruct((D_FF, D_MODEL), BF16)],
        scratch_shapes=[pltpu.VMEM((FFN_TM, D_MODEL), BF16)],
        compiler_params=_compiler_params(1),
        name="ffn_head",
    )(x, pre_g, w_in, w_in, w_out, post_g)


def _ffn(x, pre_g, w_gate, w_up, up_block0, w_out, post_g, jobs=(), head=None):
    t = x.shape[0]
    nf = D_FF // FFN_FC
    assert nf >= 2
    has_head = head is not None
    chunk = (lambda i, j: jnp.where(i == 0, 0, j)) if has_head else (lambda i, j: j)
    out_specs = [pl.BlockSpec((FFN_TM, D_MODEL), lambda i, j: (i, 0))]
    out_shapes = [jax.ShapeDtypeStruct((t, D_MODEL), F32)]
    for job in jobs:
        out_specs.extend(job.out_specs)
        out_shapes.extend(job.out_shapes)
    return pl.pallas_call(
        functools.partial(_ffn_kernel, jobs=jobs, has_head=has_head),
        grid=(t // FFN_TM, nf),
        in_specs=[
            pl.BlockSpec((FFN_TM, D_MODEL), lambda i, j: (i, 0)),
            pl.BlockSpec((1, D_MODEL), lambda i, j: (0, 0)),
            pl.BlockSpec((D_MODEL, FFN_FC), lambda i, j: (0, chunk(i, j))),
            pl.BlockSpec((D_MODEL, FFN_FC), lambda i, j: (0, up_block0 + chunk(i, j))),
            pl.BlockSpec((FFN_FC, D_MODEL), lambda i, j: (chunk(i, j), 0)),
            pl.BlockSpec((1, D_MODEL), lambda i, j: (0, 0)),
        ] + ([pl.BlockSpec(memory_space=pl.ANY)] if has_head else []) + [job.in_spec for job in jobs],
        out_specs=out_specs,
        out_shape=out_shapes,
        scratch_shapes=[pltpu.VMEM((FFN_TM, D_MODEL), BF16)] + ([pltpu.SemaphoreType.DMA(())] if has_head else []),
        compiler_params=_compiler_params(2),
        name="ffn",
    )(x, pre_g, w_gate, w_up, w_out, post_g, *([head] if has_head else []), *[job.src for job in jobs])


def _ffn_ring_kernel(x_ref, pre_g_ref, w_in_hbm, w_out_hbm, post_g_ref, o_ref,
                     n_ref, wg_buf, wu_buf, wo_buf, sems, *, up_col0):
    i = pl.program_id(0)
    s = pl.program_id(1)
    last_tile = pl.num_programs(0) - 1
    last = pl.num_programs(1) - 1
    nf = D_FF // FFN_FC
    tile_g0 = i * nf
    row_chunks = [slice(r * FFN_ROWS, (r + 1) * FFN_ROWS) for r in range(FFN_TM // FFN_ROWS)]

    def chunk_copies(c, slot):
        col = pl.multiple_of(c * FFN_FC, FFN_FC)
        return (pltpu.make_async_copy(w_in_hbm.at[:, pl.ds(col, FFN_FC)], wg_buf.at[slot], sems.at[slot, 0]),
                pltpu.make_async_copy(w_in_hbm.at[:, pl.ds(up_col0 + col, FFN_FC)], wu_buf.at[slot],
                                      sems.at[slot, 1]),
                pltpu.make_async_copy(w_out_hbm.at[pl.ds(col, FFN_FC), :], wo_buf.at[slot], sems.at[slot, 2]))

    def start(c, g):
        for copy in chunk_copies(c, lax.rem(g, RING_SLOTS)):
            copy.start()

    def wait(c, g):
        slot = lax.rem(g, RING_SLOTS)
        for copy in chunk_copies(c, slot):
            copy.wait()
        return slot

    def gate_up(n, slot):
        g = jnp.dot(n, wg_buf[slot], preferred_element_type=F32)
        u = jnp.dot(n, wu_buf[slot], preferred_element_type=F32)
        return (g * jax.nn.sigmoid(g) * u).astype(BF16)

    def down(a, slot):
        return jnp.dot(a, wo_buf[slot], preferred_element_type=F32)

    @pl.when(s == 0)
    def _():
        @pl.when(i == 0)
        def _():
            start(0, tile_g0)
            start(1, tile_g0 + 1)

        slot_a = wait(0, tile_g0)
        slot_b = wait(1, tile_g0 + 1)
        start(2, tile_g0 + 2)
        gain = pre_g_ref[...]
        for rows in row_chunks:
            for c in range(rows.start, rows.stop, NORM_ROWS):
                rs = slice(c, c + NORM_ROWS)
                n_ref[rs, :] = (_rms_scale(x_ref[rs, :]) * gain).astype(BF16)
            o_ref[rows, :] = down(gate_up(n_ref[rows, :], slot_a), slot_a)
        start(3, tile_g0 + 3)
        o_ref[...] += down(gate_up(n_ref[...], slot_b), slot_b)

    def two_chunks(next_has_two):
        c = 2 * s
        g = tile_g0 + c
        slot_a = wait(c, g)
        slot_b = wait(c + 1, g + 1)
        start(c + 2, g + 2)
        o_ref[...] += down(gate_up(n_ref[...], slot_a), slot_a)
        if next_has_two:
            start(c + 3, g + 3)
        o_ref[...] += down(gate_up(n_ref[...], slot_b), slot_b)

    @pl.when((s > 0) & (s < last - 1))
    def _():
        two_chunks(True)

    @pl.when(s == last - 1)
    def _():
        two_chunks(False)

    @pl.when(s == last)
    def _():
        g = tile_g0 + nf - 1

        @pl.when(i < last_tile)
        def _():
            start(0, g + 1)
            start(1, g + 2)

        slot = wait(nf - 1, g)
        gain = MACARON_W * post_g_ref[...]
        for rows in row_chunks:
            f = o_ref[rows, :] + down(gate_up(n_ref[rows, :], slot), slot)
            for c in range(0, FFN_ROWS, NORM_ROWS):
                rs = slice(rows.start + c, rows.start + c + NORM_ROWS)
                o_ref[rs, :] = x_ref[rs, :] + _rms_scale(f[c:c + NORM_ROWS, :]) * gain


def _ffn_ring(x, pre_g, w_in, w_out, post_g):
    t = x.shape[0]
    nf = D_FF // FFN_FC
    assert nf % 2 == 1 and nf >= 5
    return pl.pallas_call(
        functools.partial(_ffn_ring_kernel, up_col0=D_FF),
        grid=(t // FFN_TM, (nf + 1) // 2),
        in_specs=[
            pl.BlockSpec((FFN_TM, D_MODEL), lambda i, s: (i, 0)),
            pl.BlockSpec((1, D_MODEL), lambda i, s: (0, 0)),
            pl.BlockSpec(memory_space=pl.ANY),
            pl.BlockSpec(memory_space=pl.ANY),
            pl.BlockSpec((1, D_MODEL), lambda i, s: (0, 0)),
        ],
        out_specs=pl.BlockSpec((FFN_TM, D_MODEL), lambda i, s: (i, 0)),
        out_shape=jax.ShapeDtypeStruct((t, D_MODEL), F32),
        scratch_shapes=[pltpu.VMEM((FFN_TM, D_MODEL), BF16),
                        pltpu.VMEM((RING_SLOTS, D_MODEL, FFN_FC), BF16),
                        pltpu.VMEM((RING_SLOTS, D_MODEL, FFN_FC), BF16),
                        pltpu.VMEM((RING_SLOTS, FFN_FC, D_MODEL), BF16),
                        pltpu.SemaphoreType.DMA((RING_SLOTS, 3))],
        compiler_params=_compiler_params(2, RING_VMEM_LIMIT_BYTES),
        name="ffn_ring",
    )(x, pre_g, w_in, w_out, post_g)


def _qkv_kernel(*refs, jobs):
    h_ref, g_ref, w_ref = refs[:3]
    job_srcs = refs[3:3 + len(jobs)]
    u_ref, o_ref = refs[3 + len(jobs):5 + len(jobs)]
    gain = g_ref[...]
    for r in range(0, QKV_TM, QKV_ROWS):
        for c in range(r, r + QKV_ROWS, NORM_ROWS):
            rs = slice(c, c + NORM_ROWS)
            u_ref[rs, :] = (_rms_scale(h_ref[rs, :]) * gain).astype(BF16)
        rows = slice(r, r + QKV_ROWS)
        o_ref[rows, :] = jnp.dot(u_ref[rows, :], w_ref[...], preferred_element_type=F32).astype(BF16)
    _run_jobs(jobs, job_srcs, refs[5 + len(jobs):])


def _qkv_proj(h, gain, w_qkv, make_jobs):
    t = h.shape[0]
    n_steps = t // QKV_TM
    jobs = make_jobs(n_steps)
    return pl.pallas_call(
        functools.partial(_qkv_kernel, jobs=jobs),
        grid=(n_steps,),
        in_specs=[
            pl.BlockSpec((QKV_TM, D_MODEL), lambda i: (i, 0)),
            pl.BlockSpec((1, D_MODEL), lambda i: (0, 0)),
            pl.BlockSpec((D_MODEL, QKV_W), lambda i: (0, 0), pipeline_mode=pl.Buffered(1)),
        ] + [job.in_spec for job in jobs],
        out_specs=[
            pl.BlockSpec((QKV_TM, D_MODEL), lambda i: (i, 0)),
            pl.BlockSpec((QKV_TM, QKV_W), lambda i: (i, 0)),
        ] + [spec for job in jobs for spec in job.out_specs],
        out_shape=[jax.ShapeDtypeStruct((t, D_MODEL), BF16),
                   jax.ShapeDtypeStruct((t, QKV_W), BF16)] + [sh for job in jobs for sh in job.out_shapes],
        compiler_params=_compiler_params(1),
        name="qkv_proj",
    )(h, gain, w_qkv, *[job.src for job in jobs])


def _build_win_bias(tbl_ref):
    span = 3 * A_BLOCK
    qi = lax.broadcasted_iota(jnp.int32, (A_BLOCK, span), 0)
    kj = lax.broadcasted_iota(jnp.int32, (A_BLOCK, span), 1)
    absd_i = jnp.abs(qi + A_WINDOW - kj)
    absd = absd_i.astype(F32)
    in_window = absd_i <= A_WINDOW
    valid = (in_window & (kj >= A_BLOCK), in_window, in_window & (kj < 2 * A_BLOCK))
    for h, slope in enumerate(_alibi_slopes(A_Q_HEADS)):
        bias = (-slope * LOG2E) * absd
        for variant in range(3):
            tbl_ref[variant, h] = jnp.where(valid[variant], bias, NEG_INF)


def _attn_win_kernel(sink_ref, q_ref, kp_ref, kc_ref, kn_ref, vp_ref, vc_ref, vn_ref, u_ref, wg_ref, wup_ref,
                     bg_ref, o_ref, k_scr, v_scr, tbl_ref, a_scr):
    m = pl.program_id(1)

    @pl.when((pl.program_id(0) == 0) & (m == 0))
    def _():
        _build_win_bias(tbl_ref)

    k_scr[0:A_BLOCK, :] = kp_ref[...]
    k_scr[A_BLOCK:A_BLOCK + A_STEP, :] = kc_ref[...]
    k_scr[A_BLOCK + A_STEP:, :] = kn_ref[...]
    v_scr[0:A_BLOCK, :] = vp_ref[...]
    v_scr[A_BLOCK:A_BLOCK + A_STEP, :] = vc_ref[...]
    v_scr[A_BLOCK + A_STEP:, :] = vn_ref[...]
    span = 3 * A_BLOCK
    last_block = SEQ // A_BLOCK - 1
    piece = D_MODEL // (A_QB * A_KV_HEADS)

    def gate_piece(i):
        cols = slice(i * piece, (i + 1) * piece)
        gate = jnp.dot(u_ref[...], wg_ref[:, cols], preferred_element_type=F32) + bg_ref[:, cols]
        o_ref[:, cols] = jax.nn.sigmoid(gate)

    def sub_block(t):
        r = t * A_BLOCK
        block = m * A_QB + t
        variant = jnp.where(block == 0, 0, jnp.where(block == last_block, 2, 1))
        for g in range(A_KV_HEADS):
            k = k_scr[pl.ds(r, span), g * HEAD_DIM:(g + 1) * HEAD_DIM]
            v = v_scr[pl.ds(r, span), g * HEAD_DIM:(g + 1) * HEAD_DIM]
            heads = [g * A_GROUP + e for e in range(A_GROUP)]
            qs = jnp.concatenate(
                [q_ref[pl.ds(r, A_BLOCK), h * HEAD_DIM:(h + 1) * HEAD_DIM] for h in heads], axis=0)
            s = lax.dot_general(qs, k, (((1,), (1,)), ((), ())), preferred_element_type=F32)
            gate_piece(t * A_KV_HEADS + g)
            probs, dens = [], []
            for e, h in enumerate(heads):
                logits = s[e * A_BLOCK:(e + 1) * A_BLOCK, :] + tbl_ref[variant, h]
                sink = sink_ref[h] * LOG2E
                mx = jnp.maximum(jnp.max(logits, axis=-1, keepdims=True), sink)
                p = jnp.exp2(logits - mx)
                dens.append(jnp.sum(p, axis=-1, keepdims=True) + jnp.exp2(sink - mx))
                probs.append(p.astype(BF16))
            o = jnp.dot(jnp.concatenate(probs, axis=0), v, preferred_element_type=F32)
            for e, h in enumerate(heads):
                a_scr[pl.ds(r, A_BLOCK), h * HEAD_DIM:(h + 1) * HEAD_DIM] = (
                    o[e * A_BLOCK:(e + 1) * A_BLOCK, :] / dens[e]).astype(BF16)

    for t in range(A_QB):
        sub_block(t)
    attn = a_scr[...]
    for c in range(D_MODEL // GATE_TC):
        cols = slice(c * GATE_TC, (c + 1) * GATE_TC)
        o_ref[:, cols] = o_ref[:, cols] * jnp.dot(attn, wup_ref[:, cols], preferred_element_type=F32)


def _attn_win(qkv, sink, col_q, col_k, col_v, u, w_gate, w_up, b_gate):
    b = qkv.shape[0]
    n_steps = SEQ // A_STEP
    n_blocks = SEQ // A_BLOCK
    kcol, vcol = col_k // A_KV_W, col_v // A_KV_W

    def edge(col, shift):
        def index_map(bi, m):
            blk = jnp.clip(m * A_QB + shift, 0, n_blocks - 1)
            return (bi, blk, col)
        return pl.BlockSpec((None, A_BLOCK, A_KV_W), index_map)

    def centre(col):
        return pl.BlockSpec((None, A_STEP, A_KV_W), lambda bi, m: (bi, m, col))

    return pl.pallas_call(
        _attn_win_kernel,
        grid=(b, n_steps),
        in_specs=[
            pl.BlockSpec(memory_space=pltpu.SMEM),
            pl.BlockSpec((None, A_STEP, A_Q_W), lambda bi, m: (bi, m, col_q // A_Q_W)),
            edge(kcol, -1), centre(kcol), edge(kcol, A_QB),
            edge(vcol, -1), centre(vcol), edge(vcol, A_QB),
            pl.BlockSpec((None, A_STEP, D_MODEL), lambda bi, m: (bi, m, 0)),
            pl.BlockSpec((D_MODEL, D_MODEL), lambda bi, m: (0, 0)),
            pl.BlockSpec((A_Q_W, D_MODEL), lambda bi, m: (0, 0)),
            pl.BlockSpec((1, D_MODEL), lambda bi, m: (0, 0)),
        ],
        out_specs=pl.BlockSpec((None, A_STEP, D_MODEL), lambda bi, m: (bi, m, 0)),
        out_shape=jax.ShapeDtypeStruct((b, SEQ, D_MODEL), F32),
        scratch_shapes=[pltpu.VMEM((A_STEP + 2 * A_BLOCK, A_KV_W), BF16),
                        pltpu.VMEM((A_STEP + 2 * A_BLOCK, A_KV_W), BF16),
                        pltpu.VMEM((3, A_Q_HEADS, A_BLOCK, 3 * A_BLOCK), F32),
                        pltpu.VMEM((A_STEP, A_Q_W), BF16)],
        compiler_params=_compiler_params(2),
        name="attn_win",
    )(sink, qkv, qkv, qkv, qkv, qkv, qkv, qkv, u, w_gate, w_up, b_gate)


def _nbr_row_start(row):
    return min(max(row - B_WIN_R // 2, 0), GRID_ROWS - B_WIN_R)


def _nbr_key_row0(pair):
    return min(_nbr_row_start(pair * B_PAIR_ROWS), GRID_ROWS - B_KEY_ROWS)


def _build_nbr_bias(rpb_ref, tbl_ref, toep_l, toep_r):
    lanes = 2 * GRID_W
    qc = lax.broadcasted_iota(jnp.int32, (GRID_W, lanes), 0)
    lane = lax.broadcasted_iota(jnp.int32, (GRID_W, lanes), 1)
    col_start = jnp.clip(qc - B_WIN_C // 2, 0, GRID_W - B_WIN_C)
    left = lane < GRID_W
    neg = jnp.full((GRID_W, lanes), NEG_INF, F32)
    n_rel_r, n_rel_c = 2 * B_WIN_R - 1, 2 * B_WIN_C - 1

    def per_head(h, carry):
        for half, dst in ((0, toep_l), (1, toep_r)):
            kc = lane - half * GRID_W
            rel = kc - qc + (B_WIN_C - 1)
            col_ok = (kc >= col_start) & (kc < col_start + B_WIN_C) & (left if half == 0 else ~left)
            for a in range(n_rel_r):
                t = jnp.zeros((GRID_W, lanes), F32)
                for c in range(n_rel_c):
                    t = jnp.where(rel == c, rpb_ref[h, a * n_rel_c + c] * LOG2E, t)
                dst[a] = jnp.where(col_ok, t, neg)
        for vi, pair in enumerate(B_VARIANT_PAIRS):
            row0 = _nbr_key_row0(pair)
            for ql in range(B_PAIR_ROWS):
                q_row = pair * B_PAIR_ROWS + ql
                start = _nbr_row_start(q_row)
                for mt in range(B_KEY_ROWS // 2):
                    halves = []
                    for half, src in ((0, toep_l), (1, toep_r)):
                        k_row = row0 + 2 * mt + half
                        if start <= k_row < start + B_WIN_R:
                            halves.append(src[k_row - q_row + B_WIN_R - 1])
                        else:
                            halves.append(neg)
                    tbl_ref[vi, h, ql * GRID_W:(ql + 1) * GRID_W, mt * lanes:(mt + 1) * lanes] = (
                        jnp.where(left, halves[0], halves[1]))
        return carry

    lax.fori_loop(0, B_HEADS, per_head, 0)


def _nbr_window_block(step):
    return jnp.clip(step * (B_STEP // B_WINDOW_ALIGN) - 1, 0, (SEQ - B_WINDOW_TOK) // B_WINDOW_ALIGN)


def _attn_nbr_kernel(rpb_ref, q_ref, k_ref, v_ref, u_ref, wg_ref, wup_ref, bg_ref, sa_ref, o_ref,
                     tbl_ref, toep_l, toep_r, attn_scr, sig_scr):
    step = pl.program_id(1)

    @pl.when((pl.program_id(0) == 0) & (step == 0))
    def _():
        _build_nbr_bias(rpb_ref, tbl_ref, toep_l, toep_r)

    window_row0 = _nbr_window_block(step) * (B_WINDOW_ALIGN // GRID_W)
    piece = D_MODEL // B_PAIRS_PER_STEP
    head_cols = [slice(h * HEAD_DIM, (h + 1) * HEAD_DIM) for h in range(B_HEADS)]

    def gate_piece(i):
        cols = slice(i * piece, (i + 1) * piece)
        gate = jnp.dot(u_ref[...], wg_ref[:, cols], preferred_element_type=F32) + bg_ref[:, cols]
        sig_scr[:, cols] = jax.nn.sigmoid(gate)

    for pp in range(B_PAIRS_PER_STEP):
        pair = step * B_PAIRS_PER_STEP + pp
        row0 = jnp.clip(pair * B_PAIR_ROWS - B_WIN_R // 2, 0, GRID_ROWS - B_KEY_ROWS)
        variant = jnp.where(pair < B_EDGE_PAIRS, pair,
                            jnp.where(pair >= B_N_PAIRS - B_EDGE_PAIRS,
                                      pair - (B_N_PAIRS - len(B_VARIANT_PAIRS)), B_EDGE_PAIRS))
        k0 = pl.multiple_of((row0 - window_row0) * GRID_W, GRID_W)
        q_rows = slice(pp * B_PAIR_TOK, (pp + 1) * B_PAIR_TOK)
        scores = [
            lax.dot_general(q_ref[q_rows, cols], k_ref[0, pl.ds(k0, B_KEY_TOK), cols],
                            (((1,), (1,)), ((), ())), preferred_element_type=F32)
            for cols in head_cols]
        gate_piece(pp)
        s = jnp.concatenate(scores, axis=0)
        logits = s + tbl_ref[variant].reshape(B_HEADS * B_PAIR_TOK, B_KEY_TOK)
        mx = jnp.max(logits, axis=-1, keepdims=True)
        p = jnp.exp2(logits - mx)
        inv_den = 1.0 / jnp.sum(p, axis=-1, keepdims=True)
        p = p.astype(BF16)
        for h, cols in enumerate(head_cols):
            rows = slice(h * B_PAIR_TOK, (h + 1) * B_PAIR_TOK)
            o = jnp.dot(p[rows, :], v_ref[0, pl.ds(k0, B_KEY_TOK), cols], preferred_element_type=F32)
            attn_scr[q_rows, cols] = (o * inv_den[rows, :]).astype(BF16)

    attn = attn_scr[...]
    for c in range(D_MODEL // GATE_TC):
        cols = slice(c * GATE_TC, (c + 1) * GATE_TC)
        up = jnp.dot(attn, wup_ref[:, cols], preferred_element_type=F32)
        o_ref[:, cols] = (sa_ref[:, cols] + sig_scr[:, cols] * up).astype(BF16)


def _attn_nbr(qkv, rpb, col_q, col_k, col_v, u, w_gate, w_up, b_gate, sa):
    b = qkv.shape[0]
    assert B_WINDOW_ALIGN == (B_WIN_R // 2) * GRID_W and B_STEP % B_WINDOW_ALIGN == 0

    def window(col):
        return pl.BlockSpec((pl.Element(1), pl.Element(B_WINDOW_TOK), pl.Element(B_W)),
                            lambda bi, s: (bi, _nbr_window_block(s) * B_WINDOW_ALIGN, col))

    resident = dict(pipeline_mode=pl.Buffered(1))
    return pl.pallas_call(
        _attn_nbr_kernel,
        grid=(b, SEQ // B_STEP),
        in_specs=[
            pl.BlockSpec(memory_space=pltpu.SMEM),
            pl.BlockSpec((None, B_STEP, B_W), lambda bi, s: (bi, s, col_q // B_W)),
            window(col_k), window(col_v),
            pl.BlockSpec((None, B_STEP, D_MODEL), lambda bi, s: (bi, s, 0)),
            pl.BlockSpec((D_MODEL, D_MODEL), lambda bi, s: (0, 1), **resident),
            pl.BlockSpec((B_W, D_MODEL), lambda bi, s: (0, 0), **resident),
            pl.BlockSpec((1, D_MODEL), lambda bi, s: (0, 1)),
            pl.BlockSpec((None, B_STEP, D_MODEL), lambda bi, s: (bi, s, 0)),
        ],
        out_specs=pl.BlockSpec((None, B_STEP, D_MODEL), lambda bi, s: (bi, s, 0)),
        out_shape=jax.ShapeDtypeStruct((b, SEQ, D_MODEL), BF16),
        scratch_shapes=[
            pltpu.VMEM((len(B_VARIANT_PAIRS), B_HEADS, B_PAIR_TOK, B_KEY_TOK), F32),
            pltpu.VMEM((2 * B_WIN_R - 1, GRID_W, 2 * GRID_W), F32),
            pltpu.VMEM((2 * B_WIN_R - 1, GRID_W, 2 * GRID_W), F32),
            pltpu.VMEM((B_STEP, B_W), BF16),
            pltpu.VMEM((B_STEP, D_MODEL), F32),
        ],
        compiler_params=_compiler_params(2),
        name="attn_nbr",
    )(rpb, qkv, qkv, qkv, u, w_gate, w_up, b_gate, sa)


def _out_proj_kernel(m_ref, h_ref, w_ref, post_g_ref, o_ref):
    gain = post_g_ref[...]
    for r in range(OUT_TM // OUT_ROWS):
        y = jnp.dot(m_ref[r * OUT_ROWS:(r + 1) * OUT_ROWS, :], w_ref[...], preferred_element_type=F32)
        for c in range(OUT_ROWS // NORM_ROWS):
            rs = slice(r * OUT_ROWS + c * NORM_ROWS, r * OUT_ROWS + (c + 1) * NORM_ROWS)
            o_ref[rs, :] = h_ref[rs, :] + _rms_scale(y[c * NORM_ROWS:(c + 1) * NORM_ROWS, :]) * gain


def _out_proj(m, h, w_out, post_g):
    t = m.shape[0]
    return pl.pallas_call(
        _out_proj_kernel,
        grid=(t // OUT_TM,),
        in_specs=[
            pl.BlockSpec((OUT_TM, D_MODEL), lambda i: (i, 0)),
            pl.BlockSpec((OUT_TM, D_MODEL), lambda i: (i, 0)),
            pl.BlockSpec((D_MODEL, D_MODEL), lambda i: (0, 0), pipeline_mode=pl.Buffered(1)),
            pl.BlockSpec((1, D_MODEL), lambda i: (0, 0)),
        ],
        out_specs=pl.BlockSpec((OUT_TM, D_MODEL), lambda i: (i, 0)),
        out_shape=jax.ShapeDtypeStruct((t, D_MODEL), F32),
        compiler_params=_compiler_params(1),
        name="out_proj",
    )(m, h, w_out, post_g)


def _cast_mixer_w_in(src_ref, qkv_ref, gate_ref):
    for name in QKV_ORDER:
        lo, hi = QKV_SRC[name]
        blk = src_ref[:, lo:hi]
        if name in ("qa", "qb"):
            blk = blk * Q_PRESCALE
        qkv_ref[:, QKV_COL[name]:QKV_COL[name] + hi - lo] = blk.astype(BF16)
    gate_ref[...] = src_ref[:, QKV_W:].astype(BF16)


def kernel(x, ffn1_pre_g, ffn1_w_in, ffn1_w_out, ffn1_post_g, mix_pre_g, w_in, b_gate, sink_a, rpb_b,
           w_up_a, w_up_b, w_out, mix_post_g, ffn2_pre_g, ffn2_w_in, ffn2_w_out, ffn2_post_g):
    batch, seq, d = x.shape
    assert (seq, d) == (SEQ, D_MODEL)
    depth = ffn1_w_in.shape[0]
    tokens = batch * seq
    n_tiles, nf = tokens // FFN_TM, D_FF // FFN_FC
    h = x.reshape(tokens, d)

    def row(v):
        return v.reshape(1, -1).astype(F32)

    for l in range(depth):
        head, w1_gate, w1_up, w1_out = _ffn_head(h, row(ffn1_pre_g[l]), ffn1_w_in[l], ffn1_w_out[l],
                                                 row(ffn1_post_g[l]))
        jobs = (_tile_rows_job(w_in[l], n_tiles, nf, _cast_mixer_w_in, (QKV_W, 2 * D_MODEL)),)
        h, w_qkv, w_gate = _ffn(h, row(ffn1_pre_g[l]), w1_gate, w1_up, 0, w1_out, row(ffn1_post_g[l]), jobs, head)
        later = (ffn2_w_in[l], ffn2_w_out[l], w_out[l], w_up_a[l], w_up_b[l])
        u, qkv, w2_in, w2_out, w_o, w_ua, w_ub = _qkv_proj(
            h, row(mix_pre_g[l]), w_qkv, lambda n_steps: tuple(_row_slab_job(w, n_steps) for w in later))
        qkv = qkv.reshape(batch, seq, QKV_W)
        u = u.reshape(batch, seq, d)
        sa = _attn_win(qkv, sink_a[l].astype(F32), QKV_COL["qa"], QKV_COL["ka"], QKV_COL["va"],
                       u, w_gate, w_ua, row(b_gate[l]))
        mixed = _attn_nbr(qkv, rpb_b[l].astype(F32).reshape(B_HEADS, -1),
                          QKV_COL["qb"], QKV_COL["kb"], QKV_COL["vb"],
                          u, w_gate, w_ub, row(b_gate[l]), sa)
        h = _out_proj(mixed.reshape(tokens, d), h, w_o, row(mix_post_g[l]))
        h = _ffn_ring(h, row(ffn2_pre_g[l]), w2_in, w2_out, row(ffn2_post_g[l]))
    return h.reshape(batch, seq, d)
```

```python
import functools
import math
from typing import Callable, NamedTuple

import jax
import jax.numpy as jnp
import numpy as np
from jax import lax
from jax.experimental import pallas as pl
from jax.experimental.pallas import tpu as pltpu

D_MODEL = 2048
SEQ = 4096
HEAD_DIM = 128
A_Q_HEADS = 8
A_KV_HEADS = 2
A_GROUP = A_Q_HEADS // A_KV_HEADS
A_WINDOW = 128
A_BLOCK = 128
B_HEADS = 8
GRID_W = 64
GRID_ROWS = SEQ // GRID_W
B_WIN_R = 8
B_WIN_C = 16
D_FF = 5632
MACARON_W = 0.5
RMS_EPS = 1e-6
NEG_INF = -1e30
SCALE = HEAD_DIM ** -0.5
LOG2E = math.log2(math.e)
Q_PRESCALE = SCALE * LOG2E

A_Q_W = A_Q_HEADS * HEAD_DIM
A_KV_W = A_KV_HEADS * HEAD_DIM
B_W = B_HEADS * HEAD_DIM
QKV_W = A_Q_W + 2 * A_KV_W + 3 * B_W

_cuts = np.cumsum([0, A_Q_W, A_KV_W, A_KV_W, B_W, B_W, B_W]).tolist()
QKV_SRC = {name: (_cuts[i], _cuts[i + 1]) for i, name in enumerate(("qa", "ka", "va", "qb", "kb", "vb"))}
QKV_ORDER = ("qa", "qb", "kb", "vb", "ka", "va")
QKV_COL = {}
for _name in QKV_ORDER:
    QKV_COL[_name] = sum(QKV_SRC[n][1] - QKV_SRC[n][0] for n in QKV_ORDER[:QKV_ORDER.index(_name)])

F32 = jnp.float32
BF16 = jnp.bfloat16

VMEM_LIMIT_BYTES = 61 * 1024 * 1024

FFN_TM = 1024
FFN_FC = 512
FFN_ROWS = 256
HEAD_FC = 256
QKV_TM = 512
QKV_ROWS = 256
OUT_TM = 1024
OUT_ROWS = 256
NORM_ROWS = 32
CAST_ROWS = 16
MIXER_CAST_ROWS = 32

GATE_TC = 512
A_QB = 4
A_STEP = A_QB * A_BLOCK
B_PAIR_ROWS = 2
B_PAIR_TOK = B_PAIR_ROWS * GRID_W
B_KEY_ROWS = 10
B_KEY_TOK = B_KEY_ROWS * GRID_W
B_PAIRS_PER_STEP = 2
B_STEP = B_PAIRS_PER_STEP * B_PAIR_TOK
B_N_PAIRS = GRID_ROWS // B_PAIR_ROWS
B_WINDOW_ALIGN = 256
B_WINDOW_TOK = B_STEP + 2 * B_WINDOW_ALIGN
B_EDGE_PAIRS = B_WIN_R // 2 // B_PAIR_ROWS
B_VARIANT_PAIRS = (tuple(range(B_EDGE_PAIRS)) + (B_EDGE_PAIRS,)
                   + tuple(range(B_N_PAIRS - B_EDGE_PAIRS, B_N_PAIRS)))


def _alibi_slopes(n_heads):
    return [2.0 ** (-8.0 * (i + 1) / n_heads) for i in range(n_heads)]


def _compiler_params(n_axes):
    return pltpu.CompilerParams(dimension_semantics=("arbitrary",) * n_axes,
                                vmem_limit_bytes=VMEM_LIMIT_BYTES)


def _rms_scale(x):
    return x * lax.rsqrt(jnp.mean(x * x, axis=-1, keepdims=True) + RMS_EPS)


class CastJob(NamedTuple):
    src: jax.Array
    in_spec: pl.BlockSpec
    out_specs: tuple
    out_shapes: tuple
    body: Callable


def _cast_block(src_ref, dst_ref):
    dst_ref[...] = src_ref[...].astype(BF16)


def _tile_rows_job(src, n_tiles, n_steps, body=_cast_block, out_cols=None, block_rows=CAST_ROWS):
    rows, cols = src.shape
    n_blocks = rows // block_rows
    per_tile = -(-n_blocks // n_tiles)
    assert n_blocks * block_rows == rows and block_rows % CAST_ROWS == 0 and per_tile <= n_steps
    index_map = lambda i, j: (jnp.minimum(i * per_tile + jnp.minimum(j, per_tile - 1), n_blocks - 1), 0)
    out_cols = (cols,) if out_cols is None else out_cols
    return CastJob(src, pl.BlockSpec((block_rows, cols), index_map),
                   tuple(pl.BlockSpec((block_rows, c), index_map) for c in out_cols),
                   tuple(jax.ShapeDtypeStruct((rows, c), BF16) for c in out_cols), body)


def _row_slab_job(src, n_steps):
    rows, cols = src.shape
    slab = rows // n_steps
    assert slab * n_steps == rows and slab % CAST_ROWS == 0
    spec = pl.BlockSpec((slab, cols), lambda i: (i, 0))
    return CastJob(src, spec, (spec,), (jax.ShapeDtypeStruct((rows, cols), BF16),), _cast_block)


def _run_jobs(jobs, src_refs, dst_refs):
    first = 0
    for job, src_ref in zip(jobs, src_refs):
        job.body(src_ref, *dst_refs[first:first + len(job.out_specs)])
        first += len(job.out_specs)


def _ffn_kernel(*refs, jobs, has_head):
    x_ref, pre_g_ref, wg_ref, wu_ref, wo_ref, post_g_ref = refs[:6]
    n_in = 6 + has_head
    job_srcs = refs[n_in:n_in + len(jobs)]
    o_ref = refs[n_in + len(jobs)]
    job_dsts = refs[n_in + 1 + len(jobs):-1]
    n_ref = refs[-1]
    j = pl.program_id(1)
    last = pl.num_programs(1) - 1
    row_chunks = [slice(r * FFN_ROWS, (r + 1) * FFN_ROWS) for r in range(FFN_TM // FFN_ROWS)]

    def swiglu(n):
        g = jnp.dot(n, wg_ref[...], preferred_element_type=F32)
        u = jnp.dot(n, wu_ref[...], preferred_element_type=F32)
        a = (g * jax.nn.sigmoid(g) * u).astype(BF16)
        return jnp.dot(a, wo_ref[...], preferred_element_type=F32)

    def run_jobs():
        _run_jobs(jobs, job_srcs, job_dsts)

    @pl.when(j == 0)
    def _():
        gain = pre_g_ref[...]
        for rows in row_chunks:
            for c in range(rows.start, rows.stop, NORM_ROWS):
                rs = slice(c, c + NORM_ROWS)
                n_ref[rs, :] = (_rms_scale(x_ref[rs, :]) * gain).astype(BF16)
            o_ref[rows, :] = swiglu(n_ref[rows, :])
        run_jobs()

    @pl.when((j > 0) & (j < last))
    def _():
        o_ref[...] += swiglu(n_ref[...])
        run_jobs()

    @pl.when(j == last)
    def _():
        gain = MACARON_W * post_g_ref[...]
        for rows in row_chunks:
            f = o_ref[rows, :] + swiglu(n_ref[rows, :])
            for c in range(0, FFN_ROWS, NORM_ROWS):
                rs = slice(rows.start + c, rows.start + c + NORM_ROWS)
                o_ref[rs, :] = x_ref[rs, :] + _rms_scale(f[c:c + NORM_ROWS, :]) * gain
        run_jobs()


def _ffn_head_kernel(x_ref, pre_g_ref, wg_ref, wu_ref, wo_ref, post_g_ref,
                     o_ref, wg_bf_ref, wu_bf_ref, wo_bf_ref, n_ref):
    j = pl.program_id(0)
    row_chunks = [slice(r, r + NORM_ROWS) for r in range(0, FFN_TM, NORM_ROWS)]

    @pl.when(j == 0)
    def _():
        gain = pre_g_ref[...]
        for rs in row_chunks:
            n_ref[rs, :] = (_rms_scale(x_ref[rs, :]) * gain).astype(BF16)
        o_ref[...] = jnp.zeros_like(o_ref)

    wg, wu, wo = (ref[...].astype(BF16) for ref in (wg_ref, wu_ref, wo_ref))
    wg_bf_ref[...] = wg
    wu_bf_ref[...] = wu
    wo_bf_ref[...] = wo
    n = n_ref[...]
    g = jnp.dot(n, wg, preferred_element_type=F32)
    u = jnp.dot(n, wu, preferred_element_type=F32)
    a = (g * jax.nn.sigmoid(g) * u).astype(BF16)
    o_ref[...] += jnp.dot(a, wo, preferred_element_type=F32)

    @pl.when(j == pl.num_programs(0) - 1)
    def _():
        gain = MACARON_W * post_g_ref[...]
        for rs in row_chunks:
            o_ref[rs, :] = x_ref[rs, :] + _rms_scale(o_ref[rs, :]) * gain


def _ffn_head(x, pre_g, w_in, w_out, post_g):
    n_steps = D_FF // HEAD_FC
    return pl.pallas_call(
        _ffn_head_kernel,
        grid=(n_steps,),
        in_specs=[
            pl.BlockSpec((FFN_TM, D_MODEL), lambda j: (0, 0)),
            pl.BlockSpec((1, D_MODEL), lambda j: (0, 0)),
            pl.BlockSpec((D_MODEL, HEAD_FC), lambda j: (0, j)),
            pl.BlockSpec((D_MODEL, HEAD_FC), lambda j: (0, j + n_steps)),
            pl.BlockSpec((HEAD_FC, D_MODEL), lambda j: (j, 0)),
            pl.BlockSpec((1, D_MODEL), lambda j: (0, 0)),
        ],
        out_specs=[
            pl.BlockSpec((FFN_TM, D_MODEL), lambda j: (0, 0)),
            pl.BlockSpec((D_MODEL, HEAD_FC), lambda j: (0, j)),
            pl.BlockSpec((D_MODEL, HEAD_FC), lambda j: (0, j)),
            pl.BlockSpec((HEAD_FC, D_MODEL), lambda j: (j, 0)),
        ],
        out_shape=[jax.ShapeDtypeStruct(x.shape, F32),
                   jax.ShapeDtypeStruct((D_MODEL, D_FF), BF16),
                   jax.ShapeDtypeStruct((D_MODEL, D_FF), BF16),
                   jax.ShapeDtypeStruct((D_FF, D_MODEL), BF16)],
        scratch_shapes=[pltpu.VMEM((FFN_TM, D_MODEL), BF16)],
        compiler_params=_compiler_params(1),
        name="ffn_head",
    )(x, pre_g, w_in, w_in, w_out, post_g)


def _ffn(x, pre_g, w_gate, w_up, up_block0, w_out, post_g, jobs=(), head=None):
    t = x.shape[0]
    nf = D_FF // FFN_FC
    assert nf >= 2
    has_head = head is not None
    tile0 = 1 if has_head else 0
    out_specs = [pl.BlockSpec((FFN_TM, D_MODEL), lambda i, j: (i + tile0, 0))]
    out_shapes = [jax.ShapeDtypeStruct((t, D_MODEL), F32)]
    for job in jobs:
        out_specs.extend(job.out_specs)
        out_shapes.extend(job.out_shapes)
    return pl.pallas_call(
        functools.partial(_ffn_kernel, jobs=jobs, has_head=has_head),
        grid=(t // FFN_TM - tile0, nf),
        in_specs=[
            pl.BlockSpec((FFN_TM, D_MODEL), lambda i, j: (i + tile0, 0)),
            pl.BlockSpec((1, D_MODEL), lambda i, j: (0, 0)),
            pl.BlockSpec((D_MODEL, FFN_FC), lambda i, j: (0, j)),
            pl.BlockSpec((D_MODEL, FFN_FC), lambda i, j: (0, up_block0 + j)),
            pl.BlockSpec((FFN_FC, D_MODEL), lambda i, j: (j, 0)),
            pl.BlockSpec((1, D_MODEL), lambda i, j: (0, 0)),
        ] + ([pl.BlockSpec(memory_space=pl.ANY)] if has_head else []) + [job.in_spec for job in jobs],
        out_specs=out_specs,
        out_shape=out_shapes,
        input_output_aliases={6: 0} if has_head else {},
        scratch_shapes=[pltpu.VMEM((FFN_TM, D_MODEL), BF16)],
        compiler_params=_compiler_params(2),
        name="ffn",
    )(x, pre_g, w_gate, w_up, w_out, post_g, *([head] if has_head else []), *[job.src for job in jobs])


def _qkv_kernel(*refs, jobs):
    h_ref, g_ref, w_ref = refs[:3]
    job_srcs = refs[3:3 + len(jobs)]
    u_ref, o_ref = refs[3 + len(jobs):5 + len(jobs)]
    gain = g_ref[...]
    for r in range(0, QKV_TM, QKV_ROWS):
        for c in range(r, r + QKV_ROWS, NORM_ROWS):
            rs = slice(c, c + NORM_ROWS)
            u_ref[rs, :] = (_rms_scale(h_ref[rs, :]) * gain).astype(BF16)
        rows = slice(r, r + QKV_ROWS)
        o_ref[rows, :] = jnp.dot(u_ref[rows, :], w_ref[...], preferred_element_type=F32).astype(BF16)
    _run_jobs(jobs, job_srcs, refs[5 + len(jobs):])


def _qkv_proj(h, gain, w_qkv, make_jobs):
    t = h.shape[0]
    n_steps = t // QKV_TM
    jobs = make_jobs(n_steps)
    return pl.pallas_call(
        functools.partial(_qkv_kernel, jobs=jobs),
        grid=(n_steps,),
        in_specs=[
            pl.BlockSpec((QKV_TM, D_MODEL), lambda i: (i, 0)),
            pl.BlockSpec((1, D_MODEL), lambda i: (0, 0)),
            pl.BlockSpec((D_MODEL, QKV_W), lambda i: (0, 0), pipeline_mode=pl.Buffered(1)),
        ] + [job.in_spec for job in jobs],
        out_specs=[
            pl.BlockSpec((QKV_TM, D_MODEL), lambda i: (i, 0)),
            pl.BlockSpec((QKV_TM, QKV_W), lambda i: (i, 0)),
        ] + [spec for job in jobs for spec in job.out_specs],
        out_shape=[jax.ShapeDtypeStruct((t, D_MODEL), BF16),
                   jax.ShapeDtypeStruct((t, QKV_W), BF16)] + [sh for job in jobs for sh in job.out_shapes],
        compiler_params=_compiler_params(1),
        name="qkv_proj",
    )(h, gain, w_qkv, *[job.src for job in jobs])


def _build_win_bias(tbl_ref):
    span = 3 * A_BLOCK
    qi = lax.broadcasted_iota(jnp.int32, (A_BLOCK, span), 0)
    kj = lax.broadcasted_iota(jnp.int32, (A_BLOCK, span), 1)
    absd_i = jnp.abs(qi + A_WINDOW - kj)
    absd = absd_i.astype(F32)
    in_window = absd_i <= A_WINDOW
    valid = (in_window & (kj >= A_BLOCK), in_window, in_window & (kj < 2 * A_BLOCK))
    for h, slope in enumerate(_alibi_slopes(A_Q_HEADS)):
        bias = (-slope * LOG2E) * absd
        for variant in range(3):
            tbl_ref[variant, h] = jnp.where(valid[variant], bias, NEG_INF)


def _attn_win_kernel(sink_ref, q_ref, kp_ref, kc_ref, kn_ref, vp_ref, vc_ref, vn_ref, u_ref, wg_ref, wup_ref,
                     bg_ref, o_ref, k_scr, v_scr, tbl_ref, a_scr):
    m = pl.program_id(1)

    @pl.when((pl.program_id(0) == 0) & (m == 0))
    def _():
        _build_win_bias(tbl_ref)

    k_scr[0:A_BLOCK, :] = kp_ref[...]
    k_scr[A_BLOCK:A_BLOCK + A_STEP, :] = kc_ref[...]
    k_scr[A_BLOCK + A_STEP:, :] = kn_ref[...]
    v_scr[0:A_BLOCK, :] = vp_ref[...]
    v_scr[A_BLOCK:A_BLOCK + A_STEP, :] = vc_ref[...]
    v_scr[A_BLOCK + A_STEP:, :] = vn_ref[...]
    span = 3 * A_BLOCK
    last_block = SEQ // A_BLOCK - 1
    piece = D_MODEL // (A_QB * A_KV_HEADS)

    def gate_piece(i):
        cols = slice(i * piece, (i + 1) * piece)
        gate = jnp.dot(u_ref[...], wg_ref[:, cols], preferred_element_type=F32) + bg_ref[:, cols]
        o_ref[:, cols] = jax.nn.sigmoid(gate)

    def sub_block(t):
        r = t * A_BLOCK
        block = m * A_QB + t
        variant = jnp.where(block == 0, 0, jnp.where(block == last_block, 2, 1))
        for g in range(A_KV_HEADS):
            k = k_scr[pl.ds(r, span), g * HEAD_DIM:(g + 1) * HEAD_DIM]
            v = v_scr[pl.ds(r, span), g * HEAD_DIM:(g + 1) * HEAD_DIM]
            heads = [g * A_GROUP + e for e in range(A_GROUP)]
            qs = jnp.concatenate(
                [q_ref[pl.ds(r, A_BLOCK), h * HEAD_DIM:(h + 1) * HEAD_DIM] for h in heads], axis=0)
            s = lax.dot_general(qs, k, (((1,), (1,)), ((), ())), preferred_element_type=F32)
            gate_piece(t * A_KV_HEADS + g)
            probs, dens = [], []
            for e, h in enumerate(heads):
                logits = s[e * A_BLOCK:(e + 1) * A_BLOCK, :] + tbl_ref[variant, h]
                sink = sink_ref[h] * LOG2E
                mx = jnp.maximum(jnp.max(logits, axis=-1, keepdims=True), sink)
                p = jnp.exp2(logits - mx)
                dens.append(jnp.sum(p, axis=-1, keepdims=True) + jnp.exp2(sink - mx))
                probs.append(p.astype(BF16))
            o = jnp.dot(jnp.concatenate(probs, axis=0), v, preferred_element_type=F32)
            for e, h in enumerate(heads):
                a_scr[pl.ds(r, A_BLOCK), h * HEAD_DIM:(h + 1) * HEAD_DIM] = (
                    o[e * A_BLOCK:(e + 1) * A_BLOCK, :] / dens[e]).astype(BF16)

    for t in range(A_QB):
        sub_block(t)
    attn = a_scr[...]
    for c in range(D_MODEL // GATE_TC):
        cols = slice(c * GATE_TC, (c + 1) * GATE_TC)
        o_ref[:, cols] = o_ref[:, cols] * jnp.dot(attn, wup_ref[:, cols], preferred_element_type=F32)


def _attn_win(qkv, sink, col_q, col_k, col_v, u, w_gate, w_up, b_gate):
    b = qkv.shape[0]
    n_steps = SEQ // A_STEP
    n_blocks = SEQ // A_BLOCK
    kcol, vcol = col_k // A_KV_W, col_v // A_KV_W

    def edge(col, shift):
        def index_map(bi, m):
            blk = jnp.clip(m * A_QB + shift, 0, n_blocks - 1)
            return (bi, blk, col)
        return pl.BlockSpec((None, A_BLOCK, A_KV_W), index_map)

    def centre(col):
        return pl.BlockSpec((None, A_STEP, A_KV_W), lambda bi, m: (bi, m, col))

    return pl.pallas_call(
        _attn_win_kernel,
        grid=(b, n_steps),
        in_specs=[
            pl.BlockSpec(memory_space=pltpu.SMEM),
            pl.BlockSpec((None, A_STEP, A_Q_W), lambda bi, m: (bi, m, col_q // A_Q_W)),
            edge(kcol, -1), centre(kcol), edge(kcol, A_QB),
            edge(vcol, -1), centre(vcol), edge(vcol, A_QB),
            pl.BlockSpec((None, A_STEP, D_MODEL), lambda bi, m: (bi, m, 0)),
            pl.BlockSpec((D_MODEL, D_MODEL), lambda bi, m: (0, 0)),
            pl.BlockSpec((A_Q_W, D_MODEL), lambda bi, m: (0, 0)),
            pl.BlockSpec((1, D_MODEL), lambda bi, m: (0, 0)),
        ],
        out_specs=pl.BlockSpec((None, A_STEP, D_MODEL), lambda bi, m: (bi, m, 0)),
        out_shape=jax.ShapeDtypeStruct((b, SEQ, D_MODEL), F32),
        scratch_shapes=[pltpu.VMEM((A_STEP + 2 * A_BLOCK, A_KV_W), BF16),
                        pltpu.VMEM((A_STEP + 2 * A_BLOCK, A_KV_W), BF16),
                        pltpu.VMEM((3, A_Q_HEADS, A_BLOCK, 3 * A_BLOCK), F32),
                        pltpu.VMEM((A_STEP, A_Q_W), BF16)],
        compiler_params=_compiler_params(2),
        name="attn_win",
    )(sink, qkv, qkv, qkv, qkv, qkv, qkv, qkv, u, w_gate, w_up, b_gate)


def _nbr_row_start(row):
    return min(max(row - B_WIN_R // 2, 0), GRID_ROWS - B_WIN_R)


def _nbr_key_row0(pair):
    return min(_nbr_row_start(pair * B_PAIR_ROWS), GRID_ROWS - B_KEY_ROWS)


def _build_nbr_bias(rpb_ref, tbl_ref, toep_l, toep_r):
    lanes = 2 * GRID_W
    qc = lax.broadcasted_iota(jnp.int32, (GRID_W, lanes), 0)
    lane = lax.broadcasted_iota(jnp.int32, (GRID_W, lanes), 1)
    col_start = jnp.clip(qc - B_WIN_C // 2, 0, GRID_W - B_WIN_C)
    left = lane < GRID_W
    neg = jnp.full((GRID_W, lanes), NEG_INF, F32)
    n_rel_r, n_rel_c = 2 * B_WIN_R - 1, 2 * B_WIN_C - 1

    def per_head(h, carry):
        for half, dst in ((0, toep_l), (1, toep_r)):
            kc = lane - half * GRID_W
            rel = kc - qc + (B_WIN_C - 1)
            col_ok = (kc >= col_start) & (kc < col_start + B_WIN_C) & (left if half == 0 else ~left)
            for a in range(n_rel_r):
                t = jnp.zeros((GRID_W, lanes), F32)
                for c in range(n_rel_c):
                    t = jnp.where(rel == c, rpb_ref[h, a * n_rel_c + c] * LOG2E, t)
                dst[a] = jnp.where(col_ok, t, neg)
        for vi, pair in enumerate(B_VARIANT_PAIRS):
            row0 = _nbr_key_row0(pair)
            for ql in range(B_PAIR_ROWS):
                q_row = pair * B_PAIR_ROWS + ql
                start = _nbr_row_start(q_row)
                for mt in range(B_KEY_ROWS // 2):
                    halves = []
                    for half, src in ((0, toep_l), (1, toep_r)):
                        k_row = row0 + 2 * mt + half
                        if start <= k_row < start + B_WIN_R:
                            halves.append(src[k_row - q_row + B_WIN_R - 1])
                        else:
                            halves.append(neg)
                    tbl_ref[vi, h, ql * GRID_W:(ql + 1) * GRID_W, mt * lanes:(mt + 1) * lanes] = (
                        jnp.where(left, halves[0], halves[1]))
        return carry

    lax.fori_loop(0, B_HEADS, per_head, 0)


def _nbr_window_block(step):
    return jnp.clip(step * (B_STEP // B_WINDOW_ALIGN) - 1, 0, (SEQ - B_WINDOW_TOK) // B_WINDOW_ALIGN)


def _attn_nbr_kernel(rpb_ref, q_ref, k_ref, v_ref, u_ref, wg_ref, wup_ref, bg_ref, sa_ref, o_ref,
                     tbl_ref, toep_l, toep_r, attn_scr, sig_scr):
    step = pl.program_id(1)

    @pl.when((pl.program_id(0) == 0) & (step == 0))
    def _():
        _build_nbr_bias(rpb_ref, tbl_ref, toep_l, toep_r)

    window_row0 = _nbr_window_block(step) * (B_WINDOW_ALIGN // GRID_W)
    piece = D_MODEL // B_PAIRS_PER_STEP
    head_cols = [slice(h * HEAD_DIM, (h + 1) * HEAD_DIM) for h in range(B_HEADS)]

    def gate_piece(i):
        cols = slice(i * piece, (i + 1) * piece)
        gate = jnp.dot(u_ref[...], wg_ref[:, cols], preferred_element_type=F32) + bg_ref[:, cols]
        sig_scr[:, cols] = jax.nn.sigmoid(gate)

    for pp in range(B_PAIRS_PER_STEP):
        pair = step * B_PAIRS_PER_STEP + pp
        row0 = jnp.clip(pair * B_PAIR_ROWS - B_WIN_R // 2, 0, GRID_ROWS - B_KEY_ROWS)
        variant = jnp.where(pair < B_EDGE_PAIRS, pair,
                            jnp.where(pair >= B_N_PAIRS - B_EDGE_PAIRS,
                                      pair - (B_N_PAIRS - len(B_VARIANT_PAIRS)), B_EDGE_PAIRS))
        k0 = pl.multiple_of((row0 - window_row0) * GRID_W, GRID_W)
        q_rows = slice(pp * B_PAIR_TOK, (pp + 1) * B_PAIR_TOK)
        scores = [
            lax.dot_general(q_ref[q_rows, cols], k_ref[0, pl.ds(k0, B_KEY_TOK), cols],
                            (((1,), (1,)), ((), ())), preferred_element_type=F32)
            for cols in head_cols]
        gate_piece(pp)
        s = jnp.concatenate(scores, axis=0)
        logits = s + tbl_ref[variant].reshape(B_HEADS * B_PAIR_TOK, B_KEY_TOK)
        mx = jnp.max(logits, axis=-1, keepdims=True)
        p = jnp.exp2(logits - mx)
        inv_den = 1.0 / jnp.sum(p, axis=-1, keepdims=True)
        p = p.astype(BF16)
        for h, cols in enumerate(head_cols):
            rows = slice(h * B_PAIR_TOK, (h + 1) * B_PAIR_TOK)
            o = jnp.dot(p[rows, :], v_ref[0, pl.ds(k0, B_KEY_TOK), cols], preferred_element_type=F32)
            attn_scr[q_rows, cols] = (o * inv_den[rows, :]).astype(BF16)

    attn = attn_scr[...]
    for c in range(D_MODEL // GATE_TC):
        cols = slice(c * GATE_TC, (c + 1) * GATE_TC)
        up = jnp.dot(attn, wup_ref[:, cols], preferred_element_type=F32)
        o_ref[:, cols] = (sa_ref[:, cols] + sig_scr[:, cols] * up).astype(BF16)


def _attn_nbr(qkv, rpb, col_q, col_k, col_v, u, w_gate, w_up, b_gate, sa):
    b = qkv.shape[0]
    assert B_WINDOW_ALIGN == (B_WIN_R // 2) * GRID_W and B_STEP % B_WINDOW_ALIGN == 0

    def window(col):
        return pl.BlockSpec((pl.Element(1), pl.Element(B_WINDOW_TOK), pl.Element(B_W)),
                            lambda bi, s: (bi, _nbr_window_block(s) * B_WINDOW_ALIGN, col))

    resident = dict(pipeline_mode=pl.Buffered(1))
    return pl.pallas_call(
        _attn_nbr_kernel,
        grid=(b, SEQ // B_STEP),
        in_specs=[
            pl.BlockSpec(memory_space=pltpu.SMEM),
            pl.BlockSpec((None, B_STEP, B_W), lambda bi, s: (bi, s, col_q // B_W)),
            window(col_k), window(col_v),
            pl.BlockSpec((None, B_STEP, D_MODEL), lambda bi, s: (bi, s, 0)),
            pl.BlockSpec((D_MODEL, D_MODEL), lambda bi, s: (0, 1), **resident),
            pl.BlockSpec((B_W, D_MODEL), lambda bi, s: (0, 0), **resident),
            pl.BlockSpec((1, D_MODEL), lambda bi, s: (0, 1)),
            pl.BlockSpec((None, B_STEP, D_MODEL), lambda bi, s: (bi, s, 0)),
        ],
        out_specs=pl.BlockSpec((None, B_STEP, D_MODEL), lambda bi, s: (bi, s, 0)),
        out_shape=jax.ShapeDtypeStruct((b, SEQ, D_MODEL), BF16),
        scratch_shapes=[
            pltpu.VMEM((len(B_VARIANT_PAIRS), B_HEADS, B_PAIR_TOK, B_KEY_TOK), F32),
            pltpu.VMEM((2 * B_WIN_R - 1, GRID_W, 2 * GRID_W), F32),
            pltpu.VMEM((2 * B_WIN_R - 1, GRID_W, 2 * GRID_W), F32),
            pltpu.VMEM((B_STEP, B_W), BF16),
            pltpu.VMEM((B_STEP, D_MODEL), F32),
        ],
        compiler_params=_compiler_params(2),
        name="attn_nbr",
    )(rpb, qkv, qkv, qkv, u, w_gate, w_up, b_gate, sa)


def _out_proj_kernel(m_ref, h_ref, w_ref, post_g_ref, o_ref):
    gain = post_g_ref[...]
    for r in range(OUT_TM // OUT_ROWS):
        y = jnp.dot(m_ref[r * OUT_ROWS:(r + 1) * OUT_ROWS, :], w_ref[...], preferred_element_type=F32)
        for c in range(OUT_ROWS // NORM_ROWS):
            rs = slice(r * OUT_ROWS + c * NORM_ROWS, r * OUT_ROWS + (c + 1) * NORM_ROWS)
            o_ref[rs, :] = h_ref[rs, :] + _rms_scale(y[c * NORM_ROWS:(c + 1) * NORM_ROWS, :]) * gain


def _out_proj(m, h, w_out, post_g):
    t = m.shape[0]
    return pl.pallas_call(
        _out_proj_kernel,
        grid=(t // OUT_TM,),
        in_specs=[
            pl.BlockSpec((OUT_TM, D_MODEL), lambda i: (i, 0)),
            pl.BlockSpec((OUT_TM, D_MODEL), lambda i: (i, 0)),
            pl.BlockSpec((D_MODEL, D_MODEL), lambda i: (0, 0), pipeline_mode=pl.Buffered(1)),
            pl.BlockSpec((1, D_MODEL), lambda i: (0, 0)),
        ],
        out_specs=pl.BlockSpec((OUT_TM, D_MODEL), lambda i: (i, 0)),
        out_shape=jax.ShapeDtypeStruct((t, D_MODEL), F32),
        compiler_params=_compiler_params(1),
        name="out_proj",
    )(m, h, w_out, post_g)


def _cast_mixer_w_in(src_ref, qkv_ref, gate_ref):
    for name in QKV_ORDER:
        lo, hi = QKV_SRC[name]
        blk = src_ref[:, lo:hi]
        if name in ("qa", "qb"):
            blk = blk * Q_PRESCALE
        qkv_ref[:, QKV_COL[name]:QKV_COL[name] + hi - lo] = blk.astype(BF16)
    gate_ref[...] = src_ref[:, QKV_W:].astype(BF16)


def kernel(x, ffn1_pre_g, ffn1_w_in, ffn1_w_out, ffn1_post_g, mix_pre_g, w_in, b_gate, sink_a, rpb_b,
           w_up_a, w_up_b, w_out, mix_post_g, ffn2_pre_g, ffn2_w_in, ffn2_w_out, ffn2_post_g):
    batch, seq, d = x.shape
    assert (seq, d) == (SEQ, D_MODEL)
    depth = ffn1_w_in.shape[0]
    tokens = batch * seq
    n_tiles, nf = tokens // FFN_TM, D_FF // FFN_FC
    h = x.reshape(tokens, d)

    def row(v):
        return v.reshape(1, -1).astype(F32)

    for l in range(depth):
        head, w1_gate, w1_up, w1_out = _ffn_head(h, row(ffn1_pre_g[l]), ffn1_w_in[l], ffn1_w_out[l],
                                                 row(ffn1_post_g[l]))
        jobs = (_tile_rows_job(w_in[l], n_tiles - 1, nf, _cast_mixer_w_in, (QKV_W, 2 * D_MODEL), MIXER_CAST_ROWS),)
        h, w_qkv, w_gate = _ffn(h, row(ffn1_pre_g[l]), w1_gate, w1_up, 0, w1_out, row(ffn1_post_g[l]), jobs, head)
        later = (ffn2_w_in[l], ffn2_w_out[l], w_out[l], w_up_a[l], w_up_b[l])
        u, qkv, w2_in, w2_out, w_o, w_ua, w_ub = _qkv_proj(
            h, row(mix_pre_g[l]), w_qkv, lambda n_steps: tuple(_row_slab_job(w, n_steps) for w in later))
        qkv = qkv.reshape(batch, seq, QKV_W)
        u = u.reshape(batch, seq, d)
        sa = _attn_win(qkv, sink_a[l].astype(F32), QKV_COL["qa"], QKV_COL["ka"], QKV_COL["va"],
                       u, w_gate, w_ua, row(b_gate[l]))
        mixed = _attn_nbr(qkv, rpb_b[l].astype(F32).reshape(B_HEADS, -1),
                          QKV_COL["qb"], QKV_COL["kb"], QKV_COL["vb"],
                          u, w_gate, w_ub, row(b_gate[l]), sa)
        h = _out_proj(mixed.reshape(tokens, d), h, w_o, row(mix_post_g[l]))
        h, = _ffn(h, row(ffn2_pre_g[l]), w2_in, w2_in, nf, w2_out, row(ffn2_post_g[l]))
    return h.reshape(batch, seq, d)
```
